```python
import math
import jax, jax.numpy as jnp
from jax import lax
import numpy as np

D_MODEL = 2048
BATCH = 4
SEQ = 8192
DEPTH = 1
DEC_BATCH = 32
DEC_SEQ = 16
PAST_LEN = 1024

CHUNK = 64
D_A = D_MODEL // 2
S5_GROUP = 16
G_A = D_A // S5_GROUP
N_STATE = 64
D_B = D_MODEL // 2
SC_WIDTH = 3
D_FF = 5504
FFN_WIDTH = 3
EPS = 1e-6
IN_COLS = D_A + 3 * D_B + 2 * D_MODEL

kernel_name = "hybrid_s5_shortconv_streaming_step"


def rmsnorm(x, g):
    xf = x.astype(jnp.float32)
    ms = jnp.mean(xf * xf, axis=-1, keepdims=True)
    return (xf * lax.rsqrt(ms + EPS) * g.astype(jnp.float32)).astype(x.dtype)


def causal_dwconv(x, buf, w, b):
    width = w.shape[0]
    L = x.shape[1]
    xp = jnp.concatenate([buf.astype(x.dtype), x], axis=1)
    y = xp[:, 0:L] * w[0]
    for k in range(1, width):
        y = y + xp[:, k:k + L] * w[k]
    return y + b, xp[:, -(width - 1):]


def s5_discretize(lam_re, lam_im, log_dt, b_re, b_im):
    f32 = jnp.float32
    dt = jnp.exp(log_dt.astype(f32))[:, None]
    lr, li = lam_re.astype(f32), lam_im.astype(f32)
    mag = jnp.exp(lr * dt)
    ab_re = mag * jnp.cos(li * dt)
    ab_im = mag * jnp.sin(li * dt)
    den = lr * lr + li * li
    nr, ni = ab_re - 1.0, ab_im
    f_re = (nr * lr + ni * li) / den
    f_im = (ni * lr - nr * li) / den
    br, bi = b_re.astype(f32), b_im.astype(f32)
    bb_re = f_re[..., None] * br - f_im[..., None] * bi
    bb_im = f_re[..., None] * bi + f_im[..., None] * br
    return ab_re, ab_im, bb_re, bb_im


def _complex_affine_combine(e1, e2):
    a1r, a1i, b1r, b1i = e1
    a2r, a2i, b2r, b2i = e2
    ar = a2r * a1r - a2i * a1i
    ai = a2r * a1i + a2i * a1r
    br = a2r * b1r - a2i * b1i + b2r
    bi = a2r * b1i + a2i * b1r + b2i
    return (ar, ai, br, bi)


def s5_block(u, h_re, h_im, ab_re, ab_im, bb_re, bb_im, c_re, c_im, d_skip):
    bu_re = jnp.einsum('blgp,gnp->blgn', u, bb_re)
    bu_im = jnp.einsum('blgp,gnp->blgn', u, bb_im)
    a_re = jnp.broadcast_to(ab_re, bu_re.shape)
    a_im = jnp.broadcast_to(ab_im, bu_re.shape)
    acr, aci, bcr, bci = lax.associative_scan(_complex_affine_combine, (a_re, a_im, bu_re, bu_im), axis=1)
    h0r, h0i = h_re[:, None], h_im[:, None]
    st_re = acr * h0r - aci * h0i + bcr
    st_im = acr * h0i + aci * h0r + bci
    y = (jnp.einsum('blgn,gpn->blgp', st_re, c_re) - jnp.einsum('blgn,gpn->blgp', st_im, c_im)
         + d_skip * u)
    return y, st_re[:, -1], st_im[:, -1]


def s5_mixer(u, h_re, h_im, disc, c_re, c_im, d_skip):
    f32 = jnp.float32
    ab_re, ab_im, bb_re, bb_im = disc
    cr, ci = c_re.astype(f32), c_im.astype(f32)
    dg = d_skip.astype(f32).reshape(G_A, S5_GROUP)
    bsz, L, _ = u.shape
    ug = u.astype(f32).reshape(bsz, L, G_A, S5_GROUP)
    if L > CHUNK:
        nc = L // CHUNK
        uc = ug.reshape(bsz, nc, CHUNK, G_A, S5_GROUP).transpose(1, 0, 2, 3, 4)

        def body(carry, u_blk):
            hr, hi = carry
            y_blk, hr, hi = s5_block(u_blk, hr, hi, ab_re, ab_im, bb_re, bb_im, cr, ci, dg)
            return (hr, hi), y_blk

        (h_re, h_im), ys = lax.scan(body, (h_re, h_im), uc)
        y = ys.transpose(1, 0, 2, 3, 4).reshape(bsz, L, D_A)
    else:
        y, h_re, h_im = s5_block(ug, h_re, h_im, ab_re, ab_im, bb_re, bb_im, cr, ci, dg)
        y = y.reshape(bsz, L, D_A)
    return y, h_re, h_im


def layer(x, s5_re, s5_im, sc_buf, ffn_buf, p):
    (norm1_g, w_in, lam_re, lam_im, log_dt, b_re, b_im, c_re, c_im, d_skip,
     w_glu, sc_conv_w, sc_conv_b, w_sc_out, w_o,
     norm2_g, w_up, ffn_conv_w, ffn_conv_b, w_down) = p
    f32 = jnp.float32
    dt = x.dtype
    h = rmsnorm(x, norm1_g)
    proj = h @ w_in
    u_a = proj[..., :D_A]
    g_in = proj[..., D_A:D_A + D_B]
    c_in = proj[..., D_A + D_B:D_A + 2 * D_B]
    v_in = proj[..., D_A + 2 * D_B:D_A + 3 * D_B]
    gates = jax.nn.sigmoid(proj[..., D_A + 3 * D_B:].astype(f32))
    g_a, g_b = gates[..., :D_MODEL], gates[..., D_MODEL:]
    disc = s5_discretize(lam_re, lam_im, log_dt, b_re, b_im)
    y_a, n_re, n_im = s5_mixer(u_a, s5_re.astype(f32), s5_im.astype(f32), disc, c_re, c_im, d_skip)
    y_a = jax.nn.gelu(y_a).astype(dt)
    glu = y_a @ w_glu
    br_a = glu[..., :D_MODEL].astype(f32) * jax.nn.sigmoid(glu[..., D_MODEL:].astype(f32))
    conv_out, new_sc = causal_dwconv(c_in * v_in, sc_buf, sc_conv_w, sc_conv_b)
    br_b = ((g_in * conv_out) @ w_sc_out).astype(f32)
    merged = (g_a * br_a + g_b * br_b).astype(dt)
    x = x + merged @ w_o
    h2 = rmsnorm(x, norm2_g)
    up = h2 @ w_up
    up_c, new_ffn = causal_dwconv(up, ffn_buf, ffn_conv_w, ffn_conv_b)
    val, gt = up_c[..., :D_FF], up_c[..., D_FF:]
    x = x + (jax.nn.silu(gt) * val) @ w_down
    return x, n_re, n_im, new_sc, new_ffn


def trunk(x, s5_re, s5_im, sc_buf, ffn_buf, layer_params, final_norm_g):
    new_re, new_im, new_sc, new_ffn = [], [], [], []
    for l in range(DEPTH):
        p = tuple(w[l] for w in layer_params)
        x, r, i, sc, ff = layer(x, s5_re[l], s5_im[l], sc_buf[l], ffn_buf[l], p)
        new_re.append(r.astype(s5_re.dtype))
        new_im.append(i.astype(s5_im.dtype))
        new_sc.append(sc.astype(sc_buf.dtype))
        new_ffn.append(ff.astype(ffn_buf.dtype))
    y = rmsnorm(x, final_norm_g)
    return y, jnp.stack(new_re), jnp.stack(new_im), jnp.stack(new_sc), jnp.stack(new_ffn)


def setup_inputs(seed: int = 0) -> dict:
    key = jax.random.key(seed)
    ks = jax.random.split(key, 32)
    f32 = jnp.float32

    def nrm(k, shape, s):
        return jax.random.normal(k, shape, f32) * s

    lam_im_base = jnp.pi * jnp.arange(N_STATE, dtype=f32)
    inp = {
        "x_prompt": nrm(ks[0], (BATCH, SEQ, D_MODEL), 1.0),
        "x_sample": nrm(ks[1], (DEC_BATCH, DEC_SEQ, D_MODEL), 1.0),
        "state_s5_re": nrm(ks[2], (DEPTH, DEC_BATCH, G_A, N_STATE), 0.5),
        "state_s5_im": nrm(ks[3], (DEPTH, DEC_BATCH, G_A, N_STATE), 0.5),
        "cache_sc_conv": nrm(ks[4], (DEPTH, DEC_BATCH, SC_WIDTH - 1, D_B), 0.5),
        "cache_ffn_conv": nrm(ks[5], (DEPTH, DEC_BATCH, FFN_WIDTH - 1, 2 * D_FF), 1.0),
        "norm1_g": 1.0 + nrm(ks[6], (DEPTH, D_MODEL), 0.02),
        "w_in": nrm(ks[7], (DEPTH, D_MODEL, IN_COLS), D_MODEL ** -0.5),
        "lam_re": -0.5 + nrm(ks[8], (DEPTH, G_A, N_STATE), 0.01),
        "lam_im": lam_im_base + nrm(ks[9], (DEPTH, G_A, N_STATE), 0.01),
        "log_dt": jax.random.uniform(ks[10], (DEPTH, G_A), f32, math.log(1e-3), math.log(1e-1)),
        "b_re": nrm(ks[11], (DEPTH, G_A, N_STATE, S5_GROUP), (2 * S5_GROUP) ** -0.5),
        "b_im": nrm(ks[12], (DEPTH, G_A, N_STATE, S5_GROUP), (2 * S5_GROUP) ** -0.5),
        "c_re": nrm(ks[13], (DEPTH, G_A, S5_GROUP, N_STATE), N_STATE ** -0.5),
        "c_im": nrm(ks[14], (DEPTH, G_A, S5_GROUP, N_STATE), N_STATE ** -0.5),
        "d_skip": nrm(ks[15], (DEPTH, D_A), 0.5),
        "w_glu": nrm(ks[16], (DEPTH, D_A, 2 * D_MODEL), D_A ** -0.5),
        "sc_conv_w": nrm(ks[17], (DEPTH, SC_WIDTH, D_B), SC_WIDTH ** -0.5),
        "sc_conv_b": nrm(ks[18], (DEPTH, D_B), 0.02),
        "w_sc_out": nrm(ks[19], (DEPTH, D_B, D_MODEL), D_B ** -0.5),
        "w_o": nrm(ks[20], (DEPTH, D_MODEL, D_MODEL), D_MODEL ** -0.5),
        "norm2_g": 1.0 + nrm(ks[21], (DEPTH, D_MODEL), 0.02),
        "w_up": nrm(ks[22], (DEPTH, D_MODEL, 2 * D_FF), D_MODEL ** -0.5),
        "ffn_conv_w": nrm(ks[23], (DEPTH, FFN_WIDTH, 2 * D_FF), FFN_WIDTH ** -0.5),
        "ffn_conv_b": nrm(ks[24], (DEPTH, 2 * D_FF), 0.02),
        "w_down": nrm(ks[25], (DEPTH, D_FF, D_MODEL), D_FF ** -0.5),
        "final_norm_g": 1.0 + nrm(ks[26], (D_MODEL,), 0.02),
    }
    return inp


def reference(x_prompt, x_sample, state_s5_re, state_s5_im, cache_sc_conv, cache_ffn_conv,
              norm1_g, w_in, lam_re, lam_im, log_dt, b_re, b_im, c_re, c_im, d_skip,
              w_glu, sc_conv_w, sc_conv_b, w_sc_out, w_o,
              norm2_g, w_up, ffn_conv_w, ffn_conv_b, w_down, final_norm_g):
    layer_params = (norm1_g, w_in, lam_re, lam_im, log_dt, b_re, b_im, c_re, c_im, d_skip,
                    w_glu, sc_conv_w, sc_conv_b, w_sc_out, w_o,
                    norm2_g, w_up, ffn_conv_w, ffn_conv_b, w_down)
    bp = x_prompt.shape[0]
    z_re = jnp.zeros((DEPTH, bp, G_A, N_STATE), state_s5_re.dtype)
    z_im = jnp.zeros((DEPTH, bp, G_A, N_STATE), state_s5_im.dtype)
    z_sc = jnp.zeros((DEPTH, bp, SC_WIDTH - 1, D_B), cache_sc_conv.dtype)
    z_ffn = jnp.zeros((DEPTH, bp, FFN_WIDTH - 1, 2 * D_FF), cache_ffn_conv.dtype)
    y_prompt, p_s5_re, p_s5_im, p_sc_conv, p_ffn_conv = trunk(
        x_prompt, z_re, z_im, z_sc, z_ffn, layer_params, final_norm_g)
    y_sample, s_s5_re, s_s5_im, s_sc_conv, s_ffn_conv = trunk(
        x_sample, state_s5_re, state_s5_im, cache_sc_conv, cache_ffn_conv, layer_params, final_norm_g)
    return (y_prompt, y_sample, p_s5_re, p_s5_im, p_sc_conv, p_ffn_conv,
            s_s5_re, s_s5_im, s_sc_conv, s_ffn_conv)
```

```python
import functools
import math

import jax
import jax.numpy as jnp
from jax import lax
from jax.experimental import pallas as pl
from jax.experimental.pallas import tpu as pltpu

D_MODEL = 2048
D_A = D_MODEL // 2
S5_GROUP = 16
G_A = D_A // S5_GROUP
N_STATE = 64
D_B = D_MODEL // 2
D_FF = 5504
EPS = 1e-6
IN_COLS = D_A + 3 * D_B + 2 * D_MODEL

LANES = 128
SUBLANES = 8
VMEM_LIMIT_BYTES = 58 * 1024 * 1024

S5_T = 16
S5_TP = S5_T * S5_GROUP
S5_ST = 2 * N_STATE
D_FF_PAD = 5632
FFN_BLOCK = 512
IN_BLOCK = 1024
CONV_PAD = SUBLANES

bf16 = jnp.bfloat16
f32 = jnp.float32


def _rmsnorm(x, g):
    ms = jnp.mean(x * x, axis=-1, keepdims=True)
    return x * lax.rsqrt(ms + EPS) * g


def _params(sem):
    return pltpu.CompilerParams(dimension_semantics=sem, vmem_limit_bytes=VMEM_LIMIT_BYTES)


def _const_spec(shape):
    nd = len(shape)
    return pl.BlockSpec(shape, lambda *_: (0,) * nd, pipeline_mode=pl.Buffered(1))


def _in_proj_kernel(x_ref, g_ref, w_ref, u_ref, gin_ref, cv_ref, gates_ref, h_scr):
    j = pl.program_id(1)

    @pl.when(j == 0)
    def _():
        h_scr[...] = _rmsnorm(x_ref[...], g_ref[...]).astype(bf16)

    acc = jnp.dot(h_scr[...], w_ref[...], preferred_element_type=f32)

    @pl.when(j == 0)
    def _():
        u_ref[...] = acc

    @pl.when(j == 1)
    def _():
        gin_ref[...] = acc.astype(bf16)

    @pl.when((j == 2) | (j == 3))
    def _():
        half = IN_BLOCK // 2
        cv_ref[...] = (acc[:, :half] * acc[:, half:]).astype(bf16)

    @pl.when(j >= 4)
    def _():
        gates_ref[...] = jax.nn.sigmoid(acc).astype(bf16)


def _in_proj(x, norm_g, w_in_p, tm):
    t = x.shape[0]
    nj = IN_COLS // IN_BLOCK
    half = IN_BLOCK // 2
    return pl.pallas_call(
        _in_proj_kernel,
        grid=(t // tm, nj),
        in_specs=[
            pl.BlockSpec((tm, D_MODEL), lambda i, j: (i, 0)),
            pl.BlockSpec((1, D_MODEL), lambda i, j: (0, 0)),
            pl.BlockSpec((D_MODEL, IN_BLOCK), lambda i, j: (0, j)),
        ],
        out_specs=[
            pl.BlockSpec((tm, D_A), lambda i, j: (i, 0)),
            pl.BlockSpec((tm, D_B), lambda i, j: (i, 0)),
            pl.BlockSpec((tm, half), lambda i, j: (i, jnp.clip(j - 2, 0, 1))),
            pl.BlockSpec((tm, IN_BLOCK), lambda i, j: (i, jnp.clip(j - 4, 0, 3))),
        ],
        out_shape=[
            jax.ShapeDtypeStruct((t, D_A), f32),
            jax.ShapeDtypeStruct((t, D_B), bf16),
            jax.ShapeDtypeStruct((t, D_B), bf16),
            jax.ShapeDtypeStruct((t, 2 * D_MODEL), bf16),
        ],
        scratch_shapes=[pltpu.VMEM((tm, D_MODEL), bf16)],
        compiler_params=_params(("arbitrary", "arbitrary")),
        name="in_proj",
    )(x, norm_g, w_in_p)


def _prep_w_in(w_in):
    half = IN_BLOCK // 2
    u = w_in[:, :D_A]
    g = w_in[:, D_A:D_A + D_B]
    c = w_in[:, D_A + D_B:D_A + 2 * D_B]
    v = w_in[:, D_A + 2 * D_B:D_A + 3 * D_B]
    gates = w_in[:, D_A + 3 * D_B:]
    cv = jnp.concatenate([c[:, :half], v[:, :half], c[:, half:], v[:, half:]], axis=1)
    return jnp.concatenate([u, g, cv, gates], axis=1).astype(bf16)


def _s5_kernel(u_ref, m_ref, wh_ref, wl_ref, v_ref, pq_ref, d_ref, h0_ref, h0s_ref,
               y_ref, sfin_ref, s_scr, t_scr, loc_scr, locs_scr, prev_scr, *, gb, cb, nb):
    ci = pl.program_id(1)

    @pl.when(ci == 0)
    def _():
        s_scr[...] = h0_ref[...]
        t_scr[...] = h0s_ref[...]

    for g in range(gb):
        u = u_ref[:, g * S5_TP:(g + 1) * S5_TP]
        uh = u.astype(bf16)
        ul = (u - uh.astype(f32)).astype(bf16)
        wh = wh_ref[g]
        loc = (jnp.dot(uh, wh, preferred_element_type=f32)
               + jnp.dot(ul, wh, preferred_element_type=f32)
               + jnp.dot(uh, wl_ref[g], preferred_element_type=f32))
        loc_scr[:, g * S5_ST:(g + 1) * S5_ST] = loc[:, :S5_ST]
        locs_scr[:, g * S5_ST:(g + 1) * S5_ST] = loc[:, S5_ST:]

    p = pq_ref[0:1, :]
    q = pq_ref[1:2, :]
    s = s_scr[...]
    t = t_scr[...]
    for c in range(cb):
        rows = slice(c * nb, (c + 1) * nb)
        prev_scr[rows, :] = s
        s, t = (p * s + q * t + loc_scr[rows, :],
                p * t - q * s + locs_scr[rows, :])
    s_scr[...] = s
    t_scr[...] = t
    sfin_ref[...] = s

    for g in range(gb):
        u = u_ref[:, g * S5_TP:(g + 1) * S5_TP]
        sp = prev_scr[:, g * S5_ST:(g + 1) * S5_ST].astype(bf16)
        y = (jnp.dot(u.astype(bf16), m_ref[g], preferred_element_type=f32)
             + jnp.dot(sp, v_ref[g], preferred_element_type=f32)
             + d_ref[:, g * S5_TP:(g + 1) * S5_TP] * u)
        y_ref[:, g * S5_TP:(g + 1) * S5_TP] = jax.nn.gelu(y).astype(bf16)


def _s5(u_rows, tables, h0, h0s, nb, gb, cb):
    m, wh, wl, v, pq, dvec = tables
    rows = u_rows.shape[0]
    nc = rows // nb
    blk_rows = cb * nb
    kern = functools.partial(_s5_kernel, gb=gb, cb=cb, nb=nb)
    return pl.pallas_call(
        kern,
        grid=(G_A // gb, nc // cb),
        in_specs=[
            pl.BlockSpec((blk_rows, gb * S5_TP), lambda gi, ci: (ci, gi)),
            pl.BlockSpec((gb, S5_TP, S5_TP), lambda gi, ci: (gi, 0, 0)),
            pl.BlockSpec((gb, S5_TP, 2 * S5_ST), lambda gi, ci: (gi, 0, 0)),
            pl.BlockSpec((gb, S5_TP, 2 * S5_ST), lambda gi, ci: (gi, 0, 0)),
            pl.BlockSpec((gb, S5_ST, S5_TP), lambda gi, ci: (gi, 0, 0)),
            pl.BlockSpec((2, gb * S5_ST), lambda gi, ci: (0, gi)),
            pl.BlockSpec((1, gb * S5_TP), lambda gi, ci: (0, gi)),
            pl.BlockSpec((nb, gb * S5_ST), lambda gi, ci: (0, gi)),
            pl.BlockSpec((nb, gb * S5_ST), lambda gi, ci: (0, gi)),
        ],
        out_specs=[
            pl.BlockSpec((blk_rows, gb * S5_TP), lambda gi, ci: (ci, gi)),
            pl.BlockSpec((nb, gb * S5_ST), lambda gi, ci: (0, gi)),
        ],
        out_shape=[
            jax.ShapeDtypeStruct((rows, G_A * S5_TP), bf16),
            jax.ShapeDtypeStruct((nb, G_A * S5_ST), f32),
        ],
        scratch_shapes=[
            pltpu.VMEM((nb, gb * S5_ST), f32),
            pltpu.VMEM((nb, gb * S5_ST), f32),
            pltpu.VMEM((blk_rows, gb * S5_ST), f32),
            pltpu.VMEM((blk_rows, gb * S5_ST), f32),
            pltpu.VMEM((blk_rows, gb * S5_ST), f32),
        ],
        compiler_params=_params(("arbitrary", "arbitrary")),
        name="s5",
    )(u_rows, m, wh, wl, v, pq, dvec, h0, h0s)


def _s5_tables(lam_re, lam_im, log_dt, b_re, b_im, c_re, c_im, d_skip):
    hp = lax.Precision.HIGHEST
    dt = jnp.exp(log_dt.astype(f32))[:, None]
    lr, li = lam_re.astype(f32), lam_im.astype(f32)
    mag = jnp.exp(lr * dt)
    ab_re = mag * jnp.cos(li * dt)
    ab_im = mag * jnp.sin(li * dt)
    den = lr * lr + li * li
    nr, ni = ab_re - 1.0, ab_im
    f_re = (nr * lr + ni * li) / den
    f_im = (ni * lr - nr * li) / den
    br, bi = b_re.astype(f32), b_im.astype(f32)
    bb_re = f_re[..., None] * br - f_im[..., None] * bi
    bb_im = f_re[..., None] * bi + f_im[..., None] * br
    cr, ci = c_re.astype(f32), c_im.astype(f32)

    k = jnp.arange(S5_T + 1, dtype=f32)[:, None, None]
    pm = jnp.exp(lr * dt * k)
    pr = pm * jnp.cos(li * dt * k)
    pi = pm * jnp.sin(li * dt * k)

    er = cr[None] * pr[:, :, None, :] - ci[None] * pi[:, :, None, :]
    ei = cr[None] * pi[:, :, None, :] + ci[None] * pr[:, :, None, :]
    kk = (jnp.einsum('tgqn,gnp->tgqp', er[:S5_T], bb_re, precision=hp)
          - jnp.einsum('tgqn,gnp->tgqp', ei[:S5_T], bb_im, precision=hp))
    ti = jnp.arange(S5_T)
    tau = ti[None, :] - ti[:, None]
    kt = kk[jnp.clip(tau, 0, S5_T - 1)]
    kt = jnp.where((tau >= 0)[:, :, None, None, None], kt, 0.0)
    m = kt.transpose(2, 0, 4, 1, 3).reshape(G_A, S5_TP, S5_TP)

    prr = pr[S5_T - 1 - ti]
    pir = pi[S5_T - 1 - ti]
    w_re = prr[:, :, :, None] * bb_re[None] - pir[:, :, :, None] * bb_im[None]
    w_im = prr[:, :, :, None] * bb_im[None] + pir[:, :, :, None] * bb_re[None]
    w_re = w_re.transpose(1, 0, 3, 2).reshape(G_A, S5_TP, N_STATE)
    w_im = w_im.transpose(1, 0, 3, 2).reshape(G_A, S5_TP, N_STATE)
    w = jnp.concatenate([w_re, w_im, w_im, w_re], axis=-1)
    wh = w.astype(bf16)
    wl = (w - wh.astype(f32)).astype(bf16)

    v_re = er[1:].transpose(1, 3, 0, 2).reshape(G_A, N_STATE, S5_TP)
    v_im = (-ei[1:]).transpose(1, 3, 0, 2).reshape(G_A, N_STATE, S5_TP)
    v = jnp.concatenate([v_re, v_im], axis=1).astype(bf16)

    ar, ai = pr[S5_T], pi[S5_T]
    p_row = jnp.concatenate([ar, ar], axis=1).reshape(1, G_A * S5_ST)
    q_row = jnp.concatenate([-ai, ai], axis=1).reshape(1, G_A * S5_ST)
    pq = jnp.concatenate([p_row, q_row], axis=0)

    dg = d_skip.astype(f32).reshape(G_A, 1, S5_GROUP)
    dvec = jnp.broadcast_to(dg, (G_A, S5_T, S5_GROUP)).reshape(1, G_A * S5_TP)
    return m.astype(bf16), wh, wl, v, pq, dvec


def _s5_mixer(u_a, s_re, s_im, tables, gb, cb):
    nb, length, _ = u_a.shape
    nc = length // S5_T
    u_rows = (u_a.reshape(nb, nc, S5_T, G_A, S5_GROUP)
              .transpose(1, 0, 3, 2, 4).reshape(nc * nb, G_A * S5_TP))
    h0 = jnp.concatenate([s_re, s_im], axis=-1).reshape(nb, G_A * S5_ST)
    h0s = jnp.concatenate([s_im, s_re], axis=-1).reshape(nb, G_A * S5_ST)
    y_rows, sfin = _s5(u_rows, tables, h0, h0s, nb, gb, cb)
    y = (y_rows.reshape(nc, nb, G_A, S5_T, S5_GROUP)
         .transpose(1, 0, 3, 2, 4).reshape(nb, length, D_A))
    sfin = sfin.reshape(nb, G_A, S5_ST)
    return y, sfin[..., :N_STATE], sfin[..., N_STATE:]


def _conv3(ext_ref, w_ref, b_ref, lt):
    lo = CONV_PAD - 2
    out = ext_ref[:, lo:lo + lt, :] * w_ref[0:1, :]
    out = out + ext_ref[:, lo + 1:lo + 1 + lt, :] * w_ref[1:2, :]
    out = out + ext_ref[:, lo + 2:lo + 2 + lt, :] * w_ref[2:3, :]
    return out + b_ref[...]


def _mix_kernel(ya_ref, gin_ref, cv_ref, gates_ref, x_ref, cache_ref,
                wglu_ref, wsc_ref, wo_ref, cw_ref, cb_ref,
                x1_ref, nsc_ref, ext_scr, mrg_scr, *, sb, lt, nchunk):
    i = pl.program_id(1)
    rows = sb * lt
    lo = CONV_PAD - 2

    @pl.when(i == 0)
    def _():
        ext_scr[:, lo:CONV_PAD, :] = cache_ref[...]

    @pl.when(i > 0)
    def _():
        ext_scr[:, lo:CONV_PAD, :] = ext_scr[:, lo + lt:CONV_PAD + lt, :]

    ext_scr[:, CONV_PAD:, :] = cv_ref[...].astype(f32)
    nsc_ref[...] = ext_scr[:, lo + lt:CONV_PAD + lt, :]

    conv = _conv3(ext_scr, cw_ref, cb_ref, lt)
    gated = (gin_ref[...].astype(f32) * conv).astype(bf16).reshape(rows, D_B)
    ya = ya_ref[...].reshape(rows, D_A)
    wc = D_MODEL // nchunk
    for n in range(nchunk):
        cols = slice(n * wc, (n + 1) * wc)
        gcols = slice(D_MODEL + n * wc, D_MODEL + (n + 1) * wc)
        a = jnp.dot(ya, wglu_ref[:, cols], preferred_element_type=f32)
        gt = jnp.dot(ya, wglu_ref[:, gcols], preferred_element_type=f32)
        br_a = a * jax.nn.sigmoid(gt)
        br_b = jnp.dot(gated, wsc_ref[:, cols], preferred_element_type=f32)
        g_a = gates_ref[:, :, cols].reshape(rows, wc).astype(f32)
        g_b = gates_ref[:, :, gcols].reshape(rows, wc).astype(f32)
        mrg_scr[:, cols] = (g_a * br_a + g_b * br_b).astype(bf16)
    o = jnp.dot(mrg_scr[...], wo_ref[...], preferred_element_type=f32)
    x1_ref[...] = x_ref[...] + o.reshape(sb, lt, D_MODEL)


def _mix_out(ya, gin, cv, gates, x, cache, wglu, wsc, wo, cw, cbias, sb, lt):
    ns, length, _ = x.shape
    kern = functools.partial(_mix_kernel, sb=sb, lt=lt, nchunk=4)
    tile = lambda c: pl.BlockSpec((sb, lt, c), lambda s, i: (s, i, 0))
    return pl.pallas_call(
        kern,
        grid=(ns // sb, length // lt),
        in_specs=[
            tile(D_A), tile(D_B), tile(D_B), tile(2 * D_MODEL), tile(D_MODEL),
            pl.BlockSpec((sb, 2, D_B), lambda s, i: (s, 0, 0)),
            _const_spec((D_A, 2 * D_MODEL)),
            _const_spec((D_B, D_MODEL)),
            _const_spec((D_MODEL, D_MODEL)),
            _const_spec((3, D_B)),
            _const_spec((1, D_B)),
        ],
        out_specs=[
            tile(D_MODEL),
            pl.BlockSpec((sb, 2, D_B), lambda s, i: (s, 0, 0)),
        ],
        out_shape=[
            jax.ShapeDtypeStruct((ns, length, D_MODEL), f32),
            jax.ShapeDtypeStruct((ns, 2, D_B), f32),
        ],
        scratch_shapes=[
            pltpu.VMEM((sb, lt + CONV_PAD, D_B), f32),
            pltpu.VMEM((sb * lt, D_MODEL), bf16),
        ],
        compiler_params=_params(("arbitrary", "arbitrary")),
        name="mix_out",
    )(ya, gin, cv, gates, x, cache, wglu, wsc, wo, cw, cbias)


def _ffn_kernel(x1_ref, g2_ref, gf_ref, wv_ref, wg_ref, wd_ref,
                cwv_ref, cwg_ref, cbv_ref, cbg_ref, cachev_ref, cacheg_ref,
                y_ref, nfv_ref, nfg_ref,
                h2_scr, acc_scr, extv_scr, extg_scr, carv_scr, carg_scr, *, sb, lt, nf):
    i = pl.program_id(1)
    f = pl.program_id(2)
    rows = sb * lt
    lo = CONV_PAD - 2

    @pl.when(f == 0)
    def _():
        h2_scr[...] = _rmsnorm(x1_ref[...].reshape(rows, D_MODEL), g2_ref[...]).astype(bf16)

    h2 = h2_scr[...]

    def branch(w_ref, cache_ref, car_scr, ext_scr, nf_ref, cw_ref, cb_ref):
        up = jnp.dot(h2, w_ref[...], preferred_element_type=f32).reshape(sb, lt, FFN_BLOCK)

        @pl.when(i == 0)
        def _():
            ext_scr[:, lo:CONV_PAD, :] = cache_ref[...]

        @pl.when(i > 0)
        def _():
            ext_scr[:, lo:CONV_PAD, :] = car_scr[f]

        ext_scr[:, CONV_PAD:, :] = up
        last = ext_scr[:, lo + lt:CONV_PAD + lt, :]
        car_scr[f] = last
        nf_ref[...] = last
        return _conv3(ext_scr, cw_ref, cb_ref, lt)

    val = branch(wv_ref, cachev_ref, carv_scr, extv_scr, nfv_ref, cwv_ref, cbv_ref)
    gt = branch(wg_ref, cacheg_ref, carg_scr, extg_scr, nfg_ref, cwg_ref, cbg_ref)
    act = (jax.nn.silu(gt) * val).astype(bf16).reshape(rows, FFN_BLOCK)
    part = jnp.dot(act, wd_ref[...], preferred_element_type=f32)

    @pl.when(f == 0)
    def _():
        acc_scr[...] = part

    @pl.when(f > 0)
    def _():
        acc_scr[...] += part

    @pl.when(f == nf - 1)
    def _():
        x2 = x1_ref[...].reshape(rows, D_MODEL) + acc_scr[...]
        y_ref[...] = _rmsnorm(x2, gf_ref[...]).reshape(sb, lt, D_MODEL)


def _ffn(x1, g2, gfin, w_up_p, w_down_p, cw_p, cb_p, cache_p, sb, lt):
    ns, length, _ = x1.shape
    nf = D_FF_PAD // FFN_BLOCK
    kern = functools.partial(_ffn_kernel, sb=sb, lt=lt, nf=nf)
    xt = pl.BlockSpec((sb, lt, D_MODEL), lambda s, i, f: (s, i, 0))
    row = pl.BlockSpec((1, D_MODEL), lambda s, i, f: (0, 0))
    return pl.pallas_call(
        kern,
        grid=(ns // sb, length // lt, nf),
        in_specs=[
            xt, row, row,
            pl.BlockSpec((D_MODEL, FFN_BLOCK), lambda s, i, f: (0, f)),
            pl.BlockSpec((D_MODEL, FFN_BLOCK), lambda s, i, f: (0, nf + f)),
            pl.BlockSpec((FFN_BLOCK, D_MODEL), lambda s, i, f: (f, 0)),
            pl.BlockSpec((3, FFN_BLOCK), lambda s, i, f: (0, f)),
            pl.BlockSpec((3, FFN_BLOCK), lambda s, i, f: (0, nf + f)),
            pl.BlockSpec((1, FFN_BLOCK), lambda s, i, f: (0, f)),
            pl.BlockSpec((1, FFN_BLOCK), lambda s, i, f: (0, nf + f)),
            pl.BlockSpec((sb, 2, FFN_BLOCK), lambda s, i, f: (s, 0, f)),
            pl.BlockSpec((sb, 2, FFN_BLOCK), lambda s, i, f: (s, 0, nf + f)),
        ],
        out_specs=[
            xt,
            pl.BlockSpec((sb, None, 2, FFN_BLOCK), lambda s, i, f: (s, i, 0, f)),
            pl.BlockSpec((sb, None, 2, FFN_BLOCK), lambda s, i, f: (s, i, 0, f)),
        ],
        out_shape=[
            jax.ShapeDtypeStruct((ns, length, D_MODEL), f32),
            jax.ShapeDtypeStruct((ns, length // lt, 2, D_FF_PAD), f32),
            jax.ShapeDtypeStruct((ns, length // lt, 2, D_FF_PAD), f32),
        ],
        scratch_shapes=[
            pltpu.VMEM((sb * lt, D_MODEL), bf16),
            pltpu.VMEM((sb * lt, D_MODEL), f32),
            pltpu.VMEM((sb, lt + CONV_PAD, FFN_BLOCK), f32),
            pltpu.VMEM((sb, lt + CONV_PAD, FFN_BLOCK), f32),
            pltpu.VMEM((nf, sb, 2, FFN_BLOCK), f32),
            pltpu.VMEM((nf, sb, 2, FFN_BLOCK), f32),
        ],
        compiler_params=_params(("arbitrary", "arbitrary", "arbitrary")),
        name="ffn",
    )(x1, g2, gfin, w_up_p, w_up_p, w_down_p, cw_p, cw_p, cb_p, cb_p, cache_p, cache_p)


def _pad_ff(a, axis):
    pad = [(0, 0)] * a.ndim
    pad[axis] = (0, D_FF_PAD - D_FF)
    if a.shape[axis] == D_FF:
        return jnp.pad(a, pad)
    lo, hi = jnp.split(a, 2, axis=axis)
    return jnp.concatenate([jnp.pad(lo, pad), jnp.pad(hi, pad)], axis=axis)


def _trunk(x, s_re, s_im, sc_buf, ffn_buf, w, cfg):
    ns, length, _ = x.shape
    tm_in, sb_mix, lt_mix, sb_ffn, lt_ffn, gb, cb = cfg
    xf = x.reshape(ns * length, D_MODEL)
    u_a, gin, cv, gates = _in_proj(xf, w["norm1_g"], w["w_in"], tm_in)
    ya, n_re, n_im = _s5_mixer(u_a.reshape(ns, length, D_A), s_re, s_im, w["s5"], gb, cb)
    r3 = lambda a: a.reshape(ns, length, a.shape[-1])
    x1, new_sc = _mix_out(ya, r3(gin), r3(cv), r3(gates), x, sc_buf,
                          w["w_glu"], w["w_sc_out"], w["w_o"], w["sc_conv_w"], w["sc_conv_b"],
                          sb_mix, lt_mix)
    y, nfv, nfg = _ffn(x1, w["norm2_g"], w["final_norm_g"], w["w_up"], w["w_down"],
                       w["ffn_conv_w"], w["ffn_conv_b"], _pad_ff(ffn_buf, 2), sb_ffn, lt_ffn)
    new_ffn = jnp.concatenate([nfv[:, -1, :, :D_FF], nfg[:, -1, :, :D_FF]], axis=-1)
    return y, n_re[None], n_im[None], new_sc[None], new_ffn[None]


def kernel(x_prompt, x_sample, state_s5_re, state_s5_im, cache_sc_conv, cache_ffn_conv, norm1_g, w_in, lam_re, lam_im, log_dt, b_re, b_im, c_re, c_im, d_skip, w_glu, sc_conv_w, sc_conv_b, w_sc_out, w_o, norm2_g, w_up, ffn_conv_w, ffn_conv_b, w_down, final_norm_g):
    w = {
        "norm1_g": norm1_g[0].reshape(1, D_MODEL),
        "w_in": _prep_w_in(w_in[0]),
        "s5": _s5_tables(lam_re[0], lam_im[0], log_dt[0], b_re[0], b_im[0],
                         c_re[0], c_im[0], d_skip[0]),
        "w_glu": w_glu[0].astype(bf16),
        "sc_conv_w": sc_conv_w[0],
        "sc_conv_b": sc_conv_b[0].reshape(1, D_B),
        "w_sc_out": w_sc_out[0].astype(bf16),
        "w_o": w_o[0].astype(bf16),
        "norm2_g": norm2_g[0].reshape(1, D_MODEL),
        "w_up": _pad_ff(w_up[0], 1).astype(bf16),
        "ffn_conv_w": _pad_ff(ffn_conv_w[0], 1),
        "ffn_conv_b": _pad_ff(ffn_conv_b[0].reshape(1, 2 * D_FF), 1),
        "w_down": _pad_ff(w_down[0], 0).astype(bf16),
        "final_norm_g": final_norm_g.reshape(1, D_MODEL),
    }
    bp, lp, _ = x_prompt.shape
    bs, ls, _ = x_sample.shape
    zeros = lambda *s: jnp.zeros(s, f32)
    p_cfg = (512, 1, 256, 1, 512, 16, 64)
    yp, p_re, p_im, p_sc, p_ffn = _trunk(
        x_prompt, zeros(bp, G_A, N_STATE), zeros(bp, G_A, N_STATE),
        zeros(bp, 2, D_B), zeros(bp, 2, 2 * D_FF), w, p_cfg)
    s_cfg = (bs * ls, bs // 2, ls, bs, ls, 16, 1)
    ys, s_re, s_im, s_sc, s_ffn = _trunk(
        x_sample, state_s5_re[0], state_s5_im[0], cache_sc_conv[0], cache_ffn_conv[0], w, s_cfg)
    return (yp, ys, p_re, p_im, p_sc, p_ffn, s_re, s_im, s_sc, s_ffn)
```

```python
import functools

import jax
import jax.numpy as jnp
from jax import lax
from jax.experimental import pallas as pl
from jax.experimental.pallas import tpu as pltpu

D_MODEL = 2048
D_A = D_MODEL // 2
S5_GROUP = 16
G_A = D_A // S5_GROUP
N_STATE = 64
D_B = D_MODEL // 2
D_FF = 5504
EPS = 1e-6
IN_COLS = D_A + 3 * D_B + 2 * D_MODEL

LANES = 128
SUBLANES = 8
VMEM_LIMIT_BYTES = 58 * 1024 * 1024

N_SLAB = D_A // LANES
SLAB_G = LANES // S5_GROUP
S5_T = 16
S5_K = S5_T * LANES
S5_ST = 2 * N_STATE
S5_SL = SLAB_G * S5_ST
D_FF_PAD = 5632
FFN_BLOCK = 512
CONV_PAD = SUBLANES

bf16 = jnp.bfloat16
f32 = jnp.float32


def _rmsnorm(x, g):
    ms = jnp.mean(x * x, axis=-1, keepdims=True)
    return x * lax.rsqrt(ms + EPS) * g


def _params(sem):
    return pltpu.CompilerParams(dimension_semantics=sem, vmem_limit_bytes=VMEM_LIMIT_BYTES)


def _const_spec(shape):
    nd = len(shape)
    return pl.BlockSpec(shape, lambda *_: (0,) * nd, pipeline_mode=pl.Buffered(1))


def _dot(a, b):
    return jnp.dot(a, b, preferred_element_type=f32)


def _in_proj_kernel(x_ref, g_ref, w_ref, u_ref, gin_ref, cv_ref, gates_ref):
    h = _rmsnorm(x_ref[...], g_ref[...]).astype(bf16)
    u = _dot(h, w_ref[:, :D_A])
    for k in range(N_SLAB):
        u_ref[k] = u[:, k * LANES:(k + 1) * LANES]
    gin_ref[...] = _dot(h, w_ref[:, D_A:D_A + D_B]).astype(bf16)
    c = _dot(h, w_ref[:, D_A + D_B:D_A + 2 * D_B])
    v = _dot(h, w_ref[:, D_A + 2 * D_B:D_A + 3 * D_B])
    cv_ref[...] = (c * v).astype(bf16)
    g0 = D_A + 3 * D_B
    for n in range(2 * D_MODEL // D_B):
        acc = _dot(h, w_ref[:, g0 + n * D_B:g0 + (n + 1) * D_B])
        gates_ref[:, n * D_B:(n + 1) * D_B] = jax.nn.sigmoid(acc).astype(bf16)


def _in_proj(x, norm_g, w_in_b, tm):
    t = x.shape[0]
    return pl.pallas_call(
        _in_proj_kernel,
        grid=(t // tm,),
        in_specs=[
            pl.BlockSpec((tm, D_MODEL), lambda i: (i, 0)),
            _const_spec((1, D_MODEL)),
            _const_spec((D_MODEL, IN_COLS)),
        ],
        out_specs=[
            pl.BlockSpec((N_SLAB, tm, LANES), lambda i: (0, i, 0)),
            pl.BlockSpec((tm, D_B), lambda i: (i, 0)),
            pl.BlockSpec((tm, D_B), lambda i: (i, 0)),
            pl.BlockSpec((tm, 2 * D_MODEL), lambda i: (i, 0)),
        ],
        out_shape=[
            jax.ShapeDtypeStruct((N_SLAB, t, LANES), f32),
            jax.ShapeDtypeStruct((t, D_B), bf16),
            jax.ShapeDtypeStruct((t, D_B), bf16),
            jax.ShapeDtypeStruct((t, 2 * D_MODEL), bf16),
        ],
        compiler_params=_params(("arbitrary",)),
        name="in_proj",
    )(x, norm_g, w_in_b)


def _s5_rows_out(y, xs, d_ref, store):
    for t in range(S5_T):
        piece = y[:, t * LANES:(t + 1) * LANES] + d_ref[...] * xs[t]
        store(t, jax.nn.gelu(piece))


def _s5_prompt_kernel(u_ref, m_ref, v_ref, w_ref, pq_ref, d_ref, h0_ref, h0s_ref,
                      y_ref, sfin_ref, s_scr, t_scr, loc_scr, loct_scr, prev_scr, *, nb, cb):
    ci = pl.program_id(1)

    @pl.when(ci == 0)
    def _():
        s_scr[...] = h0_ref[...]
        t_scr[...] = h0s_ref[...]
        loc_scr[...] = jnp.zeros_like(loc_scr)
        loct_scr[...] = jnp.zeros_like(loct_scr)

    xs = [jnp.concatenate([u_ref[b, pl.ds(t, cb, stride=S5_T), :] for b in range(nb)], axis=0)
          for t in range(S5_T)]
    lhs = jnp.concatenate(xs, axis=1).astype(bf16)

    loc = _dot(lhs, w_ref[...])
    for j in range(SLAB_G):
        lj = loc[:, j * S5_ST:(j + 1) * S5_ST]
        ljt = pltpu.roll(lj, N_STATE, axis=1)
        for b in range(nb):
            rows = slice(b * cb, (b + 1) * cb)
            loc_scr[j, pl.ds(b, cb, stride=SUBLANES), :] = lj[rows]
            loct_scr[j, pl.ds(b, cb, stride=SUBLANES), :] = ljt[rows]

    def step(c, carry):
        ss, ts = carry
        r = pl.multiple_of(c * SUBLANES, SUBLANES)
        ns, nt = [], []
        for j in range(SLAB_G):
            p = pq_ref[0, j:j + 1, :]
            q = pq_ref[1, j:j + 1, :]
            prev_scr[j, pl.ds(r, SUBLANES), :] = ss[j]
            ns.append(p * ss[j] + q * ts[j] + loc_scr[j, pl.ds(r, SUBLANES), :])
            nt.append(p * ts[j] - q * ss[j] + loct_scr[j, pl.ds(r, SUBLANES), :])
        return tuple(ns), tuple(nt)

    init = (tuple(s_scr[j] for j in range(SLAB_G)), tuple(t_scr[j] for j in range(SLAB_G)))
    ss, ts = lax.fori_loop(0, cb, step, init)
    for j in range(SLAB_G):
        s_scr[j] = ss[j]
        t_scr[j] = ts[j]
        sfin_ref[j] = ss[j]

    sprev = jnp.concatenate(
        [jnp.concatenate([prev_scr[j, pl.ds(b, cb, stride=SUBLANES), :] for b in range(nb)], axis=0)
         for j in range(SLAB_G)], axis=1).astype(bf16)
    y = _dot(lhs, m_ref[...]) + _dot(sprev, v_ref[...])

    def store(t, val):
        for b in range(nb):
            y_ref[b, pl.ds(t, cb, stride=S5_T), :] = val[b * cb:(b + 1) * cb]

    _s5_rows_out(y, xs, d_ref, store)


def _s5_prompt(u, tables, h0, h0s, cb):
    m, v, wh, _, pq, dvec = tables
    _, nb, length, _ = u.shape
    nc = length // S5_T
    kern = functools.partial(_s5_prompt_kernel, nb=nb, cb=cb)
    tok = pl.BlockSpec((None, nb, cb * S5_T, LANES), lambda k, ci: (k, 0, ci, 0))
    per_slab = lambda *shape, **kw: pl.BlockSpec(
        (None,) + shape, lambda k, ci: (k,) + (0,) * len(shape), **kw)
    table = functools.partial(per_slab, pipeline_mode=pl.Buffered(1))
    state = per_slab(SLAB_G, SUBLANES, S5_ST)
    return pl.pallas_call(
        kern,
        grid=(N_SLAB, nc // cb),
        in_specs=[tok, table(S5_K, S5_K), table(S5_SL, S5_K), table(S5_K, S5_SL),
                  per_slab(2, SLAB_G, S5_ST), per_slab(1, LANES), state, state],
        out_specs=[tok, state],
        out_shape=[
            jax.ShapeDtypeStruct(u.shape, f32),
            jax.ShapeDtypeStruct((N_SLAB, SLAB_G, SUBLANES, S5_ST), f32),
        ],
        scratch_shapes=[
            pltpu.VMEM((SLAB_G, SUBLANES, S5_ST), f32),
            pltpu.VMEM((SLAB_G, SUBLANES, S5_ST), f32),
            pltpu.VMEM((SLAB_G, cb * SUBLANES, S5_ST), f32),
            pltpu.VMEM((SLAB_G, cb * SUBLANES, S5_ST), f32),
            pltpu.VMEM((SLAB_G, cb * SUBLANES, S5_ST), f32),
        ],
        compiler_params=_params(("arbitrary", "arbitrary")),
        name="s5_prompt",
    )(u, m, v, wh, pq, dvec, h0, h0s)


def _s5_sample_kernel(u_ref, m_ref, v_ref, wh_ref, wl_ref, pq_ref, d_ref, h0_ref, h0s_ref,
                      y_ref, sfin_ref, *, nb):
    xs = [u_ref[pl.ds(t, nb, stride=S5_T), :] for t in range(S5_T)]
    lhs_f = jnp.concatenate(xs, axis=1)
    lhs = lhs_f.astype(bf16)
    lhs_lo = (lhs_f - lhs.astype(f32)).astype(bf16)
    wh = wh_ref[...]
    loc = _dot(lhs, wh) + _dot(lhs_lo, wh) + _dot(lhs, wl_ref[...])
    for j in range(SLAB_G):
        p = pq_ref[0, j:j + 1, :]
        q = pq_ref[1, j:j + 1, :]
        sfin_ref[j] = p * h0_ref[j] + q * h0s_ref[j] + loc[:, j * S5_ST:(j + 1) * S5_ST]
    sprev = jnp.concatenate([h0_ref[j] for j in range(SLAB_G)], axis=1).astype(bf16)
    y = _dot(lhs, m_ref[...]) + _dot(sprev, v_ref[...])

    def store(t, val):
        y_ref[pl.ds(t, nb, stride=S5_T), :] = val

    _s5_rows_out(y, xs, d_ref, store)


def _s5_sample(u, tables, h0, h0s, nb):
    m, v, wh, wl, pq, dvec = tables
    kern = functools.partial(_s5_sample_kernel, nb=nb)
    per_slab = lambda *shape: pl.BlockSpec((None,) + shape, lambda k: (k,) + (0,) * len(shape))
    tok = per_slab(nb * S5_T, LANES)
    state = per_slab(SLAB_G, nb, S5_ST)
    return pl.pallas_call(
        kern,
        grid=(N_SLAB,),
        in_specs=[tok, per_slab(S5_K, S5_K), per_slab(S5_SL, S5_K), per_slab(S5_K, S5_SL),
                  per_slab(S5_K, S5_SL), per_slab(2, SLAB_G, S5_ST), per_slab(1, LANES),
                  state, state],
        out_specs=[tok, state],
        out_shape=[
            jax.ShapeDtypeStruct(u.shape, f32),
            jax.ShapeDtypeStruct((N_SLAB, SLAB_G, nb, S5_ST), f32),
        ],
        compiler_params=_params(("arbitrary",)),
        name="s5_sample",
    )(u, m, v, wh, wl, pq, dvec, h0, h0s)


def _s5_tables(lam_re, lam_im, log_dt, b_re, b_im, c_re, c_im, d_skip):
    hp = lax.Precision.HIGHEST
    dt = jnp.exp(log_dt.astype(f32))[:, None]
    lr, li = lam_re.astype(f32), lam_im.astype(f32)
    mag = jnp.exp(lr * dt)
    ab_re = mag * jnp.cos(li * dt)
    ab_im = mag * jnp.sin(li * dt)
    den = lr * lr + li * li
    nr, ni = ab_re - 1.0, ab_im
    f_re = (nr * lr + ni * li) / den
    f_im = (ni * lr - nr * li) / den
    br, bi = b_re.astype(f32), b_im.astype(f32)
    bb_re = f_re[..., None] * br - f_im[..., None] * bi
    bb_im = f_re[..., None] * bi + f_im[..., None] * br
    cr, ci = c_re.astype(f32), c_im.astype(f32)

    k = jnp.arange(S5_T + 1, dtype=f32)[:, None, None]
    pm = jnp.exp(lr * dt * k)
    pr = pm * jnp.cos(li * dt * k)
    pi = pm * jnp.sin(li * dt * k)

    er = cr[None] * pr[:, :, None, :] - ci[None] * pi[:, :, None, :]
    ei = cr[None] * pi[:, :, None, :] + ci[None] * pr[:, :, None, :]
    kk = (jnp.einsum('tgqn,gnp->tgqp', er[:S5_T], bb_re, precision=hp)
          - jnp.einsum('tgqn,gnp->tgqp', ei[:S5_T], bb_im, precision=hp))
    ti = jnp.arange(S5_T)
    tau = ti[None, :] - ti[:, None]
    kt = kk[jnp.clip(tau, 0, S5_T - 1)]
    kt = jnp.where((tau >= 0)[:, :, None, None, None], kt, 0.0)
    eye = jnp.eye(SLAB_G, dtype=f32)
    m6 = kt.reshape(S5_T, S5_T, N_SLAB, SLAB_G, S5_GROUP, S5_GROUP).transpose(2, 0, 3, 5, 1, 4)
    m = (m6[:, :, :, :, :, None, :] * eye[None, None, :, None, None, :, None])
    m = m.reshape(N_SLAB, S5_K, S5_K).astype(bf16)

    prr = pr[S5_T - 1 - ti]
    pir = pi[S5_T - 1 - ti]
    w_re = prr[:, :, :, None] * bb_re[None] - pir[:, :, :, None] * bb_im[None]
    w_im = prr[:, :, :, None] * bb_im[None] + pir[:, :, :, None] * bb_re[None]
    w2 = jnp.concatenate([w_re, w_im], axis=2)
    w5 = w2.reshape(S5_T, N_SLAB, SLAB_G, S5_ST, S5_GROUP).transpose(1, 0, 2, 4, 3)
    w = (w5[:, :, :, :, None, :] * eye[None, None, :, None, :, None]).reshape(N_SLAB, S5_K, S5_SL)
    wh = w.astype(bf16)
    wl = (w - wh.astype(f32)).astype(bf16)

    v2 = jnp.concatenate([er[1:], -ei[1:]], axis=3)
    v5 = v2.reshape(S5_T, N_SLAB, SLAB_G, S5_GROUP, S5_ST).transpose(1, 2, 4, 0, 3)
    v = (v5[:, :, :, :, None, :] * eye[None, :, None, None, :, None]).reshape(N_SLAB, S5_SL, S5_K)
    v = v.astype(bf16)

    ar, ai = pr[S5_T], pi[S5_T]
    p_row = jnp.concatenate([ar, ar], axis=1).reshape(N_SLAB, 1, SLAB_G, S5_ST)
    q_row = jnp.concatenate([-ai, ai], axis=1).reshape(N_SLAB, 1, SLAB_G, S5_ST)
    pq = jnp.concatenate([p_row, q_row], axis=1)
    dvec = d_skip.astype(f32).reshape(N_SLAB, 1, LANES)
    return m, v, wh, wl, pq, dvec


def _s5_mixer(u, s_re, s_im, tables, cb):
    nb = s_re.shape[0]
    length = u.shape[1] // nb

    def to_slab(a):
        return a.reshape(nb, N_SLAB, SLAB_G, S5_ST).transpose(1, 2, 0, 3)

    h0 = to_slab(jnp.concatenate([s_re, s_im], axis=-1))
    h0s = to_slab(jnp.concatenate([s_im, s_re], axis=-1))
    if length == S5_T:
        y, sfin = _s5_sample(u, tables, h0, h0s, nb)
    else:
        pad = [(0, 0), (0, 0), (0, SUBLANES - nb), (0, 0)]
        y, sfin = _s5_prompt(u.reshape(N_SLAB, nb, length, LANES), tables,
                             jnp.pad(h0, pad), jnp.pad(h0s, pad), cb)
        y = y.reshape(u.shape)
        sfin = sfin[:, :, :nb]
    sfin = sfin.transpose(2, 0, 1, 3).reshape(nb, G_A, S5_ST)
    return y, sfin[..., :N_STATE], sfin[..., N_STATE:]


def _conv3(ext_ref, w_ref, b_ref, lt):
    lo = CONV_PAD - 2
    out = ext_ref[:, lo:lo + lt, :] * w_ref[0:1, :]
    out = out + ext_ref[:, lo + 1:lo + 1 + lt, :] * w_ref[1:2, :]
    out = out + ext_ref[:, lo + 2:lo + 2 + lt, :] * w_ref[2:3, :]
    return out + b_ref[...]


def _mix_kernel(ya_ref, gin_ref, cv_ref, gates_ref, x_ref, cache_ref,
                wglu_ref, wsc_ref, wo_ref, cw_ref, cb_ref,
                x1_ref, nsc_ref, ext_scr, mrg_scr, *, sb, lt, nchunk):
    i = pl.program_id(1)
    rows = sb * lt
    lo = CONV_PAD - 2

    @pl.when(i == 0)
    def _():
        ext_scr[:, lo:CONV_PAD, :] = cache_ref[...]

    @pl.when(i > 0)
    def _():
        ext_scr[:, lo:CONV_PAD, :] = ext_scr[:, lo + lt:CONV_PAD + lt, :]

    ext_scr[:, CONV_PAD:, :] = cv_ref[...].astype(f32)
    nsc_ref[...] = ext_scr[:, lo + lt:CONV_PAD + lt, :]

    conv = _conv3(ext_scr, cw_ref, cb_ref, lt)
    gated = (gin_ref[...].astype(f32) * conv).astype(bf16).reshape(rows, D_B)
    ya = jnp.concatenate([ya_ref[k].reshape(rows, LANES) for k in range(N_SLAB)],
                         axis=1).astype(bf16)
    wc = D_MODEL // nchunk
    for n in range(nchunk):
        cols = slice(n * wc, (n + 1) * wc)
        gcols = slice(D_MODEL + n * wc, D_MODEL + (n + 1) * wc)
        a = _dot(ya, wglu_ref[:, cols])
        gt = _dot(ya, wglu_ref[:, gcols])
        br_a = a * jax.nn.sigmoid(gt)
        br_b = _dot(gated, wsc_ref[:, cols])
        g_a = gates_ref[:, :, cols].reshape(rows, wc).astype(f32)
        g_b = gates_ref[:, :, gcols].reshape(rows, wc).astype(f32)
        mrg_scr[:, cols] = (g_a * br_a + g_b * br_b).astype(bf16)
    o = _dot(mrg_scr[...], wo_ref[...])
    x1_ref[...] = x_ref[...] + o.reshape(sb, lt, D_MODEL)


def _mix_out(ya, gin, cv, gates, x, cache, wglu, wsc, wo, cw, cbias, sb, lt):
    ns, length, _ = x.shape
    kern = functools.partial(_mix_kernel, sb=sb, lt=lt, nchunk=4)
    tile = lambda c: pl.BlockSpec((sb, lt, c), lambda s, i: (s, i, 0))
    return pl.pallas_call(
        kern,
        grid=(ns // sb, length // lt),
        in_specs=[
            pl.BlockSpec((N_SLAB, sb, lt, LANES), lambda s, i: (0, s, i, 0)),
            tile(D_B), tile(D_B), tile(2 * D_MODEL), tile(D_MODEL),
            pl.BlockSpec((sb, 2, D_B), lambda s, i: (s, 0, 0)),
            _const_spec((D_A, 2 * D_MODEL)),
            _const_spec((D_B, D_MODEL)),
            _const_spec((D_MODEL, D_MODEL)),
            _const_spec((3, D_B)),
            _const_spec((1, D_B)),
        ],
        out_specs=[
            tile(D_MODEL),
            pl.BlockSpec((sb, 2, D_B), lambda s, i: (s, 0, 0)),
        ],
        out_shape=[
            jax.ShapeDtypeStruct((ns, length, D_MODEL), f32),
            jax.ShapeDtypeStruct((ns, 2, D_B), f32),
        ],
        scratch_shapes=[
            pltpu.VMEM((sb, lt + CONV_PAD, D_B), f32),
            pltpu.VMEM((sb * lt, D_MODEL), bf16),
        ],
        compiler_params=_params(("arbitrary", "arbitrary")),
        name="mix_out",
    )(ya, gin, cv, gates, x, cache, wglu, wsc, wo, cw, cbias)


def _ffn_kernel(x1_ref, g2_ref, gf_ref, wv_ref, wg_ref, wd_ref,
                cwv_ref, cwg_ref, cbv_ref, cbg_ref, cachev_ref, cacheg_ref,
                y_ref, nfv_ref, nfg_ref,
                h2_scr, acc_scr, extv_scr, extg_scr, carv_scr, carg_scr, *, sb, lt, nf):
    i = pl.program_id(1)
    f = pl.program_id(2)
    rows = sb * lt
    lo = CONV_PAD - 2

    @pl.when(f == 0)
    def _():
        h2_scr[...] = _rmsnorm(x1_ref[...].reshape(rows, D_MODEL), g2_ref[...]).astype(bf16)
        acc_scr[...] = jnp.zeros_like(acc_scr)

    @pl.when(i == 0)
    def _():
        carv_scr[f] = cachev_ref[...]
        carg_scr[f] = cacheg_ref[...]

    h2 = h2_scr[...]

    def branch(w_ref, car_scr, ext_scr, nf_ref, cw_ref, cb_ref):
        up = _dot(h2, w_ref[...]).reshape(sb, lt, FFN_BLOCK)
        ext_scr[:, lo:CONV_PAD, :] = car_scr[f]
        ext_scr[:, CONV_PAD:, :] = up
        last = ext_scr[:, lo + lt:CONV_PAD + lt, :]
        car_scr[f] = last
        nf_ref[...] = last
        return _conv3(ext_scr, cw_ref, cb_ref, lt)

    val = branch(wv_ref, carv_scr, extv_scr, nfv_ref, cwv_ref, cbv_ref)
    gt = branch(wg_ref, carg_scr, extg_scr, nfg_ref, cwg_ref, cbg_ref)
    act = (jax.nn.silu(gt) * val).astype(bf16).reshape(rows, FFN_BLOCK)
    acc_scr[...] += _dot(act, wd_ref[...])

    @pl.when(f == nf - 1)
    def _():
        x2 = x1_ref[...].reshape(rows, D_MODEL) + acc_scr[...]
        y_ref[...] = _rmsnorm(x2, gf_ref[...]).reshape(sb, lt, D_MODEL)


def _ffn(x1, g2, gfin, w_up_p, w_down_p, cw_p, cb_p, cache_p, sb, lt):
    ns, length, _ = x1.shape
    nf = D_FF_PAD // FFN_BLOCK
    kern = functools.partial(_ffn_kernel, sb=sb, lt=lt, nf=nf)
    xt = pl.BlockSpec((sb, lt, D_MODEL), lambda s, i, f: (s, i, 0))
    row = pl.BlockSpec((1, D_MODEL), lambda s, i, f: (0, 0))
    return pl.pallas_call(
        kern,
        grid=(ns // sb, length // lt, nf),
        in_specs=[
            xt, row, row,
            pl.BlockSpec((D_MODEL, FFN_BLOCK), lambda s, i, f: (0, f)),
            pl.BlockSpec((D_MODEL, FFN_BLOCK), lambda s, i, f: (0, nf + f)),
            pl.BlockSpec((FFN_BLOCK, D_MODEL), lambda s, i, f: (f, 0)),
            pl.BlockSpec((3, FFN_BLOCK), lambda s, i, f: (0, f)),
            pl.BlockSpec((3, FFN_BLOCK), lambda s, i, f: (0, nf + f)),
            pl.BlockSpec((1, FFN_BLOCK), lambda s, i, f: (0, f)),
            pl.BlockSpec((1, FFN_BLOCK), lambda s, i, f: (0, nf + f)),
            pl.BlockSpec((sb, 2, FFN_BLOCK), lambda s, i, f: (s, 0, f)),
            pl.BlockSpec((sb, 2, FFN_BLOCK), lambda s, i, f: (s, 0, nf + f)),
        ],
        out_specs=[
            xt,
            pl.BlockSpec((sb, None, 2, FFN_BLOCK), lambda s, i, f: (s, i, 0, f)),
            pl.BlockSpec((sb, None, 2, FFN_BLOCK), lambda s, i, f: (s, i, 0, f)),
        ],
        out_shape=[
            jax.ShapeDtypeStruct((ns, length, D_MODEL), f32),
            jax.ShapeDtypeStruct((ns, length // lt, 2, D_FF_PAD), f32),
            jax.ShapeDtypeStruct((ns, length // lt, 2, D_FF_PAD), f32),
        ],
        scratch_shapes=[
            pltpu.VMEM((sb * lt, D_MODEL), bf16),
            pltpu.VMEM((sb * lt, D_MODEL), f32),
            pltpu.VMEM((sb, lt + CONV_PAD, FFN_BLOCK), f32),
            pltpu.VMEM((sb, lt + CONV_PAD, FFN_BLOCK), f32),
            pltpu.VMEM((nf, sb, 2, FFN_BLOCK), f32),
            pltpu.VMEM((nf, sb, 2, FFN_BLOCK), f32),
        ],
        compiler_params=_params(("arbitrary", "arbitrary", "arbitrary")),
        name="ffn",
    )(x1, g2, gfin, w_up_p, w_up_p, w_down_p, cw_p, cw_p, cb_p, cb_p, cache_p, cache_p)


def _pad_ff(a, axis):
    pad = [(0, 0)] * a.ndim
    pad[axis] = (0, D_FF_PAD - D_FF)
    if a.shape[axis] == D_FF:
        return jnp.pad(a, pad)
    lo, hi = jnp.split(a, 2, axis=axis)
    return jnp.concatenate([jnp.pad(lo, pad), jnp.pad(hi, pad)], axis=axis)


def _trunk(x, s_re, s_im, sc_buf, ffn_buf, w, cfg):
    ns, length, _ = x.shape
    tm_in, sb_mix, lt_mix, sb_ffn, lt_ffn, cb = cfg
    xf = x.reshape(ns * length, D_MODEL)
    u_a, gin, cv, gates = _in_proj(xf, w["norm1_g"], w["w_in"], tm_in)
    ya, n_re, n_im = _s5_mixer(u_a, s_re, s_im, w["s5"], cb)
    r3 = lambda a: a.reshape(ns, length, a.shape[-1])
    x1, new_sc = _mix_out(ya.reshape(N_SLAB, ns, length, LANES), r3(gin), r3(cv), r3(gates), x,
                          sc_buf, w["w_glu"], w["w_sc_out"], w["w_o"],
                          w["sc_conv_w"], w["sc_conv_b"], sb_mix, lt_mix)
    y, nfv, nfg = _ffn(x1, w["norm2_g"], w["final_norm_g"], w["w_up"], w["w_down"],
                       w["ffn_conv_w"], w["ffn_conv_b"], _pad_ff(ffn_buf, 2), sb_ffn, lt_ffn)
    new_ffn = jnp.concatenate([nfv[:, -1, :, :D_FF], nfg[:, -1, :, :D_FF]], axis=-1)
    return y, n_re[None], n_im[None], new_sc[None], new_ffn[None]


def kernel(x_prompt, x_sample, state_s5_re, state_s5_im, cache_sc_conv, cache_ffn_conv, norm1_g, w_in, lam_re, lam_im, log_dt, b_re, b_im, c_re, c_im, d_skip, w_glu, sc_conv_w, sc_conv_b, w_sc_out, w_o, norm2_g, w_up, ffn_conv_w, ffn_conv_b, w_down, final_norm_g):
    w = {
        "norm1_g": norm1_g[0].reshape(1, D_MODEL),
        "w_in": w_in[0].astype(bf16),
        "s5": _s5_tables(lam_re[0], lam_im[0], log_dt[0], b_re[0], b_im[0],
                         c_re[0], c_im[0], d_skip[0]),
        "w_glu": w_glu[0].astype(bf16),
        "sc_conv_w": sc_conv_w[0],
        "sc_conv_b": sc_conv_b[0].reshape(1, D_B),
        "w_sc_out": w_sc_out[0].astype(bf16),
        "w_o": w_o[0].astype(bf16),
        "norm2_g": norm2_g[0].reshape(1, D_MODEL),
        "w_up": _pad_ff(w_up[0], 1).astype(bf16),
        "ffn_conv_w": _pad_ff(ffn_conv_w[0], 1),
        "ffn_conv_b": _pad_ff(ffn_conv_b[0].reshape(1, 2 * D_FF), 1),
        "w_down": _pad_ff(w_down[0], 0).astype(bf16),
        "final_norm_g": final_norm_g.reshape(1, D_MODEL),
    }
    bp, lp, _ = x_prompt.shape
    bs, ls, _ = x_sample.shape
    zeros = lambda *s: jnp.zeros(s, f32)
    p_cfg = (256, 1, 256, 1, 512, 64)
    yp, p_re, p_im, p_sc, p_ffn = _trunk(
        x_prompt, zeros(bp, G_A, N_STATE), zeros(bp, G_A, N_STATE),
        zeros(bp, 2, D_B), zeros(bp, 2, 2 * D_FF), w, p_cfg)
    s_cfg = (256, bs // 2, ls, bs, ls, 1)
    ys, s_re, s_im, s_sc, s_ffn = _trunk(
        x_sample, state_s5_re[0], state_s5_im[0], cache_sc_conv[0], cache_ffn_conv[0], w, s_cfg)
    return (yp, ys, p_re, p_im, p_sc, p_ffn, s_re, s_im, s_sc, s_ffn)
```

```python
import functools

import jax
import jax.numpy as jnp
import numpy as np
from jax import lax
from jax.experimental import pallas as pl
from jax.experimental.pallas import tpu as pltpu

D_MODEL = 2048
D_A = D_MODEL // 2
S5_GROUP = 16
G_A = D_A // S5_GROUP
N_STATE = 64
D_B = D_MODEL // 2
D_FF = 5504
EPS = 1e-6
IN_COLS = D_A + 3 * D_B + 2 * D_MODEL

LANES = 128
SUBLANES = 8
MXU_DIM = 256
VMEM_LIMIT_BYTES = 58 * 1024 * 1024

N_SLAB = D_A // LANES
SLAB_G = LANES // S5_GROUP
S5_T = 16
S5_TQ = S5_T * S5_GROUP
S5_K = S5_T * LANES
S5_ST = 2 * N_STATE
S5_SL = SLAB_G * S5_ST
D_FF_PAD = 5632
FFN_BLOCK = 512
CONV_PAD = SUBLANES

bf16 = jnp.bfloat16
f32 = jnp.float32


def _rmsnorm(x, g):
    ms = jnp.mean(x * x, axis=-1, keepdims=True)
    return x * lax.rsqrt(ms + EPS) * g


def _params(sem):
    return pltpu.CompilerParams(dimension_semantics=sem, vmem_limit_bytes=VMEM_LIMIT_BYTES)


def _const_spec(shape):
    nd = len(shape)
    return pl.BlockSpec(shape, lambda *_: (0,) * nd, pipeline_mode=pl.Buffered(1))


def _dot(a, b):
    return jnp.dot(a, b, preferred_element_type=f32)


def _in_proj_kernel(x_ref, g_ref, w_ref, u_ref, gin_ref, cv_ref, gates_ref):
    h = _rmsnorm(x_ref[...], g_ref[...]).astype(bf16)
    u = _dot(h, w_ref[:, :D_A])
    for k in range(N_SLAB):
        u_ref[k] = u[:, k * LANES:(k + 1) * LANES]
    gin_ref[...] = _dot(h, w_ref[:, D_A:D_A + D_B]).astype(bf16)
    c = _dot(h, w_ref[:, D_A + D_B:D_A + 2 * D_B])
    v = _dot(h, w_ref[:, D_A + 2 * D_B:D_A + 3 * D_B])
    cv_ref[...] = (c * v).astype(bf16)
    g0 = D_A + 3 * D_B
    for n in range(2 * D_MODEL // D_B):
        acc = _dot(h, w_ref[:, g0 + n * D_B:g0 + (n + 1) * D_B])
        gates_ref[:, n * D_B:(n + 1) * D_B] = jax.nn.sigmoid(acc).astype(bf16)


def _in_proj(x, norm_g, w_in_b, tm):
    t = x.shape[0]
    return pl.pallas_call(
        _in_proj_kernel,
        grid=(t // tm,),
        in_specs=[
            pl.BlockSpec((tm, D_MODEL), lambda i: (i, 0)),
            _const_spec((1, D_MODEL)),
            _const_spec((D_MODEL, IN_COLS)),
        ],
        out_specs=[
            pl.BlockSpec((N_SLAB, tm, LANES), lambda i: (0, i, 0)),
            pl.BlockSpec((tm, D_B), lambda i: (i, 0)),
            pl.BlockSpec((tm, D_B), lambda i: (i, 0)),
            pl.BlockSpec((tm, 2 * D_MODEL), lambda i: (i, 0)),
        ],
        out_shape=[
            jax.ShapeDtypeStruct((N_SLAB, t, LANES), f32),
            jax.ShapeDtypeStruct((t, D_B), bf16),
            jax.ShapeDtypeStruct((t, D_B), bf16),
            jax.ShapeDtypeStruct((t, 2 * D_MODEL), bf16),
        ],
        compiler_params=_params(("arbitrary",)),
        name="in_proj",
    )(x, norm_g, w_in_b)


def _group_mask(shape, row0, row_shift, lane_shift):
    r = lax.broadcasted_iota(jnp.int32, shape, 0) + row0
    c = lax.broadcasted_iota(jnp.int32, shape, 1)
    return ((r >> row_shift) & (SLAB_G - 1)) == ((c >> lane_shift) & (SLAB_G - 1))


def _expand_tables(kc_ref, vc_ref, wc_refs, r_ref, m_scr, v_scr, w_scrs):
    rep = r_ref[...]
    bd = _dot(kc_ref[...], rep)
    bd = jnp.where(_group_mask(bd.shape, 0, 4, 4), bd, 0.0).astype(bf16)
    for t in range(S5_T):
        rows = slice(t * LANES, (t + 1) * LANES)
        if t:
            m_scr[rows, :t * LANES] = jnp.zeros((LANES, t * LANES), bf16)
        m_scr[rows, t * LANES:] = bd[:, :(S5_T - t) * LANES]
    step = MXU_DIM
    for r0 in range(0, S5_SL, step):
        vb = _dot(vc_ref[r0:r0 + step, :], rep)
        v_scr[r0:r0 + step, :] = jnp.where(_group_mask(vb.shape, r0, 7, 4), vb, 0.0).astype(bf16)
    for wc_ref, w_scr in zip(wc_refs, w_scrs):
        for r0 in range(0, S5_K, step):
            wt = jnp.concatenate([wc_ref[r0:r0 + step, :].astype(f32)] * SLAB_G, axis=1)
            w_scr[r0:r0 + step, :] = jnp.where(_group_mask(wt.shape, r0, 4, 7), wt, 0.0).astype(bf16)


def _s5_outputs(lhs, sprev, xs, m_scr, v_scr, d_ref, store):
    per = MXU_DIM // LANES
    for a in range(S5_T // per):
        k = (a + 1) * MXU_DIM
        cols = slice(a * MXU_DIM, (a + 1) * MXU_DIM)
        y = _dot(lhs[:, :k], m_scr[:k, cols]) + _dot(sprev, v_scr[:, cols])
        for i in range(per):
            t = a * per + i
            piece = y[:, i * LANES:(i + 1) * LANES] + d_ref[...] * xs[t]
            store(t, jax.nn.gelu(piece))


def _s5_prompt_kernel(u_ref, kc_ref, vc_ref, wc_ref, r_ref, pq_ref, d_ref, h0_ref, h0s_ref,
                      y_ref, sfin_ref,
                      m_scr, v_scr, w_scr, s_scr, t_scr, loc_scr, loct_scr, prev_scr, *, nb, cb):
    ci = pl.program_id(1)

    @pl.when(ci == 0)
    def _():
        _expand_tables(kc_ref, vc_ref, [wc_ref], r_ref, m_scr, v_scr, [w_scr])
        s_scr[...] = h0_ref[...]
        t_scr[...] = h0s_ref[...]
        loc_scr[...] = jnp.zeros_like(loc_scr)
        loct_scr[...] = jnp.zeros_like(loct_scr)

    xs = [jnp.concatenate([u_ref[b, pl.ds(t, cb, stride=S5_T), :] for b in range(nb)], axis=0)
          for t in range(S5_T)]
    lhs = jnp.concatenate(xs, axis=1).astype(bf16)

    loc = _dot(lhs, w_scr[...])
    for j in range(SLAB_G):
        lj = loc[:, j * S5_ST:(j + 1) * S5_ST]
        ljt = pltpu.roll(lj, N_STATE, axis=1)
        for b in range(nb):
            rows = slice(b * cb, (b + 1) * cb)
            loc_scr[j, pl.ds(b, cb, stride=SUBLANES), :] = lj[rows]
            loct_scr[j, pl.ds(b, cb, stride=SUBLANES), :] = ljt[rows]

    ss = [s_scr[j] for j in range(SLAB_G)]
    ts = [t_scr[j] for j in range(SLAB_G)]
    for c in range(cb):
        rows = slice(c * SUBLANES, (c + 1) * SUBLANES)
        for j in range(SLAB_G):
            p = pq_ref[0, j:j + 1, :]
            q = pq_ref[1, j:j + 1, :]
            prev_scr[j, rows, :] = ss[j]
            ss[j], ts[j] = (p * ss[j] + q * ts[j] + loc_scr[j, rows, :],
                            p * ts[j] - q * ss[j] + loct_scr[j, rows, :])
    for j in range(SLAB_G):
        s_scr[j] = ss[j]
        t_scr[j] = ts[j]
        sfin_ref[j] = ss[j]

    sprev = jnp.concatenate(
        [jnp.concatenate([prev_scr[j, pl.ds(b, cb, stride=SUBLANES), :] for b in range(nb)], axis=0)
         for j in range(SLAB_G)], axis=1).astype(bf16)

    def store(t, val):
        for b in range(nb):
            y_ref[b, pl.ds(t, cb, stride=S5_T), :] = val[b * cb:(b + 1) * cb]

    _s5_outputs(lhs, sprev, xs, m_scr, v_scr, d_ref, store)


def _s5_prompt(u, tables, h0, h0s, cb):
    kc, vc, wc, _, rep, pq, dvec = tables
    _, nb, length, _ = u.shape
    nc = length // S5_T
    kern = functools.partial(_s5_prompt_kernel, nb=nb, cb=cb)
    tok = pl.BlockSpec((None, nb, cb * S5_T, LANES), lambda k, ci: (k, 0, ci, 0))
    per_slab = lambda *shape: pl.BlockSpec((None,) + shape, lambda k, ci: (k,) + (0,) * len(shape))
    state = per_slab(SLAB_G, SUBLANES, S5_ST)
    return pl.pallas_call(
        kern,
        grid=(N_SLAB, nc // cb),
        in_specs=[tok, per_slab(LANES, S5_TQ), per_slab(S5_SL, S5_TQ), per_slab(S5_K, S5_ST),
                  _const_spec((S5_TQ, S5_K)),
                  per_slab(2, SLAB_G, S5_ST), per_slab(1, LANES), state, state],
        out_specs=[tok, state],
        out_shape=[
            jax.ShapeDtypeStruct(u.shape, f32),
            jax.ShapeDtypeStruct((N_SLAB, SLAB_G, SUBLANES, S5_ST), f32),
        ],
        scratch_shapes=[
            pltpu.VMEM((S5_K, S5_K), bf16),
            pltpu.VMEM((S5_SL, S5_K), bf16),
            pltpu.VMEM((S5_K, S5_SL), bf16),
            pltpu.VMEM((SLAB_G, SUBLANES, S5_ST), f32),
            pltpu.VMEM((SLAB_G, SUBLANES, S5_ST), f32),
            pltpu.VMEM((SLAB_G, cb * SUBLANES, S5_ST), f32),
            pltpu.VMEM((SLAB_G, cb * SUBLANES, S5_ST), f32),
            pltpu.VMEM((SLAB_G, cb * SUBLANES, S5_ST), f32),
        ],
        compiler_params=_params(("arbitrary", "arbitrary")),
        name="s5_prompt",
    )(u, kc, vc, wc, rep, pq, dvec, h0, h0s)


def _s5_sample_kernel(u_ref, kc_ref, vc_ref, wc_ref, wlc_ref, r_ref, pq_ref, d_ref, h0_ref, h0s_ref,
                      y_ref, sfin_ref, m_scr, v_scr, wh_scr, wl_scr, *, nb):
    _expand_tables(kc_ref, vc_ref, [wc_ref, wlc_ref], r_ref, m_scr, v_scr, [wh_scr, wl_scr])
    xs = [u_ref[pl.ds(t, nb, stride=S5_T), :] for t in range(S5_T)]
    lhs_f = jnp.concatenate(xs, axis=1)
    lhs = lhs_f.astype(bf16)
    lhs_lo = (lhs_f - lhs.astype(f32)).astype(bf16)
    wh = wh_scr[...]
    loc = _dot(lhs, wh) + _dot(lhs_lo, wh) + _dot(lhs, wl_scr[...])
    for j in range(SLAB_G):
        p = pq_ref[0, j:j + 1, :]
        q = pq_ref[1, j:j + 1, :]
        sfin_ref[j] = p * h0_ref[j] + q * h0s_ref[j] + loc[:, j * S5_ST:(j + 1) * S5_ST]
    sprev = jnp.concatenate([h0_ref[j] for j in range(SLAB_G)], axis=1).astype(bf16)

    def store(t, val):
        y_ref[pl.ds(t, nb, stride=S5_T), :] = val

    _s5_outputs(lhs, sprev, xs, m_scr, v_scr, d_ref, store)


def _s5_sample(u, tables, h0, h0s, nb):
    kc, vc, wc, wlc, rep, pq, dvec = tables
    kern = functools.partial(_s5_sample_kernel, nb=nb)
    per_slab = lambda *shape: pl.BlockSpec((None,) + shape, lambda k: (k,) + (0,) * len(shape))
    tok = per_slab(nb * S5_T, LANES)
    state = per_slab(SLAB_G, nb, S5_ST)
    return pl.pallas_call(
        kern,
        grid=(N_SLAB,),
        in_specs=[tok, per_slab(LANES, S5_TQ), per_slab(S5_SL, S5_TQ), per_slab(S5_K, S5_ST),
                  per_slab(S5_K, S5_ST), _const_spec((S5_TQ, S5_K)),
                  per_slab(2, SLAB_G, S5_ST), per_slab(1, LANES), state, state],
        out_specs=[tok, state],
        out_shape=[
            jax.ShapeDtypeStruct(u.shape, f32),
            jax.ShapeDtypeStruct((N_SLAB, SLAB_G, nb, S5_ST), f32),
        ],
        scratch_shapes=[
            pltpu.VMEM((S5_K, S5_K), bf16),
            pltpu.VMEM((S5_SL, S5_K), bf16),
            pltpu.VMEM((S5_K, S5_SL), bf16),
            pltpu.VMEM((S5_K, S5_SL), bf16),
        ],
        compiler_params=_params(("arbitrary",)),
        name="s5_sample",
    )(u, kc, vc, wc, wlc, rep, pq, dvec, h0, h0s)


def _replication_matrix():
    src = np.arange(S5_TQ)
    dst = np.arange(S5_K)
    same = ((src[:, None] // S5_GROUP == dst[None, :] // LANES)
            & (src[:, None] % S5_GROUP == dst[None, :] % S5_GROUP))
    return jnp.asarray(same, dtype=bf16)


def _s5_tables(lam_re, lam_im, log_dt, b_re, b_im, c_re, c_im, d_skip):
    hp = lax.Precision.HIGHEST
    dt = jnp.exp(log_dt.astype(f32))[:, None]
    lr, li = lam_re.astype(f32), lam_im.astype(f32)
    mag = jnp.exp(lr * dt)
    ab_re = mag * jnp.cos(li * dt)
    ab_im = mag * jnp.sin(li * dt)
    den = lr * lr + li * li
    nr, ni = ab_re - 1.0, ab_im
    f_re = (nr * lr + ni * li) / den
    f_im = (ni * lr - nr * li) / den
    br, bi = b_re.astype(f32), b_im.astype(f32)
    bb_re = f_re[..., None] * br - f_im[..., None] * bi
    bb_im = f_re[..., None] * bi + f_im[..., None] * br
    cr, ci = c_re.astype(f32), c_im.astype(f32)

    k = jnp.arange(S5_T + 1, dtype=f32)[:, None, None]
    pm = jnp.exp(lr * dt * k)
    pr = pm * jnp.cos(li * dt * k)
    pi = pm * jnp.sin(li * dt * k)

    er = cr[None] * pr[:, :, None, :] - ci[None] * pi[:, :, None, :]
    ei = cr[None] * pi[:, :, None, :] + ci[None] * pr[:, :, None, :]
    kk = (jnp.einsum('tgqn,gnp->tgqp', er[:S5_T], bb_re, precision=hp)
          - jnp.einsum('tgqn,gnp->tgqp', ei[:S5_T], bb_im, precision=hp))
    kc = (kk.reshape(S5_T, N_SLAB, SLAB_G, S5_GROUP, S5_GROUP)
          .transpose(1, 2, 4, 0, 3).reshape(N_SLAB, LANES, S5_TQ).astype(bf16))

    ti = jnp.arange(S5_T)
    prr = pr[S5_T - 1 - ti]
    pir = pi[S5_T - 1 - ti]
    w_re = prr[:, :, :, None] * bb_re[None] - pir[:, :, :, None] * bb_im[None]
    w_im = prr[:, :, :, None] * bb_im[None] + pir[:, :, :, None] * bb_re[None]
    w2 = jnp.concatenate([w_re, w_im], axis=2)
    w = (w2.reshape(S5_T, N_SLAB, SLAB_G, S5_ST, S5_GROUP)
         .transpose(1, 0, 2, 4, 3).reshape(N_SLAB, S5_K, S5_ST))
    wc = w.astype(bf16)
    wlc = (w - wc.astype(f32)).astype(bf16)

    v2 = jnp.concatenate([er[1:], -ei[1:]], axis=3)
    vc = (v2.reshape(S5_T, N_SLAB, SLAB_G, S5_GROUP, S5_ST)
          .transpose(1, 2, 4, 0, 3).reshape(N_SLAB, S5_SL, S5_TQ).astype(bf16))

    ar, ai = pr[S5_T], pi[S5_T]
    p_row = jnp.concatenate([ar, ar], axis=1).reshape(N_SLAB, 1, SLAB_G, S5_ST)
    q_row = jnp.concatenate([-ai, ai], axis=1).reshape(N_SLAB, 1, SLAB_G, S5_ST)
    pq = jnp.concatenate([p_row, q_row], axis=1)
    dvec = d_skip.astype(f32).reshape(N_SLAB, 1, LANES)
    return kc, vc, wc, wlc, _replication_matrix(), pq, dvec


def _s5_mixer(u, s_re, s_im, tables, cb):
    nb = s_re.shape[0]
    length = u.shape[1] // nb

    def to_slab(a):
        return a.reshape(nb, N_SLAB, SLAB_G, S5_ST).transpose(1, 2, 0, 3)

    h0 = to_slab(jnp.concatenate([s_re, s_im], axis=-1))
    h0s = to_slab(jnp.concatenate([s_im, s_re], axis=-1))
    if length == S5_T:
        y, sfin = _s5_sample(u, tables, h0, h0s, nb)
    else:
        pad = [(0, 0), (0, 0), (0, SUBLANES - nb), (0, 0)]
        y, sfin = _s5_prompt(u.reshape(N_SLAB, nb, length, LANES), tables,
                             jnp.pad(h0, pad), jnp.pad(h0s, pad), cb)
        y = y.reshape(u.shape)
        sfin = sfin[:, :, :nb]
    sfin = sfin.transpose(2, 0, 1, 3).reshape(nb, G_A, S5_ST)
    return y, sfin[..., :N_STATE], sfin[..., N_STATE:]


def _conv3(ext_ref, w_ref, b_ref, lt):
    lo = CONV_PAD - 2
    out = ext_ref[:, lo:lo + lt, :] * w_ref[0:1, :]
    out = out + ext_ref[:, lo + 1:lo + 1 + lt, :] * w_ref[1:2, :]
    out = out + ext_ref[:, lo + 2:lo + 2 + lt, :] * w_ref[2:3, :]
    return out + b_ref[...]


def _mix_kernel(ya_ref, gin_ref, cv_ref, gates_ref, x_ref, cache_ref,
                wglu_ref, wsc_ref, wo_ref, cw_ref, cb_ref,
                x1_ref, nsc_ref, ext_scr, mrg_scr, *, sb, lt, nchunk):
    i = pl.program_id(1)
    rows = sb * lt
    lo = CONV_PAD - 2

    @pl.when(i == 0)
    def _():
        ext_scr[:, lo:CONV_PAD, :] = cache_ref[...]

    @pl.when(i > 0)
    def _():
        ext_scr[:, lo:CONV_PAD, :] = ext_scr[:, lo + lt:CONV_PAD + lt, :]

    ext_scr[:, CONV_PAD:, :] = cv_ref[...].astype(f32)
    nsc_ref[...] = ext_scr[:, lo + lt:CONV_PAD + lt, :]

    conv = _conv3(ext_scr, cw_ref, cb_ref, lt)
    gated = (gin_ref[...].astype(f32) * conv).astype(bf16).reshape(rows, D_B)
    ya = jnp.concatenate([ya_ref[k].reshape(rows, LANES) for k in range(N_SLAB)],
                         axis=1).astype(bf16)
    wc = D_MODEL // nchunk
    for n in range(nchunk):
        cols = slice(n * wc, (n + 1) * wc)
        gcols = slice(D_MODEL + n * wc, D_MODEL + (n + 1) * wc)
        a = _dot(ya, wglu_ref[:, cols])
        gt = _dot(ya, wglu_ref[:, gcols])
        br_a = a * jax.nn.sigmoid(gt)
        br_b = _dot(gated, wsc_ref[:, cols])
        g_a = gates_ref[:, :, cols].reshape(rows, wc).astype(f32)
        g_b = gates_ref[:, :, gcols].reshape(rows, wc).astype(f32)
        mrg_scr[:, cols] = (g_a * br_a + g_b * br_b).astype(bf16)
    o = _dot(mrg_scr[...], wo_ref[...])
    x1_ref[...] = x_ref[...] + o.reshape(sb, lt, D_MODEL)


def _mix_out(ya, gin, cv, gates, x, cache, wglu, wsc, wo, cw, cbias, sb, lt):
    ns, length, _ = x.shape
    kern = functools.partial(_mix_kernel, sb=sb, lt=lt, nchunk=4)
    tile = lambda c: pl.BlockSpec((sb, lt, c), lambda s, i: (s, i, 0))
    return pl.pallas_call(
        kern,
        grid=(ns // sb, length // lt),
        in_specs=[
            pl.BlockSpec((N_SLAB, sb, lt, LANES), lambda s, i: (0, s, i, 0)),
            tile(D_B), tile(D_B), tile(2 * D_MODEL), tile(D_MODEL),
            pl.BlockSpec((sb, 2, D_B), lambda s, i: (s, 0, 0)),
            _const_spec((D_A, 2 * D_MODEL)),
            _const_spec((D_B, D_MODEL)),
            _const_spec((D_MODEL, D_MODEL)),
            _const_spec((3, D_B)),
            _const_spec((1, D_B)),
        ],
        out_specs=[
            tile(D_MODEL),
            pl.BlockSpec((sb, 2, D_B), lambda s, i: (s, 0, 0)),
        ],
        out_shape=[
            jax.ShapeDtypeStruct((ns, length, D_MODEL), f32),
            jax.ShapeDtypeStruct((ns, 2, D_B), f32),
        ],
        scratch_shapes=[
            pltpu.VMEM((sb, lt + CONV_PAD, D_B), f32),
            pltpu.VMEM((sb * lt, D_MODEL), bf16),
        ],
        compiler_params=_params(("arbitrary", "arbitrary")),
        name="mix_out",
    )(ya, gin, cv, gates, x, cache, wglu, wsc, wo, cw, cbias)


def _ffn_kernel(x1u_ref, x1d_ref, g2_ref, gf_ref, wv_ref, wg_ref, wd_ref,
                cwv_ref, cwg_ref, cbv_ref, cbg_ref, cachev_ref, cacheg_ref,
                y_ref, nfv_ref, nfg_ref,
                h2_scr, acc_scr, upv_scr, upg_scr, act_scr, extv_scr, extg_scr, carv_scr, carg_scr,
                *, sb, lt, nf, njobs):
    n = pl.program_id(1)
    rows = sb * lt
    lo = CONV_PAD - 2
    slot_u = lax.rem(n, 2)
    slot_c = 1 - slot_u
    f_u = lax.rem(jnp.minimum(n, njobs - 1), nf)
    job_c = jnp.clip(n - 1, 0, njobs - 1)
    f_c = lax.rem(job_c, nf)
    f_d = lax.rem(jnp.clip(n - 2, 0, njobs - 1), nf)

    @pl.when(n == 0)
    def _():
        upv_scr[...] = jnp.zeros_like(upv_scr)
        upg_scr[...] = jnp.zeros_like(upg_scr)
        act_scr[...] = jnp.zeros_like(act_scr)

    @pl.when(f_u == 0)
    def _():
        h2_scr[...] = _rmsnorm(x1u_ref[...].reshape(rows, D_MODEL), g2_ref[...]).astype(bf16)

    first_tile = job_c < nf

    def conv_branch(up_scr, cache_ref, car_scr, ext_scr, cw_ref, cb_ref):
        ext_scr[:, lo:CONV_PAD, :] = jnp.where(first_tile, cache_ref[...], car_scr[f_c])
        ext_scr[:, CONV_PAD:, :] = up_scr[slot_c].reshape(sb, lt, FFN_BLOCK)
        last = ext_scr[:, lo + lt:CONV_PAD + lt, :]
        car_scr[f_c] = last
        return _conv3(ext_scr, cw_ref, cb_ref, lt), last

    val, last_v = conv_branch(upv_scr, cachev_ref, carv_scr, extv_scr, cwv_ref, cbv_ref)
    gt, last_g = conv_branch(upg_scr, cacheg_ref, carg_scr, extg_scr, cwg_ref, cbg_ref)
    act = (jax.nn.silu(gt) * val).astype(bf16).reshape(rows, FFN_BLOCK)

    part = _dot(act_scr[slot_u], wd_ref[...])
    acc_scr[...] = jnp.where(f_d == 0, part, acc_scr[...] + part)
    act_scr[slot_c] = act

    h2 = h2_scr[...]
    upv_scr[slot_u] = _dot(h2, wv_ref[...])
    upg_scr[slot_u] = _dot(h2, wg_ref[...])

    @pl.when((n >= 1) & (n <= njobs))
    def _():
        nfv_ref[...] = last_v
        nfg_ref[...] = last_g

    @pl.when((n >= 2) & (f_d == nf - 1))
    def _():
        x2 = x1d_ref[...].reshape(rows, D_MODEL) + acc_scr[...]
        y_ref[...] = _rmsnorm(x2, gf_ref[...]).reshape(sb, lt, D_MODEL)


def _ffn(x1, g2, gfin, w_up_p, w_down_p, cw_p, cb_p, cache_p, sb, lt):
    ns, length, _ = x1.shape
    nf = D_FF_PAD // FFN_BLOCK
    njobs = (length // lt) * nf
    kern = functools.partial(_ffn_kernel, sb=sb, lt=lt, nf=nf, njobs=njobs)
    job = lambda n, lag: jnp.clip(n - lag, 0, njobs - 1)
    tile = lambda lag, **kw: pl.BlockSpec(
        (sb, lt, D_MODEL), lambda s, n: (s, job(n, lag) // nf, 0), **kw)
    row = _const_spec((1, D_MODEL))
    return pl.pallas_call(
        kern,
        grid=(ns // sb, njobs + 2),
        in_specs=[
            tile(0), tile(2, pipeline_mode=pl.Buffered(1)), row, row,
            pl.BlockSpec((D_MODEL, FFN_BLOCK), lambda s, n: (0, job(n, 0) % nf)),
            pl.BlockSpec((D_MODEL, FFN_BLOCK), lambda s, n: (0, nf + job(n, 0) % nf)),
            pl.BlockSpec((FFN_BLOCK, D_MODEL), lambda s, n: (job(n, 2) % nf, 0)),
            pl.BlockSpec((3, FFN_BLOCK), lambda s, n: (0, job(n, 1) % nf)),
            pl.BlockSpec((3, FFN_BLOCK), lambda s, n: (0, nf + job(n, 1) % nf)),
            pl.BlockSpec((1, FFN_BLOCK), lambda s, n: (0, job(n, 1) % nf)),
            pl.BlockSpec((1, FFN_BLOCK), lambda s, n: (0, nf + job(n, 1) % nf)),
            pl.BlockSpec((sb, 2, FFN_BLOCK), lambda s, n: (s, 0, job(n, 1) % nf)),
            pl.BlockSpec((sb, 2, FFN_BLOCK), lambda s, n: (s, 0, nf + job(n, 1) % nf)),
        ],
        out_specs=[
            tile(2),
            pl.BlockSpec((sb, None, 2, FFN_BLOCK),
                         lambda s, n: (s, job(n, 1) // nf, 0, job(n, 1) % nf)),
            pl.BlockSpec((sb, None, 2, FFN_BLOCK),
                         lambda s, n: (s, job(n, 1) // nf, 0, job(n, 1) % nf)),
        ],
        out_shape=[
            jax.ShapeDtypeStruct((ns, length, D_MODEL), f32),
            jax.ShapeDtypeStruct((ns, length // lt, 2, D_FF_PAD), f32),
            jax.ShapeDtypeStruct((ns, length // lt, 2, D_FF_PAD), f32),
        ],
        scratch_shapes=[
            pltpu.VMEM((sb * lt, D_MODEL), bf16),
            pltpu.VMEM((sb * lt, D_MODEL), f32),
            pltpu.VMEM((2, sb * lt, FFN_BLOCK), f32),
            pltpu.VMEM((2, sb * lt, FFN_BLOCK), f32),
            pltpu.VMEM((2, sb * lt, FFN_BLOCK), bf16),
            pltpu.VMEM((sb, lt + CONV_PAD, FFN_BLOCK), f32),
            pltpu.VMEM((sb, lt + CONV_PAD, FFN_BLOCK), f32),
            pltpu.VMEM((nf, sb, 2, FFN_BLOCK), f32),
            pltpu.VMEM((nf, sb, 2, FFN_BLOCK), f32),
        ],
        compiler_params=_params(("arbitrary", "arbitrary")),
        name="ffn",
    )(x1, x1, g2, gfin, w_up_p, w_up_p, w_down_p, cw_p, cw_p, cb_p, cb_p, cache_p, cache_p)


def _pad_ff(a, axis):
    pad = [(0, 0)] * a.ndim
    pad[axis] = (0, D_FF_PAD - D_FF)
    if a.shape[axis] == D_FF:
        return jnp.pad(a, pad)
    lo, hi = jnp.split(a, 2, axis=axis)
    return jnp.concatenate([jnp.pad(lo, pad), jnp.pad(hi, pad)], axis=axis)


def _trunk(x, s_re, s_im, sc_buf, ffn_buf, w, cfg):
    ns, length, _ = x.shape
    tm_in, sb_mix, lt_mix, sb_ffn, lt_ffn, cb = cfg
    xf = x.reshape(ns * length, D_MODEL)
    u_a, gin, cv, gates = _in_proj(xf, w["norm1_g"], w["w_in"], tm_in)
    ya, n_re, n_im = _s5_mixer(u_a, s_re, s_im, w["s5"], cb)
    r3 = lambda a: a.reshape(ns, length, a.shape[-1])
    x1, new_sc = _mix_out(ya.reshape(N_SLAB, ns, length, LANES), r3(gin), r3(cv), r3(gates), x,
                          sc_buf, w["w_glu"], w["w_sc_out"], w["w_o"],
                          w["sc_conv_w"], w["sc_conv_b"], sb_mix, lt_mix)
    y, nfv, nfg = _ffn(x1, w["norm2_g"], w["final_norm_g"], w["w_up"], w["w_down"],
                       w["ffn_conv_w"], w["ffn_conv_b"], _pad_ff(ffn_buf, 2), sb_ffn, lt_ffn)
    new_ffn = jnp.concatenate([nfv[:, -1, :, :D_FF], nfg[:, -1, :, :D_FF]], axis=-1)
    return y, n_re[None], n_im[None], new_sc[None], new_ffn[None]


def kernel(x_prompt, x_sample, state_s5_re, state_s5_im, cache_sc_conv, cache_ffn_conv, norm1_g, w_in, lam_re, lam_im, log_dt, b_re, b_im, c_re, c_im, d_skip, w_glu, sc_conv_w, sc_conv_b, w_sc_out, w_o, norm2_g, w_up, ffn_conv_w, ffn_conv_b, w_down, final_norm_g):
    w = {
        "norm1_g": norm1_g[0].reshape(1, D_MODEL),
        "w_in": w_in[0].astype(bf16),
        "s5": _s5_tables(lam_re[0], lam_im[0], log_dt[0], b_re[0], b_im[0],
                         c_re[0], c_im[0], d_skip[0]),
        "w_glu": w_glu[0].astype(bf16),
        "sc_conv_w": sc_conv_w[0],
        "sc_conv_b": sc_conv_b[0].reshape(1, D_B),
        "w_sc_out": w_sc_out[0].astype(bf16),
        "w_o": w_o[0].astype(bf16),
        "norm2_g": norm2_g[0].reshape(1, D_MODEL),
        "w_up": _pad_ff(w_up[0], 1).astype(bf16),
        "ffn_conv_w": _pad_ff(ffn_conv_w[0], 1),
        "ffn_conv_b": _pad_ff(ffn_conv_b[0].reshape(1, 2 * D_FF), 1),
        "w_down": _pad_ff(w_down[0], 0).astype(bf16),
        "final_norm_g": final_norm_g.reshape(1, D_MODEL),
    }
    bp, lp, _ = x_prompt.shape
    bs, ls, _ = x_sample.shape
    zeros = lambda *s: jnp.zeros(s, f32)
    p_cfg = (256, 1, 256, 1, 512, 64)
    yp, p_re, p_im, p_sc, p_ffn = _trunk(
        x_prompt, zeros(bp, G_A, N_STATE), zeros(bp, G_A, N_STATE),
        zeros(bp, 2, D_B), zeros(bp, 2, 2 * D_FF), w, p_cfg)
    s_cfg = (256, bs // 2, ls, bs, ls, 1)
    ys, s_re, s_im, s_sc, s_ffn = _trunk(
        x_sample, state_s5_re[0], state_s5_im[0], cache_sc_conv[0], cache_ffn_conv[0], w, s_cfg)
    return (yp, ys, p_re, p_im, p_sc, p_ffn, s_re, s_im, s_sc, s_ffn)
```

```python
import functools

import jax
import jax.numpy as jnp
import numpy as np
from jax import lax
from jax.experimental import pallas as pl
from jax.experimental.pallas import tpu as pltpu

D_MODEL = 2048
D_A = D_MODEL // 2
S5_GROUP = 16
G_A = D_A // S5_GROUP
N_STATE = 64
D_B = D_MODEL // 2
D_FF = 5504
EPS = 1e-6
IN_COLS = D_A + 3 * D_B + 2 * D_MODEL

LANES = 128
SUBLANES = 8
MXU_DIM = 256
VMEM_LIMIT_BYTES = 58 * 1024 * 1024

N_SLAB = D_A // LANES
SLAB_G = LANES // S5_GROUP
S5_T = 16
S5_TQ = S5_T * S5_GROUP
S5_K = S5_T * LANES
S5_ST = 2 * N_STATE
S5_SL = SLAB_G * S5_ST
D_FF_PAD = 5632
FFN_HALF = D_FF_PAD // 2
CONV_PAD = SUBLANES

bf16 = jnp.bfloat16
f32 = jnp.float32


def _rmsnorm(x, g):
    ms = jnp.mean(x * x, axis=-1, keepdims=True)
    return x * lax.rsqrt(ms + EPS) * g


def _params(sem):
    return pltpu.CompilerParams(dimension_semantics=sem, vmem_limit_bytes=VMEM_LIMIT_BYTES)


def _const_spec(shape):
    nd = len(shape)
    return pl.BlockSpec(shape, lambda *_: (0,) * nd, pipeline_mode=pl.Buffered(1))


def _dot(a, b):
    return jnp.dot(a, b, preferred_element_type=f32)


def _in_proj_kernel(x_ref, g_ref, w_ref, u_ref, gin_ref, cv_ref, gates_ref):
    h = _rmsnorm(x_ref[...], g_ref[...]).astype(bf16)
    u = _dot(h, w_ref[:, :D_A])
    for k in range(N_SLAB):
        u_ref[k] = u[:, k * LANES:(k + 1) * LANES]
    gin_ref[...] = _dot(h, w_ref[:, D_A:D_A + D_B]).astype(bf16)
    c = _dot(h, w_ref[:, D_A + D_B:D_A + 2 * D_B])
    v = _dot(h, w_ref[:, D_A + 2 * D_B:D_A + 3 * D_B])
    cv_ref[...] = (c * v).astype(bf16)
    g0 = D_A + 3 * D_B
    for n in range(2 * D_MODEL // D_B):
        acc = _dot(h, w_ref[:, g0 + n * D_B:g0 + (n + 1) * D_B])
        gates_ref[:, n * D_B:(n + 1) * D_B] = jax.nn.sigmoid(acc).astype(bf16)


def _in_proj(x, norm_g, w_in_b, tm):
    t = x.shape[0]
    return pl.pallas_call(
        _in_proj_kernel,
        grid=(t // tm,),
        in_specs=[
            pl.BlockSpec((tm, D_MODEL), lambda i: (i, 0)),
            _const_spec((1, D_MODEL)),
            _const_spec((D_MODEL, IN_COLS)),
        ],
        out_specs=[
            pl.BlockSpec((N_SLAB, tm, LANES), lambda i: (0, i, 0)),
            pl.BlockSpec((tm, D_B), lambda i: (i, 0)),
            pl.BlockSpec((tm, D_B), lambda i: (i, 0)),
            pl.BlockSpec((tm, 2 * D_MODEL), lambda i: (i, 0)),
        ],
        out_shape=[
            jax.ShapeDtypeStruct((N_SLAB, t, LANES), f32),
            jax.ShapeDtypeStruct((t, D_B), bf16),
            jax.ShapeDtypeStruct((t, D_B), bf16),
            jax.ShapeDtypeStruct((t, 2 * D_MODEL), bf16),
        ],
        compiler_params=_params(("arbitrary",)),
        name="in_proj",
    )(x, norm_g, w_in_b)


def _group_mask(shape, row0, row_shift, lane_shift):
    r = lax.broadcasted_iota(jnp.int32, shape, 0) + row0
    c = lax.broadcasted_iota(jnp.int32, shape, 1)
    return ((r >> row_shift) & (SLAB_G - 1)) == ((c >> lane_shift) & (SLAB_G - 1))


def _expand_tables(kc_ref, vc_ref, wc_refs, r_ref, m_scr, v_scr, w_scrs):
    rep = r_ref[...]
    bd = _dot(kc_ref[...], rep)
    bd = jnp.where(_group_mask(bd.shape, 0, 4, 4), bd, 0.0).astype(bf16)
    for t in range(S5_T):
        rows = slice(t * LANES, (t + 1) * LANES)
        if t:
            m_scr[rows, :t * LANES] = jnp.zeros((LANES, t * LANES), bf16)
        m_scr[rows, t * LANES:] = bd[:, :(S5_T - t) * LANES]
    step = MXU_DIM
    for r0 in range(0, S5_SL, step):
        vb = _dot(vc_ref[r0:r0 + step, :], rep)
        v_scr[r0:r0 + step, :] = jnp.where(_group_mask(vb.shape, r0, 7, 4), vb, 0.0).astype(bf16)
    for wc_ref, w_scr in zip(wc_refs, w_scrs):
        for r0 in range(0, S5_K, step):
            wt = jnp.concatenate([wc_ref[r0:r0 + step, :].astype(f32)] * SLAB_G, axis=1)
            w_scr[r0:r0 + step, :] = jnp.where(_group_mask(wt.shape, r0, 4, 7), wt, 0.0).astype(bf16)


def _s5_outputs(lhs, sprev, xs, m_scr, v_scr, d_ref, store):
    per = MXU_DIM // LANES
    for a in range(S5_T // per):
        k = (a + 1) * MXU_DIM
        cols = slice(a * MXU_DIM, (a + 1) * MXU_DIM)
        y = _dot(lhs[:, :k], m_scr[:k, cols]) + _dot(sprev, v_scr[:, cols])
        for i in range(per):
            t = a * per + i
            piece = y[:, i * LANES:(i + 1) * LANES] + d_ref[...] * xs[t]
            store(t, jax.nn.gelu(piece))


def _s5_prompt_kernel(u_ref, kc_ref, vc_ref, wc_ref, r_ref, pq_ref, d_ref, h0_ref, h0s_ref,
                      y_ref, sfin_ref,
                      m_scr, v_scr, w_scr, s_scr, t_scr, loc_scr, loct_scr, prev_scr, *, nb, cb):
    ci = pl.program_id(1)

    @pl.when(ci == 0)
    def _():
        _expand_tables(kc_ref, vc_ref, [wc_ref], r_ref, m_scr, v_scr, [w_scr])
        s_scr[...] = h0_ref[...]
        t_scr[...] = h0s_ref[...]
        loc_scr[...] = jnp.zeros_like(loc_scr)
        loct_scr[...] = jnp.zeros_like(loct_scr)

    xs = [jnp.concatenate([u_ref[b, pl.ds(t, cb, stride=S5_T), :] for b in range(nb)], axis=0)
          for t in range(S5_T)]
    lhs = jnp.concatenate(xs, axis=1).astype(bf16)

    loc = _dot(lhs, w_scr[...])
    for j in range(SLAB_G):
        lj = loc[:, j * S5_ST:(j + 1) * S5_ST]
        ljt = pltpu.roll(lj, N_STATE, axis=1)
        for b in range(nb):
            rows = slice(b * cb, (b + 1) * cb)
            loc_scr[j, pl.ds(b, cb, stride=SUBLANES), :] = lj[rows]
            loct_scr[j, pl.ds(b, cb, stride=SUBLANES), :] = ljt[rows]

    ss = [s_scr[j] for j in range(SLAB_G)]
    ts = [t_scr[j] for j in range(SLAB_G)]
    for c in range(cb):
        rows = slice(c * SUBLANES, (c + 1) * SUBLANES)
        for j in range(SLAB_G):
            p = pq_ref[0, j:j + 1, :]
            q = pq_ref[1, j:j + 1, :]
            prev_scr[j, rows, :] = ss[j]
            ss[j], ts[j] = (p * ss[j] + q * ts[j] + loc_scr[j, rows, :],
                            p * ts[j] - q * ss[j] + loct_scr[j, rows, :])
    for j in range(SLAB_G):
        s_scr[j] = ss[j]
        t_scr[j] = ts[j]
        sfin_ref[j] = ss[j]

    sprev = jnp.concatenate(
        [jnp.concatenate([prev_scr[j, pl.ds(b, cb, stride=SUBLANES), :] for b in range(nb)], axis=0)
         for j in range(SLAB_G)], axis=1).astype(bf16)

    def store(t, val):
        for b in range(nb):
            y_ref[b, pl.ds(t, cb, stride=S5_T), :] = val[b * cb:(b + 1) * cb]

    _s5_outputs(lhs, sprev, xs, m_scr, v_scr, d_ref, store)


def _s5_prompt(u, tables, h0, h0s, cb):
    kc, vc, wc, _, rep, pq, dvec = tables
    _, nb, length, _ = u.shape
    nc = length // S5_T
    kern = functools.partial(_s5_prompt_kernel, nb=nb, cb=cb)
    tok = pl.BlockSpec((None, nb, cb * S5_T, LANES), lambda k, ci: (k, 0, ci, 0))
    per_slab = lambda *shape: pl.BlockSpec((None,) + shape, lambda k, ci: (k,) + (0,) * len(shape))
    state = per_slab(SLAB_G, SUBLANES, S5_ST)
    return pl.pallas_call(
        kern,
        grid=(N_SLAB, nc // cb),
        in_specs=[tok, per_slab(LANES, S5_TQ), per_slab(S5_SL, S5_TQ), per_slab(S5_K, S5_ST),
                  _const_spec((S5_TQ, S5_K)),
                  per_slab(2, SLAB_G, S5_ST), per_slab(1, LANES), state, state],
        out_specs=[tok, state],
        out_shape=[
            jax.ShapeDtypeStruct(u.shape, f32),
            jax.ShapeDtypeStruct((N_SLAB, SLAB_G, SUBLANES, S5_ST), f32),
        ],
        scratch_shapes=[
            pltpu.VMEM((S5_K, S5_K), bf16),
            pltpu.VMEM((S5_SL, S5_K), bf16),
            pltpu.VMEM((S5_K, S5_SL), bf16),
            pltpu.VMEM((SLAB_G, SUBLANES, S5_ST), f32),
            pltpu.VMEM((SLAB_G, SUBLANES, S5_ST), f32),
            pltpu.VMEM((SLAB_G, cb * SUBLANES, S5_ST), f32),
            pltpu.VMEM((SLAB_G, cb * SUBLANES, S5_ST), f32),
            pltpu.VMEM((SLAB_G, cb * SUBLANES, S5_ST), f32),
        ],
        compiler_params=_params(("arbitrary", "arbitrary")),
        name="s5_prompt",
    )(u, kc, vc, wc, rep, pq, dvec, h0, h0s)


def _s5_sample_kernel(u_ref, kc_ref, vc_ref, wc_ref, wlc_ref, r_ref, pq_ref, d_ref, h0_ref, h0s_ref,
                      y_ref, sfin_ref, m_scr, v_scr, wh_scr, wl_scr, *, nb):
    _expand_tables(kc_ref, vc_ref, [wc_ref, wlc_ref], r_ref, m_scr, v_scr, [wh_scr, wl_scr])
    xs = [u_ref[pl.ds(t, nb, stride=S5_T), :] for t in range(S5_T)]
    lhs_f = jnp.concatenate(xs, axis=1)
    lhs = lhs_f.astype(bf16)
    lhs_lo = (lhs_f - lhs.astype(f32)).astype(bf16)
    wh = wh_scr[...]
    loc = _dot(lhs, wh) + _dot(lhs_lo, wh) + _dot(lhs, wl_scr[...])
    for j in range(SLAB_G):
        p = pq_ref[0, j:j + 1, :]
        q = pq_ref[1, j:j + 1, :]
        sfin_ref[j] = p * h0_ref[j] + q * h0s_ref[j] + loc[:, j * S5_ST:(j + 1) * S5_ST]
    sprev = jnp.concatenate([h0_ref[j] for j in range(SLAB_G)], axis=1).astype(bf16)

    def store(t, val):
        y_ref[pl.ds(t, nb, stride=S5_T), :] = val

    _s5_outputs(lhs, sprev, xs, m_scr, v_scr, d_ref, store)


def _s5_sample(u, tables, h0, h0s, nb):
    kc, vc, wc, wlc, rep, pq, dvec = tables
    kern = functools.partial(_s5_sample_kernel, nb=nb)
    per_slab = lambda *shape: pl.BlockSpec((None,) + shape, lambda k: (k,) + (0,) * len(shape))
    tok = per_slab(nb * S5_T, LANES)
    state = per_slab(SLAB_G, nb, S5_ST)
    return pl.pallas_call(
        kern,
        grid=(N_SLAB,),
        in_specs=[tok, per_slab(LANES, S5_TQ), per_slab(S5_SL, S5_TQ), per_slab(S5_K, S5_ST),
                  per_slab(S5_K, S5_ST), _const_spec((S5_TQ, S5_K)),
                  per_slab(2, SLAB_G, S5_ST), per_slab(1, LANES), state, state],
        out_specs=[tok, state],
        out_shape=[
            jax.ShapeDtypeStruct(u.shape, f32),
            jax.ShapeDtypeStruct((N_SLAB, SLAB_G, nb, S5_ST), f32),
        ],
        scratch_shapes=[
            pltpu.VMEM((S5_K, S5_K), bf16),
            pltpu.VMEM((S5_SL, S5_K), bf16),
            pltpu.VMEM((S5_K, S5_SL), bf16),
            pltpu.VMEM((S5_K, S5_SL), bf16),
        ],
        compiler_params=_params(("arbitrary",)),
        name="s5_sample",
    )(u, kc, vc, wc, wlc, rep, pq, dvec, h0, h0s)


def _replication_matrix():
    src = np.arange(S5_TQ)
    dst = np.arange(S5_K)
    same = ((src[:, None] // S5_GROUP == dst[None, :] // LANES)
            & (src[:, None] % S5_GROUP == dst[None, :] % S5_GROUP))
    return jnp.asarray(same, dtype=bf16)


def _s5_tables(lam_re, lam_im, log_dt, b_re, b_im, c_re, c_im, d_skip):
    hp = lax.Precision.HIGHEST
    dt = jnp.exp(log_dt.astype(f32))[:, None]
    lr, li = lam_re.astype(f32), lam_im.astype(f32)
    mag = jnp.exp(lr * dt)
    ab_re = mag * jnp.cos(li * dt)
    ab_im = mag * jnp.sin(li * dt)
    den = lr * lr + li * li
    nr, ni = ab_re - 1.0, ab_im
    f_re = (nr * lr + ni * li) / den
    f_im = (ni * lr - nr * li) / den
    br, bi = b_re.astype(f32), b_im.astype(f32)
    bb_re = f_re[..., None] * br - f_im[..., None] * bi
    bb_im = f_re[..., None] * bi + f_im[..., None] * br
    cr, ci = c_re.astype(f32), c_im.astype(f32)

    k = jnp.arange(S5_T + 1, dtype=f32)[:, None, None]
    pm = jnp.exp(lr * dt * k)
    pr = pm * jnp.cos(li * dt * k)
    pi = pm * jnp.sin(li * dt * k)

    er = cr[None] * pr[:, :, None, :] - ci[None] * pi[:, :, None, :]
    ei = cr[None] * pi[:, :, None, :] + ci[None] * pr[:, :, None, :]
    kk = (jnp.einsum('tgqn,gnp->tgqp', er[:S5_T], bb_re, precision=hp)
          - jnp.einsum('tgqn,gnp->tgqp', ei[:S5_T], bb_im, precision=hp))
    kc = (kk.reshape(S5_T, N_SLAB, SLAB_G, S5_GROUP, S5_GROUP)
          .transpose(1, 2, 4, 0, 3).reshape(N_SLAB, LANES, S5_TQ).astype(bf16))

    ti = jnp.arange(S5_T)
    prr = pr[S5_T - 1 - ti]
    pir = pi[S5_T - 1 - ti]
    w_re = prr[:, :, :, None] * bb_re[None] - pir[:, :, :, None] * bb_im[None]
    w_im = prr[:, :, :, None] * bb_im[None] + pir[:, :, :, None] * bb_re[None]
    w2 = jnp.concatenate([w_re, w_im], axis=2)
    w = (w2.reshape(S5_T, N_SLAB, SLAB_G, S5_ST, S5_GROUP)
         .transpose(1, 0, 2, 4, 3).reshape(N_SLAB, S5_K, S5_ST))
    wc = w.astype(bf16)
    wlc = (w - wc.astype(f32)).astype(bf16)

    v2 = jnp.concatenate([er[1:], -ei[1:]], axis=3)
    vc = (v2.reshape(S5_T, N_SLAB, SLAB_G, S5_GROUP, S5_ST)
          .transpose(1, 2, 4, 0, 3).reshape(N_SLAB, S5_SL, S5_TQ).astype(bf16))

    ar, ai = pr[S5_T], pi[S5_T]
    p_row = jnp.concatenate([ar, ar], axis=1).reshape(N_SLAB, 1, SLAB_G, S5_ST)
    q_row = jnp.concatenate([-ai, ai], axis=1).reshape(N_SLAB, 1, SLAB_G, S5_ST)
    pq = jnp.concatenate([p_row, q_row], axis=1)
    dvec = d_skip.astype(f32).reshape(N_SLAB, 1, LANES)
    return kc, vc, wc, wlc, _replication_matrix(), pq, dvec


def _s5_mixer(u, s_re, s_im, tables, cb):
    nb = s_re.shape[0]
    length = u.shape[1] // nb

    def to_slab(a):
        return a.reshape(nb, N_SLAB, SLAB_G, S5_ST).transpose(1, 2, 0, 3)

    h0 = to_slab(jnp.concatenate([s_re, s_im], axis=-1))
    h0s = to_slab(jnp.concatenate([s_im, s_re], axis=-1))
    if length == S5_T:
        y, sfin = _s5_sample(u, tables, h0, h0s, nb)
    else:
        pad = [(0, 0), (0, 0), (0, SUBLANES - nb), (0, 0)]
        y, sfin = _s5_prompt(u.reshape(N_SLAB, nb, length, LANES), tables,
                             jnp.pad(h0, pad), jnp.pad(h0s, pad), cb)
        y = y.reshape(u.shape)
        sfin = sfin[:, :, :nb]
    sfin = sfin.transpose(2, 0, 1, 3).reshape(nb, G_A, S5_ST)
    return y, sfin[..., :N_STATE], sfin[..., N_STATE:]


def _conv3(ext_ref, w_ref, b_ref, lt, cols=slice(None)):
    lo = CONV_PAD - 2
    out = ext_ref[:, lo:lo + lt, cols] * w_ref[0:1, cols]
    out = out + ext_ref[:, lo + 1:lo + 1 + lt, cols] * w_ref[1:2, cols]
    out = out + ext_ref[:, lo + 2:lo + 2 + lt, cols] * w_ref[2:3, cols]
    return out + b_ref[:, cols]


def _mix_kernel(ya_ref, gin_ref, cv_ref, gates_ref, x_ref, cache_ref,
                wglu_ref, wsc_ref, wo_ref, cw_ref, cb_ref, g2_ref,
                x1_ref, h2_ref, nsc_ref, ext_scr, mrg_scr, *, sb, lt, nchunk):
    i = pl.program_id(1)
    rows = sb * lt
    lo = CONV_PAD - 2

    @pl.when(i == 0)
    def _():
        ext_scr[:, lo:CONV_PAD, :] = cache_ref[...]

    @pl.when(i > 0)
    def _():
        ext_scr[:, lo:CONV_PAD, :] = ext_scr[:, lo + lt:CONV_PAD + lt, :]

    ext_scr[:, CONV_PAD:, :] = cv_ref[...].astype(f32)
    nsc_ref[...] = ext_scr[:, lo + lt:CONV_PAD + lt, :]

    conv = _conv3(ext_scr, cw_ref, cb_ref, lt)
    gated = (gin_ref[...].astype(f32) * conv).astype(bf16).reshape(rows, D_B)
    ya = jnp.concatenate([ya_ref[k].reshape(rows, LANES) for k in range(N_SLAB)],
                         axis=1).astype(bf16)
    wc = D_MODEL // nchunk
    for n in range(nchunk):
        cols = slice(n * wc, (n + 1) * wc)
        gcols = slice(D_MODEL + n * wc, D_MODEL + (n + 1) * wc)
        a = _dot(ya, wglu_ref[:, cols])
        gt = _dot(ya, wglu_ref[:, gcols])
        br_a = a * jax.nn.sigmoid(gt)
        br_b = _dot(gated, wsc_ref[:, cols])
        g_a = gates_ref[:, :, cols].reshape(rows, wc).astype(f32)
        g_b = gates_ref[:, :, gcols].reshape(rows, wc).astype(f32)
        mrg_scr[:, cols] = (g_a * br_a + g_b * br_b).astype(bf16)
    x1 = x_ref[...].reshape(rows, D_MODEL) + _dot(mrg_scr[...], wo_ref[...])
    x1_ref[...] = x1.reshape(sb, lt, D_MODEL)
    h2_ref[...] = _rmsnorm(x1, g2_ref[...]).astype(bf16).reshape(sb, lt, D_MODEL)


def _mix_out(ya, gin, cv, gates, x, cache, wglu, wsc, wo, cw, cbias, g2, sb, lt):
    ns, length, _ = x.shape
    kern = functools.partial(_mix_kernel, sb=sb, lt=lt, nchunk=4)
    tile = lambda c: pl.BlockSpec((sb, lt, c), lambda s, i: (s, i, 0))
    return pl.pallas_call(
        kern,
        grid=(ns // sb, length // lt),
        in_specs=[
            pl.BlockSpec((N_SLAB, sb, lt, LANES), lambda s, i: (0, s, i, 0)),
            tile(D_B), tile(D_B), tile(2 * D_MODEL), tile(D_MODEL),
            pl.BlockSpec((sb, 2, D_B), lambda s, i: (s, 0, 0)),
            _const_spec((D_A, 2 * D_MODEL)),
            _const_spec((D_B, D_MODEL)),
            _const_spec((D_MODEL, D_MODEL)),
            _const_spec((3, D_B)),
            _const_spec((1, D_B)),
            _const_spec((1, D_MODEL)),
        ],
        out_specs=[
            tile(D_MODEL),
            tile(D_MODEL),
            pl.BlockSpec((sb, 2, D_B), lambda s, i: (s, 0, 0)),
        ],
        out_shape=[
            jax.ShapeDtypeStruct((ns, length, D_MODEL), f32),
            jax.ShapeDtypeStruct((ns, length, D_MODEL), bf16),
            jax.ShapeDtypeStruct((ns, 2, D_B), f32),
        ],
        scratch_shapes=[
            pltpu.VMEM((sb, lt + CONV_PAD, D_B), f32),
            pltpu.VMEM((sb * lt, D_MODEL), bf16),
        ],
        compiler_params=_params(("arbitrary", "arbitrary")),
        name="mix_out",
    )(ya, gin, cv, gates, x, cache, wglu, wsc, wo, cw, cbias, g2)


def _ffn_up_kernel(h2_ref, wv_ref, wg_ref, cwv_ref, cwg_ref, cbv_ref, cbg_ref,
                   cachev_ref, cacheg_ref, act_ref, nfv_ref, nfg_ref, extv_scr, extg_scr,
                   *, sb, lt):
    i = pl.program_id(2)
    rows = sb * lt
    lo = CONV_PAD - 2

    @pl.when(i == 0)
    def _():
        extv_scr[:, lo:CONV_PAD, :] = cachev_ref[...]
        extg_scr[:, lo:CONV_PAD, :] = cacheg_ref[...]

    @pl.when(i > 0)
    def _():
        extv_scr[:, lo:CONV_PAD, :] = extv_scr[:, lo + lt:CONV_PAD + lt, :]
        extg_scr[:, lo:CONV_PAD, :] = extg_scr[:, lo + lt:CONV_PAD + lt, :]

    h2 = h2_ref[...].reshape(rows, D_MODEL)
    for c in range(FFN_HALF // MXU_DIM):
        cols = slice(c * MXU_DIM, (c + 1) * MXU_DIM)
        extv_scr[:, CONV_PAD:, cols] = _dot(h2, wv_ref[:, cols]).reshape(sb, lt, MXU_DIM)
        extg_scr[:, CONV_PAD:, cols] = _dot(h2, wg_ref[:, cols]).reshape(sb, lt, MXU_DIM)
        val = _conv3(extv_scr, cwv_ref, cbv_ref, lt, cols)
        gt = _conv3(extg_scr, cwg_ref, cbg_ref, lt, cols)
        act_ref[:, :, cols] = (jax.nn.silu(gt) * val).astype(bf16)
    nfv_ref[...] = extv_scr[:, lo + lt:CONV_PAD + lt, :]
    nfg_ref[...] = extg_scr[:, lo + lt:CONV_PAD + lt, :]


def _ffn_down_kernel(act_ref, x1_ref, wd_ref, gf_ref, y_ref, *, sb, lt):
    rows = sb * lt
    x2 = (x1_ref[...].reshape(rows, D_MODEL)
          + _dot(act_ref[...].reshape(rows, D_FF_PAD), wd_ref[...]))
    y_ref[...] = _rmsnorm(x2, gf_ref[...]).reshape(sb, lt, D_MODEL)


def _ffn(h2, x1, gfin, w_up_p, w_down_p, cw_p, cb_p, cache_p, sb, lt):
    ns, length, _ = x1.shape
    nh = D_FF_PAD // FFN_HALF
    nt = length // lt
    resident = dict(pipeline_mode=pl.Buffered(1))
    val = lambda rows, **kw: pl.BlockSpec((rows, FFN_HALF), lambda h, s, i: (0, h), **kw)
    gate = lambda rows, **kw: pl.BlockSpec((rows, FFN_HALF), lambda h, s, i: (0, nh + h), **kw)
    act, nfv, nfg = pl.pallas_call(
        functools.partial(_ffn_up_kernel, sb=sb, lt=lt),
        grid=(nh, ns // sb, nt),
        in_specs=[
            pl.BlockSpec((sb, lt, D_MODEL), lambda h, s, i: (s, i, 0)),
            val(D_MODEL, **resident), gate(D_MODEL, **resident),
            val(3), gate(3), val(1), gate(1),
            pl.BlockSpec((sb, 2, FFN_HALF), lambda h, s, i: (s, 0, h)),
            pl.BlockSpec((sb, 2, FFN_HALF), lambda h, s, i: (s, 0, nh + h)),
        ],
        out_specs=[
            pl.BlockSpec((sb, lt, FFN_HALF), lambda h, s, i: (s, i, h)),
            pl.BlockSpec((sb, None, 2, FFN_HALF), lambda h, s, i: (s, i, 0, h)),
            pl.BlockSpec((sb, None, 2, FFN_HALF), lambda h, s, i: (s, i, 0, h)),
        ],
        out_shape=[
            jax.ShapeDtypeStruct((ns, length, D_FF_PAD), bf16),
            jax.ShapeDtypeStruct((ns, nt, 2, D_FF_PAD), f32),
            jax.ShapeDtypeStruct((ns, nt, 2, D_FF_PAD), f32),
        ],
        scratch_shapes=[
            pltpu.VMEM((sb, lt + CONV_PAD, FFN_HALF), f32),
            pltpu.VMEM((sb, lt + CONV_PAD, FFN_HALF), f32),
        ],
        compiler_params=_params(("arbitrary", "arbitrary", "arbitrary")),
        name="ffn_up",
    )(h2, w_up_p, w_up_p, cw_p, cw_p, cb_p, cb_p, cache_p, cache_p)
    y = pl.pallas_call(
        functools.partial(_ffn_down_kernel, sb=sb, lt=lt),
        grid=(ns // sb, nt),
        in_specs=[
            pl.BlockSpec((sb, lt, D_FF_PAD), lambda s, i: (s, i, 0)),
            pl.BlockSpec((sb, lt, D_MODEL), lambda s, i: (s, i, 0)),
            _const_spec((D_FF_PAD, D_MODEL)),
            _const_spec((1, D_MODEL)),
        ],
        out_specs=pl.BlockSpec((sb, lt, D_MODEL), lambda s, i: (s, i, 0)),
        out_shape=jax.ShapeDtypeStruct((ns, length, D_MODEL), f32),
        compiler_params=_params(("arbitrary", "arbitrary")),
        name="ffn_down",
    )(act, x1, w_down_p, gfin)
    return y, nfv, nfg


def _pad_ff(a, axis):
    pad = [(0, 0)] * a.ndim
    pad[axis] = (0, D_FF_PAD - D_FF)
    if a.shape[axis] == D_FF:
        return jnp.pad(a, pad)
    lo, hi = jnp.split(a, 2, axis=axis)
    return jnp.concatenate([jnp.pad(lo, pad), jnp.pad(hi, pad)], axis=axis)


def _trunk(x, s_re, s_im, sc_buf, ffn_buf, w, cfg):
    ns, length, _ = x.shape
    tm_in, sb_mix, lt_mix, sb_ffn, lt_ffn, cb = cfg
    xf = x.reshape(ns * length, D_MODEL)
    u_a, gin, cv, gates = _in_proj(xf, w["norm1_g"], w["w_in"], tm_in)
    ya, n_re, n_im = _s5_mixer(u_a, s_re, s_im, w["s5"], cb)
    r3 = lambda a: a.reshape(ns, length, a.shape[-1])
    x1, h2, new_sc = _mix_out(ya.reshape(N_SLAB, ns, length, LANES), r3(gin), r3(cv), r3(gates),
                              x, sc_buf, w["w_glu"], w["w_sc_out"], w["w_o"],
                              w["sc_conv_w"], w["sc_conv_b"], w["norm2_g"], sb_mix, lt_mix)
    y, nfv, nfg = _ffn(h2, x1, w["final_norm_g"], w["w_up"], w["w_down"],
                       w["ffn_conv_w"], w["ffn_conv_b"], _pad_ff(ffn_buf, 2), sb_ffn, lt_ffn)
    new_ffn = jnp.concatenate([nfv[:, -1, :, :D_FF], nfg[:, -1, :, :D_FF]], axis=-1)
    return y, n_re[None], n_im[None], new_sc[None], new_ffn[None]


def kernel(x_prompt, x_sample, state_s5_re, state_s5_im, cache_sc_conv, cache_ffn_conv, norm1_g, w_in, lam_re, lam_im, log_dt, b_re, b_im, c_re, c_im, d_skip, w_glu, sc_conv_w, sc_conv_b, w_sc_out, w_o, norm2_g, w_up, ffn_conv_w, ffn_conv_b, w_down, final_norm_g):
    w = {
        "norm1_g": norm1_g[0].reshape(1, D_MODEL),
        "w_in": w_in[0].astype(bf16),
        "s5": _s5_tables(lam_re[0], lam_im[0], log_dt[0], b_re[0], b_im[0],
                         c_re[0], c_im[0], d_skip[0]),
        "w_glu": w_glu[0].astype(bf16),
        "sc_conv_w": sc_conv_w[0],
        "sc_conv_b": sc_conv_b[0].reshape(1, D_B),
        "w_sc_out": w_sc_out[0].astype(bf16),
        "w_o": w_o[0].astype(bf16),
        "norm2_g": norm2_g[0].reshape(1, D_MODEL),
        "w_up": _pad_ff(w_up[0], 1).astype(bf16),
        "ffn_conv_w": _pad_ff(ffn_conv_w[0], 1),
        "ffn_conv_b": _pad_ff(ffn_conv_b[0].reshape(1, 2 * D_FF), 1),
        "w_down": _pad_ff(w_down[0], 0).astype(bf16),
        "final_norm_g": final_norm_g.reshape(1, D_MODEL),
    }
    bp, lp, _ = x_prompt.shape
    bs, ls, _ = x_sample.shape
    zeros = lambda *s: jnp.zeros(s, f32)
    p_cfg = (256, 1, 256, 1, 256, 64)
    yp, p_re, p_im, p_sc, p_ffn = _trunk(
        x_prompt, zeros(bp, G_A, N_STATE), zeros(bp, G_A, N_STATE),
        zeros(bp, 2, D_B), zeros(bp, 2, 2 * D_FF), w, p_cfg)
    s_cfg = (256, bs // 2, ls, bs // 2, ls, 1)
    ys, s_re, s_im, s_sc, s_ffn = _trunk(
        x_sample, state_s5_re[0], state_s5_im[0], cache_sc_conv[0], cache_ffn_conv[0], w, s_cfg)
    return (yp, ys, p_re, p_im, p_sc, p_ffn, s_re, s_im, s_sc, s_ffn)
```

```python
import functools

import jax
import jax.numpy as jnp
import numpy as np
from jax import lax
from jax.experimental import pallas as pl
from jax.experimental.pallas import tpu as pltpu

D_MODEL = 2048
D_A = D_MODEL // 2
S5_GROUP = 16
G_A = D_A // S5_GROUP
N_STATE = 64
D_B = D_MODEL // 2
D_FF = 5504
EPS = 1e-6
IN_COLS = D_A + 3 * D_B + 2 * D_MODEL

LANES = 128
SUBLANES = 8
MXU_DIM = 256
VMEM_LIMIT_BYTES = 58 * 1024 * 1024

N_SLAB = D_A // LANES
SLAB_G = LANES // S5_GROUP
S5_T = 16
S5_TQ = S5_T * S5_GROUP
S5_K = S5_T * LANES
S5_ST = 2 * N_STATE
S5_SL = SLAB_G * S5_ST
D_FF_PAD = 5632
FFN_HALF = D_FF_PAD // 2
CONV_PAD = SUBLANES

bf16 = jnp.bfloat16
f32 = jnp.float32


def _rmsnorm(x, g):
    ms = jnp.mean(x * x, axis=-1, keepdims=True)
    return x * lax.rsqrt(ms + EPS) * g


def _params(sem):
    return pltpu.CompilerParams(dimension_semantics=sem, vmem_limit_bytes=VMEM_LIMIT_BYTES)


def _const_spec(shape):
    nd = len(shape)
    return pl.BlockSpec(shape, lambda *_: (0,) * nd, pipeline_mode=pl.Buffered(1))


def _dot(a, b):
    return jnp.dot(a, b, preferred_element_type=f32)


def _in_proj_kernel(x_ref, g_ref, w_ref, u_ref, gin_ref, cv_ref, gates_ref):
    h = _rmsnorm(x_ref[...], g_ref[...]).astype(bf16)
    u = _dot(h, w_ref[:, :D_A])
    for k in range(N_SLAB):
        u_ref[k] = u[:, k * LANES:(k + 1) * LANES]
    gin_ref[...] = _dot(h, w_ref[:, D_A:D_A + D_B]).astype(bf16)
    c = _dot(h, w_ref[:, D_A + D_B:D_A + 2 * D_B])
    v = _dot(h, w_ref[:, D_A + 2 * D_B:D_A + 3 * D_B])
    cv_ref[...] = (c * v).astype(bf16)
    g0 = D_A + 3 * D_B
    for n in range(2 * D_MODEL // D_B):
        acc = _dot(h, w_ref[:, g0 + n * D_B:g0 + (n + 1) * D_B])
        gates_ref[:, n * D_B:(n + 1) * D_B] = jax.nn.sigmoid(acc).astype(bf16)


def _in_proj(x, norm_g, w_in_b, tm):
    t = x.shape[0]
    return pl.pallas_call(
        _in_proj_kernel,
        grid=(t // tm,),
        in_specs=[
            pl.BlockSpec((tm, D_MODEL), lambda i: (i, 0)),
            _const_spec((1, D_MODEL)),
            _const_spec((D_MODEL, IN_COLS)),
        ],
        out_specs=[
            pl.BlockSpec((N_SLAB, tm, LANES), lambda i: (0, i, 0)),
            pl.BlockSpec((tm, D_B), lambda i: (i, 0)),
            pl.BlockSpec((tm, D_B), lambda i: (i, 0)),
            pl.BlockSpec((tm, 2 * D_MODEL), lambda i: (i, 0)),
        ],
        out_shape=[
            jax.ShapeDtypeStruct((N_SLAB, t, LANES), f32),
            jax.ShapeDtypeStruct((t, D_B), bf16),
            jax.ShapeDtypeStruct((t, D_B), bf16),
            jax.ShapeDtypeStruct((t, 2 * D_MODEL), bf16),
        ],
        compiler_params=_params(("arbitrary",)),
        name="in_proj",
    )(x, norm_g, w_in_b)


def _group_mask(shape, row0, row_shift, lane_shift):
    r = lax.broadcasted_iota(jnp.int32, shape, 0) + row0
    c = lax.broadcasted_iota(jnp.int32, shape, 1)
    return ((r >> row_shift) & (SLAB_G - 1)) == ((c >> lane_shift) & (SLAB_G - 1))


def _expand_tables(kc_ref, vc_ref, wc_refs, r_ref, m_scr, v_scr, w_scrs):
    rep = r_ref[...]
    bd = _dot(kc_ref[...], rep)
    bd = jnp.where(_group_mask(bd.shape, 0, 4, 4), bd, 0.0).astype(bf16)
    for t in range(S5_T):
        rows = slice(t * LANES, (t + 1) * LANES)
        if t:
            m_scr[rows, :t * LANES] = jnp.zeros((LANES, t * LANES), bf16)
        m_scr[rows, t * LANES:] = bd[:, :(S5_T - t) * LANES]
    step = MXU_DIM
    for r0 in range(0, S5_SL, step):
        vb = _dot(vc_ref[r0:r0 + step, :], rep)
        v_scr[r0:r0 + step, :] = jnp.where(_group_mask(vb.shape, r0, 7, 4), vb, 0.0).astype(bf16)
    for wc_ref, w_scr in zip(wc_refs, w_scrs):
        for r0 in range(0, S5_K, step):
            wt = jnp.concatenate([wc_ref[r0:r0 + step, :].astype(f32)] * SLAB_G, axis=1)
            w_scr[r0:r0 + step, :] = jnp.where(_group_mask(wt.shape, r0, 4, 7), wt, 0.0).astype(bf16)


def _s5_outputs(lhs, sprev, xs, m_scr, v_scr, d_ref, store):
    per = MXU_DIM // LANES
    for a in range(S5_T // per):
        k = (a + 1) * MXU_DIM
        cols = slice(a * MXU_DIM, (a + 1) * MXU_DIM)
        y = _dot(lhs[:, :k], m_scr[:k, cols]) + _dot(sprev, v_scr[:, cols])
        for i in range(per):
            t = a * per + i
            piece = y[:, i * LANES:(i + 1) * LANES] + d_ref[...] * xs[t]
            store(t, jax.nn.gelu(piece))


def _s5_prompt_kernel(u_ref, kc_ref, vc_ref, wc_ref, r_ref, pq_ref, d_ref, h0_ref, h0s_ref,
                      y_ref, sfin_ref,
                      m_scr, v_scr, w_scr, s_scr, t_scr, loc_scr, loct_scr, prev_scr, *, nb, cb):
    ci = pl.program_id(1)

    @pl.when(ci == 0)
    def _():
        _expand_tables(kc_ref, vc_ref, [wc_ref], r_ref, m_scr, v_scr, [w_scr])
        s_scr[...] = h0_ref[...]
        t_scr[...] = h0s_ref[...]
        loc_scr[...] = jnp.zeros_like(loc_scr)
        loct_scr[...] = jnp.zeros_like(loct_scr)

    xs = [jnp.concatenate([u_ref[b, pl.ds(t, cb, stride=S5_T), :] for b in range(nb)], axis=0)
          for t in range(S5_T)]
    lhs = jnp.concatenate(xs, axis=1).astype(bf16)

    loc = _dot(lhs, w_scr[...])
    for j in range(SLAB_G):
        lj = loc[:, j * S5_ST:(j + 1) * S5_ST]
        ljt = pltpu.roll(lj, N_STATE, axis=1)
        for b in range(nb):
            rows = slice(b * cb, (b + 1) * cb)
            loc_scr[j, pl.ds(b, cb, stride=SUBLANES), :] = lj[rows]
            loct_scr[j, pl.ds(b, cb, stride=SUBLANES), :] = ljt[rows]

    ss = [s_scr[j] for j in range(SLAB_G)]
    ts = [t_scr[j] for j in range(SLAB_G)]
    for c in range(cb):
        rows = slice(c * SUBLANES, (c + 1) * SUBLANES)
        for j in range(SLAB_G):
            p = pq_ref[0, j:j + 1, :]
            q = pq_ref[1, j:j + 1, :]
            prev_scr[j, rows, :] = ss[j]
            ss[j], ts[j] = (p * ss[j] + q * ts[j] + loc_scr[j, rows, :],
                            p * ts[j] - q * ss[j] + loct_scr[j, rows, :])
    for j in range(SLAB_G):
        s_scr[j] = ss[j]
        t_scr[j] = ts[j]
        sfin_ref[j] = ss[j]

    sprev = jnp.concatenate(
        [jnp.concatenate([prev_scr[j, pl.ds(b, cb, stride=SUBLANES), :] for b in range(nb)], axis=0)
         for j in range(SLAB_G)], axis=1).astype(bf16)

    def store(t, val):
        for b in range(nb):
            y_ref[b, pl.ds(t, cb, stride=S5_T), :] = val[b * cb:(b + 1) * cb]

    _s5_outputs(lhs, sprev, xs, m_scr, v_scr, d_ref, store)


def _s5_prompt(u, tables, h0, h0s, cb):
    kc, vc, wc, _, rep, pq, dvec = tables
    _, nb, length, _ = u.shape
    nc = length // S5_T
    kern = functools.partial(_s5_prompt_kernel, nb=nb, cb=cb)
    tok = pl.BlockSpec((None, nb, cb * S5_T, LANES), lambda k, ci: (k, 0, ci, 0))
    per_slab = lambda *shape: pl.BlockSpec((None,) + shape, lambda k, ci: (k,) + (0,) * len(shape))
    state = per_slab(SLAB_G, SUBLANES, S5_ST)
    return pl.pallas_call(
        kern,
        grid=(N_SLAB, nc // cb),
        in_specs=[tok, per_slab(LANES, S5_TQ), per_slab(S5_SL, S5_TQ), per_slab(S5_K, S5_ST),
                  _const_spec((S5_TQ, S5_K)),
                  per_slab(2, SLAB_G, S5_ST), per_slab(1, LANES), state, state],
        out_specs=[tok, state],
        out_shape=[
            jax.ShapeDtypeStruct(u.shape, f32),
            jax.ShapeDtypeStruct((N_SLAB, SLAB_G, SUBLANES, S5_ST), f32),
        ],
        scratch_shapes=[
            pltpu.VMEM((S5_K, S5_K), bf16),
            pltpu.VMEM((S5_SL, S5_K), bf16),
            pltpu.VMEM((S5_K, S5_SL), bf16),
            pltpu.VMEM((SLAB_G, SUBLANES, S5_ST), f32),
            pltpu.VMEM((SLAB_G, SUBLANES, S5_ST), f32),
            pltpu.VMEM((SLAB_G, cb * SUBLANES, S5_ST), f32),
            pltpu.VMEM((SLAB_G, cb * SUBLANES, S5_ST), f32),
            pltpu.VMEM((SLAB_G, cb * SUBLANES, S5_ST), f32),
        ],
        compiler_params=_params(("arbitrary", "arbitrary")),
        name="s5_prompt",
    )(u, kc, vc, wc, rep, pq, dvec, h0, h0s)


def _s5_sample_kernel(u_ref, kc_ref, vc_ref, wc_ref, wlc_ref, r_ref, pq_ref, d_ref, h0_ref, h0s_ref,
                      y_ref, sfin_ref, m_scr, v_scr, wh_scr, wl_scr, *, nb):
    _expand_tables(kc_ref, vc_ref, [wc_ref, wlc_ref], r_ref, m_scr, v_scr, [wh_scr, wl_scr])
    xs = [u_ref[pl.ds(t, nb, stride=S5_T), :] for t in range(S5_T)]
    lhs_f = jnp.concatenate(xs, axis=1)
    lhs = lhs_f.astype(bf16)
    lhs_lo = (lhs_f - lhs.astype(f32)).astype(bf16)
    wh = wh_scr[...]
    loc = _dot(lhs, wh) + _dot(lhs_lo, wh) + _dot(lhs, wl_scr[...])
    for j in range(SLAB_G):
        p = pq_ref[0, j:j + 1, :]
        q = pq_ref[1, j:j + 1, :]
        sfin_ref[j] = p * h0_ref[j] + q * h0s_ref[j] + loc[:, j * S5_ST:(j + 1) * S5_ST]
    sprev = jnp.concatenate([h0_ref[j] for j in range(SLAB_G)], axis=1).astype(bf16)

    def store(t, val):
        y_ref[pl.ds(t, nb, stride=S5_T), :] = val

    _s5_outputs(lhs, sprev, xs, m_scr, v_scr, d_ref, store)


def _s5_sample(u, tables, h0, h0s, nb):
    kc, vc, wc, wlc, rep, pq, dvec = tables
    kern = functools.partial(_s5_sample_kernel, nb=nb)
    per_slab = lambda *shape: pl.BlockSpec((None,) + shape, lambda k: (k,) + (0,) * len(shape))
    tok = per_slab(nb * S5_T, LANES)
    state = per_slab(SLAB_G, nb, S5_ST)
    return pl.pallas_call(
        kern,
        grid=(N_SLAB,),
        in_specs=[tok, per_slab(LANES, S5_TQ), per_slab(S5_SL, S5_TQ), per_slab(S5_K, S5_ST),
                  per_slab(S5_K, S5_ST), _const_spec((S5_TQ, S5_K)),
                  per_slab(2, SLAB_G, S5_ST), per_slab(1, LANES), state, state],
        out_specs=[tok, state],
        out_shape=[
            jax.ShapeDtypeStruct(u.shape, f32),
            jax.ShapeDtypeStruct((N_SLAB, SLAB_G, nb, S5_ST), f32),
        ],
        scratch_shapes=[
            pltpu.VMEM((S5_K, S5_K), bf16),
            pltpu.VMEM((S5_SL, S5_K), bf16),
            pltpu.VMEM((S5_K, S5_SL), bf16),
            pltpu.VMEM((S5_K, S5_SL), bf16),
        ],
        compiler_params=_params(("arbitrary",)),
        name="s5_sample",
    )(u, kc, vc, wc, wlc, rep, pq, dvec, h0, h0s)


def _replication_matrix():
    src = np.arange(S5_TQ)
    dst = np.arange(S5_K)
    same = ((src[:, None] // S5_GROUP == dst[None, :] // LANES)
            & (src[:, None] % S5_GROUP == dst[None, :] % S5_GROUP))
    return jnp.asarray(same, dtype=bf16)


def _s5_tables(lam_re, lam_im, log_dt, b_re, b_im, c_re, c_im, d_skip):
    hp = lax.Precision.HIGHEST
    dt = jnp.exp(log_dt.astype(f32))[:, None]
    lr, li = lam_re.astype(f32), lam_im.astype(f32)
    mag = jnp.exp(lr * dt)
    ab_re = mag * jnp.cos(li * dt)
    ab_im = mag * jnp.sin(li * dt)
    den = lr * lr + li * li
    nr, ni = ab_re - 1.0, ab_im
    f_re = (nr * lr + ni * li) / den
    f_im = (ni * lr - nr * li) / den
    br, bi = b_re.astype(f32), b_im.astype(f32)
    bb_re = f_re[..., None] * br - f_im[..., None] * bi
    bb_im = f_re[..., None] * bi + f_im[..., None] * br
    cr, ci = c_re.astype(f32), c_im.astype(f32)

    k = jnp.arange(S5_T + 1, dtype=f32)[:, None, None]
    pm = jnp.exp(lr * dt * k)
    pr = pm * jnp.cos(li * dt * k)
    pi = pm * jnp.sin(li * dt * k)

    er = cr[None] * pr[:, :, None, :] - ci[None] * pi[:, :, None, :]
    ei = cr[None] * pi[:, :, None, :] + ci[None] * pr[:, :, None, :]
    kk = (jnp.einsum('tgqn,gnp->tgqp', er[:S5_T], bb_re, precision=hp)
          - jnp.einsum('tgqn,gnp->tgqp', ei[:S5_T], bb_im, precision=hp))
    kc = (kk.reshape(S5_T, N_SLAB, SLAB_G, S5_GROUP, S5_GROUP)
          .transpose(1, 2, 4, 0, 3).reshape(N_SLAB, LANES, S5_TQ).astype(bf16))

    ti = jnp.arange(S5_T)
    prr = pr[S5_T - 1 - ti]
    pir = pi[S5_T - 1 - ti]
    w_re = prr[:, :, :, None] * bb_re[None] - pir[:, :, :, None] * bb_im[None]
    w_im = prr[:, :, :, None] * bb_im[None] + pir[:, :, :, None] * bb_re[None]
    w2 = jnp.concatenate([w_re, w_im], axis=2)
    w = (w2.reshape(S5_T, N_SLAB, SLAB_G, S5_ST, S5_GROUP)
         .transpose(1, 0, 2, 4, 3).reshape(N_SLAB, S5_K, S5_ST))
    wc = w.astype(bf16)
    wlc = (w - wc.astype(f32)).astype(bf16)

    v2 = jnp.concatenate([er[1:], -ei[1:]], axis=3)
    vc = (v2.reshape(S5_T, N_SLAB, SLAB_G, S5_GROUP, S5_ST)
          .transpose(1, 2, 4, 0, 3).reshape(N_SLAB, S5_SL, S5_TQ).astype(bf16))

    ar, ai = pr[S5_T], pi[S5_T]
    p_row = jnp.concatenate([ar, ar], axis=1).reshape(N_SLAB, 1, SLAB_G, S5_ST)
    q_row = jnp.concatenate([-ai, ai], axis=1).reshape(N_SLAB, 1, SLAB_G, S5_ST)
    pq = jnp.concatenate([p_row, q_row], axis=1)
    dvec = d_skip.astype(f32).reshape(N_SLAB, 1, LANES)
    return kc, vc, wc, wlc, _replication_matrix(), pq, dvec


def _s5_mixer(u, s_re, s_im, tables, cb):
    nb = s_re.shape[0]
    length = u.shape[1] // nb

    def to_slab(a):
        return a.reshape(nb, N_SLAB, SLAB_G, S5_ST).transpose(1, 2, 0, 3)

    h0 = to_slab(jnp.concatenate([s_re, s_im], axis=-1))
    h0s = to_slab(jnp.concatenate([s_im, s_re], axis=-1))
    if length == S5_T:
        y, sfin = _s5_sample(u, tables, h0, h0s, nb)
    else:
        pad = [(0, 0), (0, 0), (0, SUBLANES - nb), (0, 0)]
        y, sfin = _s5_prompt(u.reshape(N_SLAB, nb, length, LANES), tables,
                             jnp.pad(h0, pad), jnp.pad(h0s, pad), cb)
        y = y.reshape(u.shape)
        sfin = sfin[:, :, :nb]
    sfin = sfin.transpose(2, 0, 1, 3).reshape(nb, G_A, S5_ST)
    return y, sfin[..., :N_STATE], sfin[..., N_STATE:]


def _conv3(ext_ref, w_ref, b_ref, lt, cols=slice(None)):
    lo = CONV_PAD - 2
    out = ext_ref[:, lo:lo + lt, cols] * w_ref[0:1, cols]
    out = out + ext_ref[:, lo + 1:lo + 1 + lt, cols] * w_ref[1:2, cols]
    out = out + ext_ref[:, lo + 2:lo + 2 + lt, cols] * w_ref[2:3, cols]
    return out + b_ref[:, cols]


PERM_BLOCK = SUBLANES * SUBLANES


def _swap_rows(val, scr):
    rows, width = val.shape
    nsl = width // LANES
    for k in range(nsl):
        scr[k] = val[:, k * LANES:(k + 1) * LANES]
    return jnp.concatenate(
        [jnp.concatenate([scr[k, pl.ds(r0 + m, SUBLANES, stride=SUBLANES), :]
                          for r0 in range(0, rows, PERM_BLOCK) for m in range(SUBLANES)], axis=0)
         for k in range(nsl)], axis=1)


def _mix_kernel(ya_ref, gin_ref, cv_ref, gates_ref, x_ref, cache_ref,
                wglu_ref, wsc_ref, wo_ref, cw_ref, cb_ref, g2_ref,
                x1_ref, h2_ref, nsc_ref, ext_scr, mrg_scr, *perm_scr, sb, lt, nchunk):
    i = pl.program_id(1)
    rows = sb * lt
    lo = CONV_PAD - 2

    @pl.when(i == 0)
    def _():
        ext_scr[:, lo:CONV_PAD, :] = cache_ref[...]

    @pl.when(i > 0)
    def _():
        ext_scr[:, lo:CONV_PAD, :] = ext_scr[:, lo + lt:CONV_PAD + lt, :]

    ext_scr[:, CONV_PAD:, :] = cv_ref[...].astype(f32).reshape(sb, lt, D_B)
    nsc_ref[...] = ext_scr[:, lo + lt:CONV_PAD + lt, :]

    conv = _conv3(ext_scr, cw_ref, cb_ref, lt).reshape(rows, D_B)
    gated = (gin_ref[...].astype(f32) * conv).astype(bf16)
    ya = jnp.concatenate([ya_ref[k].reshape(rows, LANES) for k in range(N_SLAB)],
                         axis=1).astype(bf16)
    wc = D_MODEL // nchunk
    for n in range(nchunk):
        cols = slice(n * wc, (n + 1) * wc)
        gcols = slice(D_MODEL + n * wc, D_MODEL + (n + 1) * wc)
        a = _dot(ya, wglu_ref[:, cols])
        gt = _dot(ya, wglu_ref[:, gcols])
        br_a = a * jax.nn.sigmoid(gt)
        br_b = _dot(gated, wsc_ref[:, cols])
        g_a = gates_ref[:, cols].astype(f32)
        g_b = gates_ref[:, gcols].astype(f32)
        mrg_scr[:, cols] = (g_a * br_a + g_b * br_b).astype(bf16)
    x1 = x_ref[...].reshape(rows, D_MODEL) + _dot(mrg_scr[...], wo_ref[...])
    x1_ref[...] = x1.reshape(sb, lt, D_MODEL)
    h2 = _rmsnorm(x1, g2_ref[...])
    if perm_scr:
        h2 = _swap_rows(h2, perm_scr[0])
    h2_ref[...] = h2.astype(bf16)


def _swapped_rows(sb, lt):
    return sb == 1 and lt % PERM_BLOCK == 0


def _swap_scratch(rows, width):
    return pltpu.VMEM((width // LANES, rows, LANES), f32)


def _mix_out(ya, gin, cv, gates, x, cache, wglu, wsc, wo, cw, cbias, g2, sb, lt):
    ns, length, _ = x.shape
    nt = length // lt
    kern = functools.partial(_mix_kernel, sb=sb, lt=lt, nchunk=4)
    perm = [_swap_scratch(sb * lt, D_MODEL)] if _swapped_rows(sb, lt) else []
    tile = lambda c: pl.BlockSpec((sb, lt, c), lambda s, i: (s, i, 0))
    flat = lambda c: pl.BlockSpec((sb * lt, c), lambda s, i: (s * nt + i, 0))
    return pl.pallas_call(
        kern,
        grid=(ns // sb, nt),
        in_specs=[
            pl.BlockSpec((N_SLAB, sb, lt, LANES), lambda s, i: (0, s, i, 0)),
            flat(D_B), flat(D_B), flat(2 * D_MODEL), tile(D_MODEL),
            pl.BlockSpec((sb, 2, D_B), lambda s, i: (s, 0, 0)),
            _const_spec((D_A, 2 * D_MODEL)),
            _const_spec((D_B, D_MODEL)),
            _const_spec((D_MODEL, D_MODEL)),
            _const_spec((3, D_B)),
            _const_spec((1, D_B)),
            _const_spec((1, D_MODEL)),
        ],
        out_specs=[
            tile(D_MODEL),
            flat(D_MODEL),
            pl.BlockSpec((sb, 2, D_B), lambda s, i: (s, 0, 0)),
        ],
        out_shape=[
            jax.ShapeDtypeStruct((ns, length, D_MODEL), f32),
            jax.ShapeDtypeStruct((ns * length, D_MODEL), bf16),
            jax.ShapeDtypeStruct((ns, 2, D_B), f32),
        ],
        scratch_shapes=[
            pltpu.VMEM((sb, lt + CONV_PAD, D_B), f32),
            pltpu.VMEM((sb * lt, D_MODEL), bf16),
        ] + perm,
        compiler_params=_params(("arbitrary", "arbitrary")),
        name="mix_out",
    )(ya, gin, cv, gates, x, cache, wglu, wsc, wo, cw, cbias, g2)


def _ffn_up_kernel(h2_ref, wv_ref, wg_ref, cwv_ref, cwg_ref, cbv_ref, cbg_ref,
                   cachev_ref, cacheg_ref, act_ref, nfv_ref, nfg_ref, extv_scr, extg_scr,
                   *, sb, lt):
    i = pl.program_id(2)
    rows = sb * lt
    lo = CONV_PAD - 2

    @pl.when(i == 0)
    def _():
        extv_scr[:, lo:CONV_PAD, :] = cachev_ref[...]
        extg_scr[:, lo:CONV_PAD, :] = cacheg_ref[...]

    @pl.when(i > 0)
    def _():
        extv_scr[:, lo:CONV_PAD, :] = extv_scr[:, lo + lt:CONV_PAD + lt, :]
        extg_scr[:, lo:CONV_PAD, :] = extg_scr[:, lo + lt:CONV_PAD + lt, :]

    h2 = h2_ref[...]
    extv_scr[:, CONV_PAD:, :] = _dot(h2, wv_ref[...]).reshape(sb, lt, FFN_HALF)
    extg_scr[:, CONV_PAD:, :] = _dot(h2, wg_ref[...]).reshape(sb, lt, FFN_HALF)
    for c in range(FFN_HALF // MXU_DIM):
        cols = slice(c * MXU_DIM, (c + 1) * MXU_DIM)
        val = _conv3(extv_scr, cwv_ref, cbv_ref, lt, cols)
        gt = _conv3(extg_scr, cwg_ref, cbg_ref, lt, cols)
        act_ref[:, cols] = (jax.nn.silu(gt) * val).reshape(rows, MXU_DIM).astype(bf16)
    nfv_ref[...] = extv_scr[:, lo + lt:CONV_PAD + lt, :]
    nfg_ref[...] = extg_scr[:, lo + lt:CONV_PAD + lt, :]


def _conv3_swapped(x, car_ref, w_ref, b_ref, cols):
    x6, x7 = x[:, SUBLANES - 2], x[:, SUBLANES - 1]
    prev6 = jnp.concatenate([car_ref[0, :, cols][None], x6[:-1]], axis=0)
    prev7 = jnp.concatenate([car_ref[1, :, cols][None], x7[:-1]], axis=0)
    car_ref[0, :, cols] = x6[-1]
    car_ref[1, :, cols] = x7[-1]
    first = lax.broadcasted_iota(jnp.int32, x6.shape, 1) == 0
    back2 = jnp.where(first, pltpu.roll(prev6, 1, axis=1), pltpu.roll(x6, 1, axis=1))
    back1 = jnp.where(first, pltpu.roll(prev7, 1, axis=1), pltpu.roll(x7, 1, axis=1))
    s1 = jnp.concatenate([back1[:, None], x[:, :SUBLANES - 1]], axis=1)
    s2 = jnp.concatenate([back2[:, None], back1[:, None], x[:, :SUBLANES - 2]], axis=1)
    return (s2 * w_ref[0:1, cols] + s1 * w_ref[1:2, cols] + x * w_ref[2:3, cols]
            + b_ref[:, cols])


def _ffn_up_swapped_kernel(h2_ref, wv_ref, wg_ref, cwv_ref, cwg_ref, cbv_ref, cbg_ref,
                           cachev_ref, cacheg_ref, act_ref, nfv_ref, nfg_ref,
                           carv_scr, carg_scr, *, lt):
    i = pl.program_id(2)
    nblk = lt // PERM_BLOCK

    @pl.when(i == 0)
    def _():
        for car, cache in ((carv_scr, cachev_ref), (carg_scr, cacheg_ref)):
            for r in range(2):
                car[r] = jnp.broadcast_to(cache[0, r:r + 1, :], (SUBLANES, FFN_HALF))

    h2 = h2_ref[...]
    upv = _dot(h2, wv_ref[...])
    upg = _dot(h2, wg_ref[...])
    for c in range(FFN_HALF // MXU_DIM):
        cols = slice(c * MXU_DIM, (c + 1) * MXU_DIM)
        shape = (nblk, SUBLANES, SUBLANES, MXU_DIM)
        val = _conv3_swapped(upv[:, cols].reshape(shape), carv_scr, cwv_ref, cbv_ref, cols)
        gt = _conv3_swapped(upg[:, cols].reshape(shape), carg_scr, cwg_ref, cbg_ref, cols)
        act_ref[:, cols] = (jax.nn.silu(gt) * val).reshape(lt, MXU_DIM).astype(bf16)
    for car, nf_ref in ((carv_scr, nfv_ref), (carg_scr, nfg_ref)):
        for r in range(2):
            nf_ref[0, r:r + 1, :] = car[r, SUBLANES - 1:SUBLANES, :]


def _ffn_down_kernel(act_ref, x1_ref, wd_ref, gf_ref, y_ref, *swap_scr, sb, lt):
    rows = sb * lt
    delta = _dot(act_ref[...], wd_ref[...])
    if swap_scr:
        delta = _swap_rows(delta, swap_scr[0])
    x2 = x1_ref[...].reshape(rows, D_MODEL) + delta
    y_ref[...] = _rmsnorm(x2, gf_ref[...]).reshape(sb, lt, D_MODEL)


def _ffn(h2, x1, gfin, w_up_p, w_down_p, cw_p, cb_p, cache_p, sb, lt):
    ns, length, _ = x1.shape
    nh = D_FF_PAD // FFN_HALF
    nt = length // lt
    resident = dict(pipeline_mode=pl.Buffered(1))
    val = lambda rows, **kw: pl.BlockSpec((rows, FFN_HALF), lambda h, s, i: (0, h), **kw)
    gate = lambda rows, **kw: pl.BlockSpec((rows, FFN_HALF), lambda h, s, i: (0, nh + h), **kw)
    swapped = _swapped_rows(sb, lt)
    if swapped:
        up_kernel = functools.partial(_ffn_up_swapped_kernel, lt=lt)
        up_scratch = [pltpu.VMEM((2, SUBLANES, FFN_HALF), f32)] * 2
        down_scratch = [_swap_scratch(sb * lt, D_MODEL)]
    else:
        up_kernel = functools.partial(_ffn_up_kernel, sb=sb, lt=lt)
        up_scratch = [pltpu.VMEM((sb, lt + CONV_PAD, FFN_HALF), f32)] * 2
        down_scratch = []
    act, nfv, nfg = pl.pallas_call(
        up_kernel,
        grid=(nh, ns // sb, nt),
        in_specs=[
            pl.BlockSpec((sb * lt, D_MODEL), lambda h, s, i: (s * nt + i, 0)),
            val(D_MODEL, **resident), gate(D_MODEL, **resident),
            val(3), gate(3), val(1), gate(1),
            pl.BlockSpec((sb, 2, FFN_HALF), lambda h, s, i: (s, 0, h)),
            pl.BlockSpec((sb, 2, FFN_HALF), lambda h, s, i: (s, 0, nh + h)),
        ],
        out_specs=[
            pl.BlockSpec((sb * lt, FFN_HALF), lambda h, s, i: (s * nt + i, h)),
            pl.BlockSpec((sb, None, 2, FFN_HALF), lambda h, s, i: (s, i, 0, h)),
            pl.BlockSpec((sb, None, 2, FFN_HALF), lambda h, s, i: (s, i, 0, h)),
        ],
        out_shape=[
            jax.ShapeDtypeStruct((ns * length, D_FF_PAD), bf16),
            jax.ShapeDtypeStruct((ns, nt, 2, D_FF_PAD), f32),
            jax.ShapeDtypeStruct((ns, nt, 2, D_FF_PAD), f32),
        ],
        scratch_shapes=up_scratch,
        compiler_params=_params(("arbitrary", "arbitrary", "arbitrary")),
        name="ffn_up",
    )(h2, w_up_p, w_up_p, cw_p, cw_p, cb_p, cb_p, cache_p, cache_p)
    y = pl.pallas_call(
        functools.partial(_ffn_down_kernel, sb=sb, lt=lt),
        grid=(ns // sb, nt),
        in_specs=[
            pl.BlockSpec((sb * lt, D_FF_PAD), lambda s, i: (s * nt + i, 0)),
            pl.BlockSpec((sb, lt, D_MODEL), lambda s, i: (s, i, 0)),
            _const_spec((D_FF_PAD, D_MODEL)),
            _const_spec((1, D_MODEL)),
        ],
        out_specs=pl.BlockSpec((sb, lt, D_MODEL), lambda s, i: (s, i, 0)),
        out_shape=jax.ShapeDtypeStruct((ns, length, D_MODEL), f32),
        scratch_shapes=down_scratch,
        compiler_params=_params(("arbitrary", "arbitrary")),
        name="ffn_down",
    )(act, x1, w_down_p, gfin)
    return y, nfv, nfg


def _pad_ff(a, axis):
    pad = [(0, 0)] * a.ndim
    pad[axis] = (0, D_FF_PAD - D_FF)
    if a.shape[axis] == D_FF:
        return jnp.pad(a, pad)
    lo, hi = jnp.split(a, 2, axis=axis)
    return jnp.concatenate([jnp.pad(lo, pad), jnp.pad(hi, pad)], axis=axis)


def _trunk(x, s_re, s_im, sc_buf, ffn_buf, w, cfg):
    ns, length, _ = x.shape
    tm_in, sb_mix, lt_mix, sb_ffn, lt_ffn, cb = cfg
    xf = x.reshape(ns * length, D_MODEL)
    u_a, gin, cv, gates = _in_proj(xf, w["norm1_g"], w["w_in"], tm_in)
    ya, n_re, n_im = _s5_mixer(u_a, s_re, s_im, w["s5"], cb)
    x1, h2, new_sc = _mix_out(ya.reshape(N_SLAB, ns, length, LANES), gin, cv, gates,
                              x, sc_buf, w["w_glu"], w["w_sc_out"], w["w_o"],
                              w["sc_conv_w"], w["sc_conv_b"], w["norm2_g"], sb_mix, lt_mix)
    y, nfv, nfg = _ffn(h2, x1, w["final_norm_g"], w["w_up"], w["w_down"],
                       w["ffn_conv_w"], w["ffn_conv_b"], _pad_ff(ffn_buf, 2), sb_ffn, lt_ffn)
    new_ffn = jnp.concatenate([nfv[:, -1, :, :D_FF], nfg[:, -1, :, :D_FF]], axis=-1)
    return y, n_re[None], n_im[None], new_sc[None], new_ffn[None]


def kernel(x_prompt, x_sample, state_s5_re, state_s5_im, cache_sc_conv, cache_ffn_conv, norm1_g, w_in, lam_re, lam_im, log_dt, b_re, b_im, c_re, c_im, d_skip, w_glu, sc_conv_w, sc_conv_b, w_sc_out, w_o, norm2_g, w_up, ffn_conv_w, ffn_conv_b, w_down, final_norm_g):
    w = {
        "norm1_g": norm1_g[0].reshape(1, D_MODEL),
        "w_in": w_in[0].astype(bf16),
        "s5": _s5_tables(lam_re[0], lam_im[0], log_dt[0], b_re[0], b_im[0],
                         c_re[0], c_im[0], d_skip[0]),
        "w_glu": w_glu[0].astype(bf16),
        "sc_conv_w": sc_conv_w[0],
        "sc_conv_b": sc_conv_b[0].reshape(1, D_B),
        "w_sc_out": w_sc_out[0].astype(bf16),
        "w_o": w_o[0].astype(bf16),
        "norm2_g": norm2_g[0].reshape(1, D_MODEL),
        "w_up": _pad_ff(w_up[0], 1).astype(bf16),
        "ffn_conv_w": _pad_ff(ffn_conv_w[0], 1),
        "ffn_conv_b": _pad_ff(ffn_conv_b[0].reshape(1, 2 * D_FF), 1),
        "w_down": _pad_ff(w_down[0], 0).astype(bf16),
        "final_norm_g": final_norm_g.reshape(1, D_MODEL),
    }
    bp, lp, _ = x_prompt.shape
    bs, ls, _ = x_sample.shape
    zeros = lambda *s: jnp.zeros(s, f32)
    p_cfg = (256, 1, 256, 1, 256, 64)
    yp, p_re, p_im, p_sc, p_ffn = _trunk(
        x_prompt, zeros(bp, G_A, N_STATE), zeros(bp, G_A, N_STATE),
        zeros(bp, 2, D_B), zeros(bp, 2, 2 * D_FF), w, p_cfg)
    s_cfg = (256, bs // 2, ls, bs // 2, ls, 1)
    ys, s_re, s_im, s_sc, s_ffn = _trunk(
        x_sample, state_s5_re[0], state_s5_im[0], cache_sc_conv[0], cache_ffn_conv[0], w, s_cfg)
    return (yp, ys, p_re, p_im, p_sc, p_ffn, s_re, s_im, s_sc, s_ffn)
```

```python
import functools

import jax
import jax.numpy as jnp
import numpy as np
from jax import lax
from jax.experimental import pallas as pl
from jax.experimental.pallas import tpu as pltpu

D_MODEL = 2048
D_A = D_MODEL // 2
S5_GROUP = 16
G_A = D_A // S5_GROUP
N_STATE = 64
D_B = D_MODEL // 2
D_FF = 5504
EPS = 1e-6
IN_COLS = D_A + 3 * D_B + 2 * D_MODEL

LANES = 128
SUBLANES = 8
MXU_DIM = 256
VMEM_LIMIT_BYTES = 58 * 1024 * 1024

N_SLAB = D_A // LANES
SLAB_G = LANES // S5_GROUP
S5_T = 16
S5_TQ = S5_T * S5_GROUP
S5_K = S5_T * LANES
S5_ST = 2 * N_STATE
S5_SL = SLAB_G * S5_ST
D_FF_PAD = 5632
FFN_HALF = D_FF_PAD // 2
CONV_PAD = SUBLANES

bf16 = jnp.bfloat16
f32 = jnp.float32


def _rmsnorm(x, g):
    ms = jnp.mean(x * x, axis=-1, keepdims=True)
    return x * lax.rsqrt(ms + EPS) * g


def _params(sem):
    return pltpu.CompilerParams(dimension_semantics=sem, vmem_limit_bytes=VMEM_LIMIT_BYTES)


def _const_spec(shape):
    nd = len(shape)
    return pl.BlockSpec(shape, lambda *_: (0,) * nd, pipeline_mode=pl.Buffered(1))


def _dot(a, b):
    return jnp.dot(a, b, preferred_element_type=f32)


def _in_proj_kernel(x_ref, g_ref, w_ref, u_ref, gin_ref, cv_ref, gates_ref):
    h = _rmsnorm(x_ref[...], g_ref[...]).astype(bf16)
    u = _dot(h, w_ref[:, :D_A])
    for k in range(N_SLAB):
        u_ref[k] = u[:, k * LANES:(k + 1) * LANES]
    gin_ref[...] = _dot(h, w_ref[:, D_A:D_A + D_B]).astype(bf16)
    c = _dot(h, w_ref[:, D_A + D_B:D_A + 2 * D_B])
    v = _dot(h, w_ref[:, D_A + 2 * D_B:D_A + 3 * D_B])
    cv_ref[...] = (c * v).astype(bf16)
    g0 = D_A + 3 * D_B
    for n in range(2 * D_MODEL // D_B):
        acc = _dot(h, w_ref[:, g0 + n * D_B:g0 + (n + 1) * D_B])
        gates_ref[:, n * D_B:(n + 1) * D_B] = jax.nn.sigmoid(acc).astype(bf16)


def _in_proj(x, norm_g, w_in_b, tm):
    t = x.shape[0]
    return pl.pallas_call(
        _in_proj_kernel,
        grid=(t // tm,),
        in_specs=[
            pl.BlockSpec((tm, D_MODEL), lambda i: (i, 0)),
            _const_spec((1, D_MODEL)),
            _const_spec((D_MODEL, IN_COLS)),
        ],
        out_specs=[
            pl.BlockSpec((N_SLAB, tm, LANES), lambda i: (0, i, 0)),
            pl.BlockSpec((tm, D_B), lambda i: (i, 0)),
            pl.BlockSpec((tm, D_B), lambda i: (i, 0)),
            pl.BlockSpec((tm, 2 * D_MODEL), lambda i: (i, 0)),
        ],
        out_shape=[
            jax.ShapeDtypeStruct((N_SLAB, t, LANES), f32),
            jax.ShapeDtypeStruct((t, D_B), bf16),
            jax.ShapeDtypeStruct((t, D_B), bf16),
            jax.ShapeDtypeStruct((t, 2 * D_MODEL), bf16),
        ],
        compiler_params=_params(("arbitrary",)),
        name="in_proj",
    )(x, norm_g, w_in_b)


def _group_mask(shape, row0, row_shift, lane_shift):
    r = lax.broadcasted_iota(jnp.int32, shape, 0) + row0
    c = lax.broadcasted_iota(jnp.int32, shape, 1)
    return ((r >> row_shift) & (SLAB_G - 1)) == ((c >> lane_shift) & (SLAB_G - 1))


def _expand_tables(kc_ref, vc_ref, wc_refs, r_ref, m_scr, v_scr, w_scrs):
    rep = r_ref[...]
    bd = _dot(kc_ref[...], rep)
    bd = jnp.where(_group_mask(bd.shape, 0, 4, 4), bd, 0.0).astype(bf16)
    for t in range(S5_T):
        rows = slice(t * LANES, (t + 1) * LANES)
        if t:
            m_scr[rows, :t * LANES] = jnp.zeros((LANES, t * LANES), bf16)
        m_scr[rows, t * LANES:] = bd[:, :(S5_T - t) * LANES]
    step = MXU_DIM
    for r0 in range(0, S5_SL, step):
        vb = _dot(vc_ref[r0:r0 + step, :], rep)
        v_scr[r0:r0 + step, :] = jnp.where(_group_mask(vb.shape, r0, 7, 4), vb, 0.0).astype(bf16)
    for wc_ref, w_scr in zip(wc_refs, w_scrs):
        for r0 in range(0, S5_K, step):
            wt = jnp.concatenate([wc_ref[r0:r0 + step, :].astype(f32)] * SLAB_G, axis=1)
            w_scr[r0:r0 + step, :] = jnp.where(_group_mask(wt.shape, r0, 4, 7), wt, 0.0).astype(bf16)


def _s5_toeplitz(lhs, m_scr):
    return [_dot(lhs[:, :(a + 1) * MXU_DIM],
                 m_scr[:(a + 1) * MXU_DIM, a * MXU_DIM:(a + 1) * MXU_DIM])
            for a in range(S5_K // MXU_DIM)]


def _s5_outputs(toep, sprev, xs, v_scr, d_ref, store):
    per = MXU_DIM // LANES
    for a, part in enumerate(toep):
        y = part + _dot(sprev, v_scr[:, a * MXU_DIM:(a + 1) * MXU_DIM])
        for i in range(per):
            t = a * per + i
            piece = y[:, i * LANES:(i + 1) * LANES] + d_ref[...] * xs[t]
            store(t, jax.nn.gelu(piece))


def _s5_prompt_kernel(u_ref, kc_ref, vc_ref, wc_ref, r_ref, pq_ref, d_ref, h0_ref, h0s_ref,
                      y_ref, sfin_ref,
                      m_scr, v_scr, w_scr, s_scr, t_scr, loc_scr, loct_scr, prev_scr, *, nb, cb):
    ci = pl.program_id(1)

    @pl.when(ci == 0)
    def _():
        _expand_tables(kc_ref, vc_ref, [wc_ref], r_ref, m_scr, v_scr, [w_scr])
        s_scr[...] = h0_ref[...]
        t_scr[...] = h0s_ref[...]
        loc_scr[...] = jnp.zeros_like(loc_scr)
        loct_scr[...] = jnp.zeros_like(loct_scr)

    xs = [jnp.concatenate([u_ref[b, pl.ds(t, cb, stride=S5_T), :] for b in range(nb)], axis=0)
          for t in range(S5_T)]
    lhs = jnp.concatenate(xs, axis=1).astype(bf16)

    loc = _dot(lhs, w_scr[...])
    toep = _s5_toeplitz(lhs, m_scr)
    for j in range(SLAB_G):
        lj = loc[:, j * S5_ST:(j + 1) * S5_ST]
        ljt = pltpu.roll(lj, N_STATE, axis=1)
        for b in range(nb):
            rows = slice(b * cb, (b + 1) * cb)
            loc_scr[j, pl.ds(b, cb, stride=SUBLANES), :] = lj[rows]
            loct_scr[j, pl.ds(b, cb, stride=SUBLANES), :] = ljt[rows]

    ss = [s_scr[j] for j in range(SLAB_G)]
    ts = [t_scr[j] for j in range(SLAB_G)]
    for c in range(cb):
        rows = slice(c * SUBLANES, (c + 1) * SUBLANES)
        for j in range(SLAB_G):
            p = pq_ref[0, j:j + 1, :]
            q = pq_ref[1, j:j + 1, :]
            prev_scr[j, rows, :] = ss[j]
            ss[j], ts[j] = (p * ss[j] + q * ts[j] + loc_scr[j, rows, :],
                            p * ts[j] - q * ss[j] + loct_scr[j, rows, :])
    for j in range(SLAB_G):
        s_scr[j] = ss[j]
        t_scr[j] = ts[j]
        sfin_ref[j] = ss[j]

    sprev = jnp.concatenate(
        [jnp.concatenate([prev_scr[j, pl.ds(b, cb, stride=SUBLANES), :] for b in range(nb)], axis=0)
         for j in range(SLAB_G)], axis=1).astype(bf16)

    def store(t, val):
        for b in range(nb):
            y_ref[b, pl.ds(t, cb, stride=S5_T), :] = val[b * cb:(b + 1) * cb]

    _s5_outputs(toep, sprev, xs, v_scr, d_ref, store)


def _s5_prompt(u, tables, h0, h0s, cb):
    kc, vc, wc, _, rep, pq, dvec = tables
    _, nb, length, _ = u.shape
    nc = length // S5_T
    kern = functools.partial(_s5_prompt_kernel, nb=nb, cb=cb)
    tok = pl.BlockSpec((None, nb, cb * S5_T, LANES), lambda k, ci: (k, 0, ci, 0))
    per_slab = lambda *shape: pl.BlockSpec((None,) + shape, lambda k, ci: (k,) + (0,) * len(shape))
    state = per_slab(SLAB_G, SUBLANES, S5_ST)
    return pl.pallas_call(
        kern,
        grid=(N_SLAB, nc // cb),
        in_specs=[tok, per_slab(LANES, S5_TQ), per_slab(S5_SL, S5_TQ), per_slab(S5_K, S5_ST),
                  _const_spec((S5_TQ, S5_K)),
                  per_slab(2, SLAB_G, S5_ST), per_slab(1, LANES), state, state],
        out_specs=[tok, state],
        out_shape=[
            jax.ShapeDtypeStruct(u.shape, f32),
            jax.ShapeDtypeStruct((N_SLAB, SLAB_G, SUBLANES, S5_ST), f32),
        ],
        scratch_shapes=[
            pltpu.VMEM((S5_K, S5_K), bf16),
            pltpu.VMEM((S5_SL, S5_K), bf16),
            pltpu.VMEM((S5_K, S5_SL), bf16),
            pltpu.VMEM((SLAB_G, SUBLANES, S5_ST), f32),
            pltpu.VMEM((SLAB_G, SUBLANES, S5_ST), f32),
            pltpu.VMEM((SLAB_G, cb * SUBLANES, S5_ST), f32),
            pltpu.VMEM((SLAB_G, cb * SUBLANES, S5_ST), f32),
            pltpu.VMEM((SLAB_G, cb * SUBLANES, S5_ST), f32),
        ],
        compiler_params=_params(("arbitrary", "arbitrary")),
        name="s5_prompt",
    )(u, kc, vc, wc, rep, pq, dvec, h0, h0s)


def _s5_sample_kernel(u_ref, kc_ref, vc_ref, wc_ref, wlc_ref, r_ref, pq_ref, d_ref, h0_ref, h0s_ref,
                      y_ref, sfin_ref, m_scr, v_scr, wh_scr, wl_scr, *, nb):
    _expand_tables(kc_ref, vc_ref, [wc_ref, wlc_ref], r_ref, m_scr, v_scr, [wh_scr, wl_scr])
    xs = [u_ref[pl.ds(t, nb, stride=S5_T), :] for t in range(S5_T)]
    lhs_f = jnp.concatenate(xs, axis=1)
    lhs = lhs_f.astype(bf16)
    lhs_lo = (lhs_f - lhs.astype(f32)).astype(bf16)
    wh = wh_scr[...]
    loc = _dot(lhs, wh) + _dot(lhs_lo, wh) + _dot(lhs, wl_scr[...])
    for j in range(SLAB_G):
        p = pq_ref[0, j:j + 1, :]
        q = pq_ref[1, j:j + 1, :]
        sfin_ref[j] = p * h0_ref[j] + q * h0s_ref[j] + loc[:, j * S5_ST:(j + 1) * S5_ST]
    sprev = jnp.concatenate([h0_ref[j] for j in range(SLAB_G)], axis=1).astype(bf16)

    def store(t, val):
        y_ref[pl.ds(t, nb, stride=S5_T), :] = val

    _s5_outputs(_s5_toeplitz(lhs, m_scr), sprev, xs, v_scr, d_ref, store)


def _s5_sample(u, tables, h0, h0s, nb):
    kc, vc, wc, wlc, rep, pq, dvec = tables
    kern = functools.partial(_s5_sample_kernel, nb=nb)
    per_slab = lambda *shape: pl.BlockSpec((None,) + shape, lambda k: (k,) + (0,) * len(shape))
    tok = per_slab(nb * S5_T, LANES)
    state = per_slab(SLAB_G, nb, S5_ST)
    return pl.pallas_call(
        kern,
        grid=(N_SLAB,),
        in_specs=[tok, per_slab(LANES, S5_TQ), per_slab(S5_SL, S5_TQ), per_slab(S5_K, S5_ST),
                  per_slab(S5_K, S5_ST), _const_spec((S5_TQ, S5_K)),
                  per_slab(2, SLAB_G, S5_ST), per_slab(1, LANES), state, state],
        out_specs=[tok, state],
        out_shape=[
            jax.ShapeDtypeStruct(u.shape, f32),
            jax.ShapeDtypeStruct((N_SLAB, SLAB_G, nb, S5_ST), f32),
        ],
        scratch_shapes=[
            pltpu.VMEM((S5_K, S5_K), bf16),
            pltpu.VMEM((S5_SL, S5_K), bf16),
            pltpu.VMEM((S5_K, S5_SL), bf16),
            pltpu.VMEM((S5_K, S5_SL), bf16),
        ],
        compiler_params=_params(("arbitrary",)),
        name="s5_sample",
    )(u, kc, vc, wc, wlc, rep, pq, dvec, h0, h0s)


def _replication_matrix():
    src = np.arange(S5_TQ)
    dst = np.arange(S5_K)
    same = ((src[:, None] // S5_GROUP == dst[None, :] // LANES)
            & (src[:, None] % S5_GROUP == dst[None, :] % S5_GROUP))
    return jnp.asarray(same, dtype=bf16)


def _s5_tables(lam_re, lam_im, log_dt, b_re, b_im, c_re, c_im, d_skip):
    hp = lax.Precision.HIGHEST
    dt = jnp.exp(log_dt.astype(f32))[:, None]
    lr, li = lam_re.astype(f32), lam_im.astype(f32)
    mag = jnp.exp(lr * dt)
    ab_re = mag * jnp.cos(li * dt)
    ab_im = mag * jnp.sin(li * dt)
    den = lr * lr + li * li
    nr, ni = ab_re - 1.0, ab_im
    f_re = (nr * lr + ni * li) / den
    f_im = (ni * lr - nr * li) / den
    br, bi = b_re.astype(f32), b_im.astype(f32)
    bb_re = f_re[..., None] * br - f_im[..., None] * bi
    bb_im = f_re[..., None] * bi + f_im[..., None] * br
    cr, ci = c_re.astype(f32), c_im.astype(f32)

    k = jnp.arange(S5_T + 1, dtype=f32)[:, None, None]
    pm = jnp.exp(lr * dt * k)
    pr = pm * jnp.cos(li * dt * k)
    pi = pm * jnp.sin(li * dt * k)

    er = cr[None] * pr[:S5_T, :, None, :] - ci[None] * pi[:S5_T, :, None, :]
    ei = cr[None] * pi[:S5_T, :, None, :] + ci[None] * pr[:S5_T, :, None, :]
    kk = (jnp.einsum('tgqn,gnp->gptq', er, bb_re, precision=hp)
          - jnp.einsum('tgqn,gnp->gptq', ei, bb_im, precision=hp))
    kc = kk.reshape(N_SLAB, LANES, S5_TQ).astype(bf16)

    def slab_time(a):
        return a.reshape(S5_T, N_SLAB, SLAB_G, 1, N_STATE).transpose(1, 0, 2, 3, 4)

    back = S5_T - 1 - jnp.arange(S5_T)
    prr, pir = slab_time(pr[back]), slab_time(pi[back])
    bt_re = bb_re.transpose(0, 2, 1).reshape(N_SLAB, 1, SLAB_G, S5_GROUP, N_STATE)
    bt_im = bb_im.transpose(0, 2, 1).reshape(N_SLAB, 1, SLAB_G, S5_GROUP, N_STATE)
    w = jnp.concatenate([prr * bt_re - pir * bt_im, prr * bt_im + pir * bt_re], axis=-1)
    w = w.reshape(N_SLAB, S5_K, S5_ST)
    wc = w.astype(bf16)
    wlc = (w - wc.astype(f32)).astype(bf16)

    ct_re = cr.transpose(0, 2, 1)[:, :, None, :]
    ct_im = ci.transpose(0, 2, 1)[:, :, None, :]
    pt_re = pr[1:].transpose(1, 2, 0)[:, :, :, None]
    pt_im = pi[1:].transpose(1, 2, 0)[:, :, :, None]
    vc = jnp.concatenate([ct_re * pt_re - ct_im * pt_im, -(ct_re * pt_im + ct_im * pt_re)], axis=1)
    vc = vc.reshape(N_SLAB, S5_SL, S5_TQ).astype(bf16)

    ar, ai = pr[S5_T], pi[S5_T]
    p_row = jnp.concatenate([ar, ar], axis=1).reshape(N_SLAB, 1, SLAB_G, S5_ST)
    q_row = jnp.concatenate([-ai, ai], axis=1).reshape(N_SLAB, 1, SLAB_G, S5_ST)
    pq = jnp.concatenate([p_row, q_row], axis=1)
    dvec = d_skip.astype(f32).reshape(N_SLAB, 1, LANES)
    return kc, vc, wc, wlc, _replication_matrix(), pq, dvec


def _s5_mixer(u, s_re, s_im, tables, cb):
    nb = s_re.shape[0]
    length = u.shape[1] // nb

    def to_slab(a):
        return a.reshape(nb, N_SLAB, SLAB_G, S5_ST).transpose(1, 2, 0, 3)

    h0 = to_slab(jnp.concatenate([s_re, s_im], axis=-1))
    h0s = to_slab(jnp.concatenate([s_im, s_re], axis=-1))
    if length == S5_T:
        y, sfin = _s5_sample(u, tables, h0, h0s, nb)
    else:
        pad = [(0, 0), (0, 0), (0, SUBLANES - nb), (0, 0)]
        y, sfin = _s5_prompt(u.reshape(N_SLAB, nb, length, LANES), tables,
                             jnp.pad(h0, pad), jnp.pad(h0s, pad), cb)
        y = y.reshape(u.shape)
        sfin = sfin[:, :, :nb]
    sfin = sfin.transpose(2, 0, 1, 3).reshape(nb, G_A, S5_ST)
    return y, sfin[..., :N_STATE], sfin[..., N_STATE:]


def _conv3(ext_ref, w_ref, b_ref, lt, cols=slice(None)):
    lo = CONV_PAD - 2
    out = ext_ref[:, lo:lo + lt, cols] * w_ref[0:1, cols]
    out = out + ext_ref[:, lo + 1:lo + 1 + lt, cols] * w_ref[1:2, cols]
    out = out + ext_ref[:, lo + 2:lo + 2 + lt, cols] * w_ref[2:3, cols]
    return out + b_ref[:, cols]


PERM_BLOCK = SUBLANES * SUBLANES


def _swap_rows(val, scr):
    rows, width = val.shape
    nsl = width // LANES
    for k in range(nsl):
        scr[k] = val[:, k * LANES:(k + 1) * LANES]
    return jnp.concatenate(
        [jnp.concatenate([scr[k, pl.ds(r0 + m, SUBLANES, stride=SUBLANES), :]
                          for r0 in range(0, rows, PERM_BLOCK) for m in range(SUBLANES)], axis=0)
         for k in range(nsl)], axis=1)


def _mix_kernel(ya_ref, gin_ref, cv_ref, gates_ref, x_ref, cache_ref,
                wglu_ref, wsc_ref, wo_ref, cw_ref, cb_ref, g2_ref,
                x1_ref, h2_ref, nsc_ref, ext_scr, mrg_scr, *perm_scr, sb, lt, nchunk):
    i = pl.program_id(1)
    rows = sb * lt
    lo = CONV_PAD - 2

    @pl.when(i == 0)
    def _():
        ext_scr[:, lo:CONV_PAD, :] = cache_ref[...]

    @pl.when(i > 0)
    def _():
        ext_scr[:, lo:CONV_PAD, :] = ext_scr[:, lo + lt:CONV_PAD + lt, :]

    ext_scr[:, CONV_PAD:, :] = cv_ref[...].astype(f32).reshape(sb, lt, D_B)
    nsc_ref[...] = ext_scr[:, lo + lt:CONV_PAD + lt, :]

    conv = _conv3(ext_scr, cw_ref, cb_ref, lt).reshape(rows, D_B)
    gated = (gin_ref[...].astype(f32) * conv).astype(bf16)
    ya = jnp.concatenate([ya_ref[k].reshape(rows, LANES) for k in range(N_SLAB)],
                         axis=1).astype(bf16)
    wc = D_MODEL // nchunk
    for n in range(nchunk):
        cols = slice(n * wc, (n + 1) * wc)
        gcols = slice(D_MODEL + n * wc, D_MODEL + (n + 1) * wc)
        a = _dot(ya, wglu_ref[:, cols])
        gt = _dot(ya, wglu_ref[:, gcols])
        br_a = a * jax.nn.sigmoid(gt)
        br_b = _dot(gated, wsc_ref[:, cols])
        g_a = gates_ref[:, cols].astype(f32)
        g_b = gates_ref[:, gcols].astype(f32)
        mrg_scr[:, cols] = (g_a * br_a + g_b * br_b).astype(bf16)
    x1 = x_ref[...].reshape(rows, D_MODEL) + _dot(mrg_scr[...], wo_ref[...])
    x1_ref[...] = x1.reshape(sb, lt, D_MODEL)
    h2 = _rmsnorm(x1, g2_ref[...])
    if perm_scr:
        h2 = _swap_rows(h2, perm_scr[0])
    h2_ref[...] = h2.astype(bf16)


def _swapped_rows(sb, lt):
    return sb == 1 and lt % PERM_BLOCK == 0


def _swap_scratch(rows, width):
    return pltpu.VMEM((width // LANES, rows, LANES), f32)


def _mix_out(ya, gin, cv, gates, x, cache, wglu, wsc, wo, cw, cbias, g2, sb, lt):
    ns, length, _ = x.shape
    nt = length // lt
    kern = functools.partial(_mix_kernel, sb=sb, lt=lt, nchunk=4)
    perm = [_swap_scratch(sb * lt, D_MODEL)] if _swapped_rows(sb, lt) else []
    tile = lambda c: pl.BlockSpec((sb, lt, c), lambda s, i: (s, i, 0))
    flat = lambda c: pl.BlockSpec((sb * lt, c), lambda s, i: (s * nt + i, 0))
    return pl.pallas_call(
        kern,
        grid=(ns // sb, nt),
        in_specs=[
            pl.BlockSpec((N_SLAB, sb, lt, LANES), lambda s, i: (0, s, i, 0)),
            flat(D_B), flat(D_B), flat(2 * D_MODEL), tile(D_MODEL),
            pl.BlockSpec((sb, 2, D_B), lambda s, i: (s, 0, 0)),
            _const_spec((D_A, 2 * D_MODEL)),
            _const_spec((D_B, D_MODEL)),
            _const_spec((D_MODEL, D_MODEL)),
            _const_spec((3, D_B)),
            _const_spec((1, D_B)),
            _const_spec((1, D_MODEL)),
        ],
        out_specs=[
            tile(D_MODEL),
            flat(D_MODEL),
            pl.BlockSpec((sb, 2, D_B), lambda s, i: (s, 0, 0)),
        ],
        out_shape=[
            jax.ShapeDtypeStruct((ns, length, D_MODEL), f32),
            jax.ShapeDtypeStruct((ns * length, D_MODEL), bf16),
            jax.ShapeDtypeStruct((ns, 2, D_B), f32),
        ],
        scratch_shapes=[
            pltpu.VMEM((sb, lt + CONV_PAD, D_B), f32),
            pltpu.VMEM((sb * lt, D_MODEL), bf16),
        ] + perm,
        compiler_params=_params(("arbitrary", "arbitrary")),
        name="mix_out",
    )(ya, gin, cv, gates, x, cache, wglu, wsc, wo, cw, cbias, g2)


def _ffn_up_kernel(h2_ref, wv_ref, wg_ref, cwv_ref, cwg_ref, cbv_ref, cbg_ref,
                   cachev_ref, cacheg_ref, act_ref, nfv_ref, nfg_ref, extv_scr, extg_scr,
                   *, sb, lt):
    i = pl.program_id(2)
    rows = sb * lt
    lo = CONV_PAD - 2

    @pl.when(i == 0)
    def _():
        extv_scr[:, lo:CONV_PAD, :] = cachev_ref[...]
        extg_scr[:, lo:CONV_PAD, :] = cacheg_ref[...]

    @pl.when(i > 0)
    def _():
        extv_scr[:, lo:CONV_PAD, :] = extv_scr[:, lo + lt:CONV_PAD + lt, :]
        extg_scr[:, lo:CONV_PAD, :] = extg_scr[:, lo + lt:CONV_PAD + lt, :]

    h2 = h2_ref[...]
    extv_scr[:, CONV_PAD:, :] = _dot(h2, wv_ref[...]).reshape(sb, lt, FFN_HALF)
    extg_scr[:, CONV_PAD:, :] = _dot(h2, wg_ref[...]).reshape(sb, lt, FFN_HALF)
    for c in range(FFN_HALF // MXU_DIM):
        cols = slice(c * MXU_DIM, (c + 1) * MXU_DIM)
        val = _conv3(extv_scr, cwv_ref, cbv_ref, lt, cols)
        gt = _conv3(extg_scr, cwg_ref, cbg_ref, lt, cols)
        act_ref[:, cols] = (jax.nn.silu(gt) * val).reshape(rows, MXU_DIM).astype(bf16)
    nfv_ref[...] = extv_scr[:, lo + lt:CONV_PAD + lt, :]
    nfg_ref[...] = extg_scr[:, lo + lt:CONV_PAD + lt, :]


def _conv3_swapped(x, car_ref, w_ref, b_ref, cols):
    x6, x7 = x[:, SUBLANES - 2], x[:, SUBLANES - 1]
    prev6 = jnp.concatenate([car_ref[0, :, cols][None], x6[:-1]], axis=0)
    prev7 = jnp.concatenate([car_ref[1, :, cols][None], x7[:-1]], axis=0)
    car_ref[0, :, cols] = x6[-1]
    car_ref[1, :, cols] = x7[-1]
    first = lax.broadcasted_iota(jnp.int32, x6.shape, 1) == 0
    back2 = jnp.where(first, pltpu.roll(prev6, 1, axis=1), pltpu.roll(x6, 1, axis=1))
    back1 = jnp.where(first, pltpu.roll(prev7, 1, axis=1), pltpu.roll(x7, 1, axis=1))
    s1 = jnp.concatenate([back1[:, None], x[:, :SUBLANES - 1]], axis=1)
    s2 = jnp.concatenate([back2[:, None], back1[:, None], x[:, :SUBLANES - 2]], axis=1)
    return (s2 * w_ref[0:1, cols] + s1 * w_ref[1:2, cols] + x * w_ref[2:3, cols]
            + b_ref[:, cols])


def _ffn_up_swapped_kernel(h2_ref, wv_ref, wg_ref, cwv_ref, cwg_ref, cbv_ref, cbg_ref,
                           cachev_ref, cacheg_ref, act_ref, nfv_ref, nfg_ref,
                           carv_scr, carg_scr, *, lt):
    i = pl.program_id(2)
    nblk = lt // PERM_BLOCK

    @pl.when(i == 0)
    def _():
        for car, cache in ((carv_scr, cachev_ref), (carg_scr, cacheg_ref)):
            for r in range(2):
                car[r] = jnp.broadcast_to(cache[0, r:r + 1, :], (SUBLANES, FFN_HALF))

    h2 = h2_ref[...]
    upv = _dot(h2, wv_ref[...])
    upg = _dot(h2, wg_ref[...])
    for c in range(FFN_HALF // MXU_DIM):
        cols = slice(c * MXU_DIM, (c + 1) * MXU_DIM)
        shape = (nblk, SUBLANES, SUBLANES, MXU_DIM)
        val = _conv3_swapped(upv[:, cols].reshape(shape), carv_scr, cwv_ref, cbv_ref, cols)
        gt = _conv3_swapped(upg[:, cols].reshape(shape), carg_scr, cwg_ref, cbg_ref, cols)
        act_ref[:, cols] = (jax.nn.silu(gt) * val).reshape(lt, MXU_DIM).astype(bf16)
    for car, nf_ref in ((carv_scr, nfv_ref), (carg_scr, nfg_ref)):
        for r in range(2):
            nf_ref[0, r:r + 1, :] = car[r, SUBLANES - 1:SUBLANES, :]


def _ffn_down_kernel(act_ref, x1_ref, wd_ref, gf_ref, y_ref, *swap_scr, sb, lt):
    rows = sb * lt
    delta = _dot(act_ref[...], wd_ref[...])
    if swap_scr:
        delta = _swap_rows(delta, swap_scr[0])
    x2 = x1_ref[...].reshape(rows, D_MODEL) + delta
    y_ref[...] = _rmsnorm(x2, gf_ref[...]).reshape(sb, lt, D_MODEL)


def _ffn(h2, x1, gfin, w_up_p, w_down_p, cw_p, cb_p, cache_p, sb, lt):
    ns, length, _ = x1.shape
    nh = D_FF_PAD // FFN_HALF
    nt = length // lt
    resident = dict(pipeline_mode=pl.Buffered(1))
    val = lambda rows, **kw: pl.BlockSpec((rows, FFN_HALF), lambda h, s, i: (0, h), **kw)
    gate = lambda rows, **kw: pl.BlockSpec((rows, FFN_HALF), lambda h, s, i: (0, nh + h), **kw)
    swapped = _swapped_rows(sb, lt)
    if swapped:
        up_kernel = functools.partial(_ffn_up_swapped_kernel, lt=lt)
        up_scratch = [pltpu.VMEM((2, SUBLANES, FFN_HALF), f32)] * 2
        down_scratch = [_swap_scratch(sb * lt, D_MODEL)]
    else:
        up_kernel = functools.partial(_ffn_up_kernel, sb=sb, lt=lt)
        up_scratch = [pltpu.VMEM((sb, lt + CONV_PAD, FFN_HALF), f32)] * 2
        down_scratch = []
    act, nfv, nfg = pl.pallas_call(
        up_kernel,
        grid=(nh, ns // sb, nt),
        in_specs=[
            pl.BlockSpec((sb * lt, D_MODEL), lambda h, s, i: (s * nt + i, 0)),
            val(D_MODEL, **resident), gate(D_MODEL, **resident),
            val(3), gate(3), val(1), gate(1),
            pl.BlockSpec((sb, 2, FFN_HALF), lambda h, s, i: (s, 0, h)),
            pl.BlockSpec((sb, 2, FFN_HALF), lambda h, s, i: (s, 0, nh + h)),
        ],
        out_specs=[
            pl.BlockSpec((sb * lt, FFN_HALF), lambda h, s, i: (s * nt + i, h)),
            pl.BlockSpec((sb, None, 2, FFN_HALF), lambda h, s, i: (s, i, 0, h)),
            pl.BlockSpec((sb, None, 2, FFN_HALF), lambda h, s, i: (s, i, 0, h)),
        ],
        out_shape=[
            jax.ShapeDtypeStruct((ns * length, D_FF_PAD), bf16),
            jax.ShapeDtypeStruct((ns, nt, 2, D_FF_PAD), f32),
            jax.ShapeDtypeStruct((ns, nt, 2, D_FF_PAD), f32),
        ],
        scratch_shapes=up_scratch,
        compiler_params=_params(("arbitrary", "arbitrary", "arbitrary")),
        name="ffn_up",
    )(h2, w_up_p, w_up_p, cw_p, cw_p, cb_p, cb_p, cache_p, cache_p)
    y = pl.pallas_call(
        functools.partial(_ffn_down_kernel, sb=sb, lt=lt),
        grid=(ns // sb, nt),
        in_specs=[
            pl.BlockSpec((sb * lt, D_FF_PAD), lambda s, i: (s * nt + i, 0)),
            pl.BlockSpec((sb, lt, D_MODEL), lambda s, i: (s, i, 0)),
            _const_spec((D_FF_PAD, D_MODEL)),
            _const_spec((1, D_MODEL)),
        ],
        out_specs=pl.BlockSpec((sb, lt, D_MODEL), lambda s, i: (s, i, 0)),
        out_shape=jax.ShapeDtypeStruct((ns, length, D_MODEL), f32),
        scratch_shapes=down_scratch,
        compiler_params=_params(("arbitrary", "arbitrary")),
        name="ffn_down",
    )(act, x1, w_down_p, gfin)
    return y, nfv, nfg


def _pad_ff(a, axis):
    pad = [(0, 0)] * a.ndim
    pad[axis] = (0, D_FF_PAD - D_FF)
    if a.shape[axis] == D_FF:
        return jnp.pad(a, pad)
    lo, hi = jnp.split(a, 2, axis=axis)
    return jnp.concatenate([jnp.pad(lo, pad), jnp.pad(hi, pad)], axis=axis)


def _trunk(x, s_re, s_im, sc_buf, ffn_buf, w, cfg):
    ns, length, _ = x.shape
    tm_in, sb_mix, lt_mix, sb_ffn, lt_ffn, cb = cfg
    xf = x.reshape(ns * length, D_MODEL)
    u_a, gin, cv, gates = _in_proj(xf, w["norm1_g"], w["w_in"], tm_in)
    ya, n_re, n_im = _s5_mixer(u_a, s_re, s_im, w["s5"], cb)
    x1, h2, new_sc = _mix_out(ya.reshape(N_SLAB, ns, length, LANES), gin, cv, gates,
                              x, sc_buf, w["w_glu"], w["w_sc_out"], w["w_o"],
                              w["sc_conv_w"], w["sc_conv_b"], w["norm2_g"], sb_mix, lt_mix)
    y, nfv, nfg = _ffn(h2, x1, w["final_norm_g"], w["w_up"], w["w_down"],
                       w["ffn_conv_w"], w["ffn_conv_b"], _pad_ff(ffn_buf, 2), sb_ffn, lt_ffn)
    new_ffn = jnp.concatenate([nfv[:, -1, :, :D_FF], nfg[:, -1, :, :D_FF]], axis=-1)
    return y, n_re[None], n_im[None], new_sc[None], new_ffn[None]


def kernel(x_prompt, x_sample, state_s5_re, state_s5_im, cache_sc_conv, cache_ffn_conv, norm1_g, w_in, lam_re, lam_im, log_dt, b_re, b_im, c_re, c_im, d_skip, w_glu, sc_conv_w, sc_conv_b, w_sc_out, w_o, norm2_g, w_up, ffn_conv_w, ffn_conv_b, w_down, final_norm_g):
    w = {
        "norm1_g": norm1_g[0].reshape(1, D_MODEL),
        "w_in": w_in[0].astype(bf16),
        "s5": _s5_tables(lam_re[0], lam_im[0], log_dt[0], b_re[0], b_im[0],
                         c_re[0], c_im[0], d_skip[0]),
        "w_glu": w_glu[0].astype(bf16),
        "sc_conv_w": sc_conv_w[0],
        "sc_conv_b": sc_conv_b[0].reshape(1, D_B),
        "w_sc_out": w_sc_out[0].astype(bf16),
        "w_o": w_o[0].astype(bf16),
        "norm2_g": norm2_g[0].reshape(1, D_MODEL),
        "w_up": _pad_ff(w_up[0].astype(bf16), 1),
        "ffn_conv_w": _pad_ff(ffn_conv_w[0], 1),
        "ffn_conv_b": _pad_ff(ffn_conv_b[0].reshape(1, 2 * D_FF), 1),
        "w_down": _pad_ff(w_down[0].astype(bf16), 0),
        "final_norm_g": final_norm_g.reshape(1, D_MODEL),
    }
    bp, lp, _ = x_prompt.shape
    bs, ls, _ = x_sample.shape
    zeros = lambda *s: jnp.zeros(s, f32)
    p_cfg = (256, 1, 256, 1, 256, 64)
    yp, p_re, p_im, p_sc, p_ffn = _trunk(
        x_prompt, zeros(bp, G_A, N_STATE), zeros(bp, G_A, N_STATE),
        zeros(bp, 2, D_B), zeros(bp, 2, 2 * D_FF), w, p_cfg)
    s_cfg = (256, bs // 2, ls, bs // 2, ls, 1)
    ys, s_re, s_im, s_sc, s_ffn = _trunk(
        x_sample, state_s5_re[0], state_s5_im[0], cache_sc_conv[0], cache_ffn_conv[0], w, s_cfg)
    return (yp, ys, p_re, p_im, p_sc, p_ffn, s_re, s_im, s_sc, s_ffn)
```

```python
import functools

import jax
import jax.numpy as jnp
import numpy as np
from jax import lax
from jax.experimental import pallas as pl
from jax.experimental.pallas import tpu as pltpu

D_MODEL = 2048
D_A = D_MODEL // 2
S5_GROUP = 16
G_A = D_A // S5_GROUP
N_STATE = 64
D_B = D_MODEL // 2
D_FF = 5504
EPS = 1e-6
IN_COLS = D_A + 3 * D_B + 2 * D_MODEL

LANES = 128
SUBLANES = 8
MXU_DIM = 256
VMEM_LIMIT_BYTES = 58 * 1024 * 1024

N_SLAB = D_A // LANES
SLAB_G = LANES // S5_GROUP
S5_T = 16
S5_TQ = S5_T * S5_GROUP
S5_K = S5_T * LANES
S5_ST = 2 * N_STATE
S5_SL = SLAB_G * S5_ST
D_FF_PAD = 5632
FFN_HALF = D_FF_PAD // 2
CONV_PAD = SUBLANES

bf16 = jnp.bfloat16
f32 = jnp.float32


def _rmsnorm(x, g):
    ms = jnp.mean(x * x, axis=-1, keepdims=True)
    return x * lax.rsqrt(ms + EPS) * g


def _params(sem):
    return pltpu.CompilerParams(dimension_semantics=sem, vmem_limit_bytes=VMEM_LIMIT_BYTES)


def _const_spec(shape):
    nd = len(shape)
    return pl.BlockSpec(shape, lambda *_: (0,) * nd, pipeline_mode=pl.Buffered(1))


def _dot(a, b):
    return jnp.dot(a, b, preferred_element_type=f32)


def _in_proj_kernel(x_ref, g_ref, w_ref, u_ref, gin_ref, cv_ref, gates_ref):
    h = _rmsnorm(x_ref[...], g_ref[...]).astype(bf16)
    u = _dot(h, w_ref[:, :D_A])
    for k in range(N_SLAB):
        u_ref[k] = u[:, k * LANES:(k + 1) * LANES]
    gin_ref[...] = _dot(h, w_ref[:, D_A:D_A + D_B]).astype(bf16)
    c = _dot(h, w_ref[:, D_A + D_B:D_A + 2 * D_B])
    v = _dot(h, w_ref[:, D_A + 2 * D_B:D_A + 3 * D_B])
    cv_ref[...] = (c * v).astype(bf16)
    g0 = D_A + 3 * D_B
    for n in range(2 * D_MODEL // D_B):
        acc = _dot(h, w_ref[:, g0 + n * D_B:g0 + (n + 1) * D_B])
        gates_ref[:, n * D_B:(n + 1) * D_B] = jax.nn.sigmoid(acc).astype(bf16)


def _in_proj(x, norm_g, w_in_b, tm):
    t = x.shape[0]
    return pl.pallas_call(
        _in_proj_kernel,
        grid=(t // tm,),
        in_specs=[
            pl.BlockSpec((tm, D_MODEL), lambda i: (i, 0)),
            _const_spec((1, D_MODEL)),
            _const_spec((D_MODEL, IN_COLS)),
        ],
        out_specs=[
            pl.BlockSpec((N_SLAB, tm, LANES), lambda i: (0, i, 0)),
            pl.BlockSpec((tm, D_B), lambda i: (i, 0)),
            pl.BlockSpec((tm, D_B), lambda i: (i, 0)),
            pl.BlockSpec((tm, 2 * D_MODEL), lambda i: (i, 0)),
        ],
        out_shape=[
            jax.ShapeDtypeStruct((N_SLAB, t, LANES), f32),
            jax.ShapeDtypeStruct((t, D_B), bf16),
            jax.ShapeDtypeStruct((t, D_B), bf16),
            jax.ShapeDtypeStruct((t, 2 * D_MODEL), bf16),
        ],
        compiler_params=_params(("arbitrary",)),
        name="in_proj",
    )(x, norm_g, w_in_b)


def _group_mask(shape, row0, row_shift, lane_shift):
    r = lax.broadcasted_iota(jnp.int32, shape, 0) + row0
    c = lax.broadcasted_iota(jnp.int32, shape, 1)
    return ((r >> row_shift) & (SLAB_G - 1)) == ((c >> lane_shift) & (SLAB_G - 1))


def _expand_tables(kc_ref, vc_ref, wc_refs, r_ref, m_scr, v_scr, w_scrs):
    rep = r_ref[...]
    bd = _dot(kc_ref[...], rep)
    bd = jnp.where(_group_mask(bd.shape, 0, 4, 4), bd, 0.0).astype(bf16)
    for t in range(S5_T):
        rows = slice(t * LANES, (t + 1) * LANES)
        if t:
            m_scr[rows, :t * LANES] = jnp.zeros((LANES, t * LANES), bf16)
        m_scr[rows, t * LANES:] = bd[:, :(S5_T - t) * LANES]
    step = MXU_DIM
    for r0 in range(0, S5_SL, step):
        vb = _dot(vc_ref[r0:r0 + step, :], rep)
        v_scr[r0:r0 + step, :] = jnp.where(_group_mask(vb.shape, r0, 7, 4), vb, 0.0).astype(bf16)
    for wc_ref, w_scr in zip(wc_refs, w_scrs):
        for r0 in range(0, S5_K, step):
            wt = jnp.concatenate([wc_ref[r0:r0 + step, :].astype(f32)] * SLAB_G, axis=1)
            w_scr[r0:r0 + step, :] = jnp.where(_group_mask(wt.shape, r0, 4, 7), wt, 0.0).astype(bf16)


def _s5_toeplitz(lhs, m_scr):
    return [_dot(lhs[:, :(a + 1) * MXU_DIM],
                 m_scr[:(a + 1) * MXU_DIM, a * MXU_DIM:(a + 1) * MXU_DIM])
            for a in range(S5_K // MXU_DIM)]


def _s5_outputs(toep, sprev, xs, v_scr, d_ref, store):
    per = MXU_DIM // LANES
    for a, part in enumerate(toep):
        y = part + _dot(sprev, v_scr[:, a * MXU_DIM:(a + 1) * MXU_DIM])
        for i in range(per):
            t = a * per + i
            piece = y[:, i * LANES:(i + 1) * LANES] + d_ref[...] * xs[t]
            store(t, jax.nn.gelu(piece))


def _s5_prompt_kernel(u_ref, kc_ref, vc_ref, wc_ref, r_ref, pq_ref, d_ref, h0_ref, h0s_ref,
                      y_ref, sfin_ref,
                      m_scr, v_scr, w_scr, s_scr, t_scr, loc_scr, loct_scr, prev_scr, *, nb, cb):
    ci = pl.program_id(1)

    @pl.when(ci == 0)
    def _():
        _expand_tables(kc_ref, vc_ref, [wc_ref], r_ref, m_scr, v_scr, [w_scr])
        s_scr[...] = h0_ref[...]
        t_scr[...] = h0s_ref[...]
        loc_scr[...] = jnp.zeros_like(loc_scr)
        loct_scr[...] = jnp.zeros_like(loct_scr)

    xs = [jnp.concatenate([u_ref[b, pl.ds(t, cb, stride=S5_T), :] for b in range(nb)], axis=0)
          for t in range(S5_T)]
    lhs = jnp.concatenate(xs, axis=1).astype(bf16)

    loc = _dot(lhs, w_scr[...])
    toep = _s5_toeplitz(lhs, m_scr)
    for j in range(SLAB_G):
        lj = loc[:, j * S5_ST:(j + 1) * S5_ST]
        ljt = pltpu.roll(lj, N_STATE, axis=1)
        for b in range(nb):
            rows = slice(b * cb, (b + 1) * cb)
            loc_scr[j, pl.ds(b, cb, stride=SUBLANES), :] = lj[rows]
            loct_scr[j, pl.ds(b, cb, stride=SUBLANES), :] = ljt[rows]

    ss = [s_scr[j] for j in range(SLAB_G)]
    ts = [t_scr[j] for j in range(SLAB_G)]
    for c in range(cb):
        rows = slice(c * SUBLANES, (c + 1) * SUBLANES)
        for j in range(SLAB_G):
            p = pq_ref[0, j:j + 1, :]
            q = pq_ref[1, j:j + 1, :]
            prev_scr[j, rows, :] = ss[j]
            ss[j], ts[j] = (p * ss[j] + q * ts[j] + loc_scr[j, rows, :],
                            p * ts[j] - q * ss[j] + loct_scr[j, rows, :])
    for j in range(SLAB_G):
        s_scr[j] = ss[j]
        t_scr[j] = ts[j]
        sfin_ref[j] = ss[j]

    sprev = jnp.concatenate(
        [jnp.concatenate([prev_scr[j, pl.ds(b, cb, stride=SUBLANES), :] for b in range(nb)], axis=0)
         for j in range(SLAB_G)], axis=1).astype(bf16)

    def store(t, val):
        for b in range(nb):
            y_ref[b, pl.ds(t, cb, stride=S5_T), :] = val[b * cb:(b + 1) * cb]

    _s5_outputs(toep, sprev, xs, v_scr, d_ref, store)


def _s5_prompt(u, tables, h0, h0s, cb):
    kc, vc, wc, _, rep, pq, dvec = tables
    _, nb, length, _ = u.shape
    nc = length // S5_T
    kern = functools.partial(_s5_prompt_kernel, nb=nb, cb=cb)
    tok = pl.BlockSpec((None, nb, cb * S5_T, LANES), lambda k, ci: (k, 0, ci, 0))
    per_slab = lambda *shape: pl.BlockSpec((None,) + shape, lambda k, ci: (k,) + (0,) * len(shape))
    state = per_slab(SLAB_G, SUBLANES, S5_ST)
    return pl.pallas_call(
        kern,
        grid=(N_SLAB, nc // cb),
        in_specs=[tok, per_slab(LANES, S5_TQ), per_slab(S5_SL, S5_TQ), per_slab(S5_K, S5_ST),
                  _const_spec((S5_TQ, S5_K)),
                  per_slab(2, SLAB_G, S5_ST), per_slab(1, LANES), state, state],
        out_specs=[tok, state],
        out_shape=[
            jax.ShapeDtypeStruct(u.shape, f32),
            jax.ShapeDtypeStruct((N_SLAB, SLAB_G, SUBLANES, S5_ST), f32),
        ],
        scratch_shapes=[
            pltpu.VMEM((S5_K, S5_K), bf16),
            pltpu.VMEM((S5_SL, S5_K), bf16),
            pltpu.VMEM((S5_K, S5_SL), bf16),
            pltpu.VMEM((SLAB_G, SUBLANES, S5_ST), f32),
            pltpu.VMEM((SLAB_G, SUBLANES, S5_ST), f32),
            pltpu.VMEM((SLAB_G, cb * SUBLANES, S5_ST), f32),
            pltpu.VMEM((SLAB_G, cb * SUBLANES, S5_ST), f32),
            pltpu.VMEM((SLAB_G, cb * SUBLANES, S5_ST), f32),
        ],
        compiler_params=_params(("arbitrary", "arbitrary")),
        name="s5_prompt",
    )(u, kc, vc, wc, rep, pq, dvec, h0, h0s)


def _s5_sample_kernel(u_ref, kc_ref, vc_ref, wc_ref, wlc_ref, r_ref, pq_ref, d_ref, h0_ref, h0s_ref,
                      y_ref, sfin_ref, m_scr, v_scr, wh_scr, wl_scr, *, nb):
    _expand_tables(kc_ref, vc_ref, [wc_ref, wlc_ref], r_ref, m_scr, v_scr, [wh_scr, wl_scr])
    xs = [u_ref[pl.ds(t, nb, stride=S5_T), :] for t in range(S5_T)]
    lhs_f = jnp.concatenate(xs, axis=1)
    lhs = lhs_f.astype(bf16)
    lhs_lo = (lhs_f - lhs.astype(f32)).astype(bf16)
    wh = wh_scr[...]
    loc = _dot(lhs, wh) + _dot(lhs_lo, wh) + _dot(lhs, wl_scr[...])
    for j in range(SLAB_G):
        p = pq_ref[0, j:j + 1, :]
        q = pq_ref[1, j:j + 1, :]
        sfin_ref[j] = p * h0_ref[j] + q * h0s_ref[j] + loc[:, j * S5_ST:(j + 1) * S5_ST]
    sprev = jnp.concatenate([h0_ref[j] for j in range(SLAB_G)], axis=1).astype(bf16)

    def store(t, val):
        y_ref[pl.ds(t, nb, stride=S5_T), :] = val

    _s5_outputs(_s5_toeplitz(lhs, m_scr), sprev, xs, v_scr, d_ref, store)


def _s5_sample(u, tables, h0, h0s, nb):
    kc, vc, wc, wlc, rep, pq, dvec = tables
    kern = functools.partial(_s5_sample_kernel, nb=nb)
    per_slab = lambda *shape: pl.BlockSpec((None,) + shape, lambda k: (k,) + (0,) * len(shape))
    tok = per_slab(nb * S5_T, LANES)
    state = per_slab(SLAB_G, nb, S5_ST)
    return pl.pallas_call(
        kern,
        grid=(N_SLAB,),
        in_specs=[tok, per_slab(LANES, S5_TQ), per_slab(S5_SL, S5_TQ), per_slab(S5_K, S5_ST),
                  per_slab(S5_K, S5_ST), _const_spec((S5_TQ, S5_K)),
                  per_slab(2, SLAB_G, S5_ST), per_slab(1, LANES), state, state],
        out_specs=[tok, state],
        out_shape=[
            jax.ShapeDtypeStruct(u.shape, f32),
            jax.ShapeDtypeStruct((N_SLAB, SLAB_G, nb, S5_ST), f32),
        ],
        scratch_shapes=[
            pltpu.VMEM((S5_K, S5_K), bf16),
            pltpu.VMEM((S5_SL, S5_K), bf16),
            pltpu.VMEM((S5_K, S5_SL), bf16),
            pltpu.VMEM((S5_K, S5_SL), bf16),
        ],
        compiler_params=_params(("arbitrary",)),
        name="s5_sample",
    )(u, kc, vc, wc, wlc, rep, pq, dvec, h0, h0s)


def _replication_matrix():
    src = np.arange(S5_TQ)
    dst = np.arange(S5_K)
    same = ((src[:, None] // S5_GROUP == dst[None, :] // LANES)
            & (src[:, None] % S5_GROUP == dst[None, :] % S5_GROUP))
    return jnp.asarray(same, dtype=bf16)


def _s5_tables(lam_re, lam_im, log_dt, b_re, b_im, c_re, c_im, d_skip):
    hp = lax.Precision.HIGHEST
    dt = jnp.exp(log_dt.astype(f32))[:, None]
    lr, li = lam_re.astype(f32), lam_im.astype(f32)
    mag = jnp.exp(lr * dt)
    ab_re = mag * jnp.cos(li * dt)
    ab_im = mag * jnp.sin(li * dt)
    den = lr * lr + li * li
    nr, ni = ab_re - 1.0, ab_im
    f_re = (nr * lr + ni * li) / den
    f_im = (ni * lr - nr * li) / den
    br, bi = b_re.astype(f32), b_im.astype(f32)
    bb_re = f_re[..., None] * br - f_im[..., None] * bi
    bb_im = f_re[..., None] * bi + f_im[..., None] * br
    cr, ci = c_re.astype(f32), c_im.astype(f32)

    k = jnp.arange(S5_T + 1, dtype=f32)[:, None, None]
    pm = jnp.exp(lr * dt * k)
    pr = pm * jnp.cos(li * dt * k)
    pi = pm * jnp.sin(li * dt * k)

    er = cr[None] * pr[:S5_T, :, None, :] - ci[None] * pi[:S5_T, :, None, :]
    ei = cr[None] * pi[:S5_T, :, None, :] + ci[None] * pr[:S5_T, :, None, :]
    kk = (jnp.einsum('tgqn,gnp->gptq', er, bb_re, precision=hp)
          - jnp.einsum('tgqn,gnp->gptq', ei, bb_im, precision=hp))
    kc = kk.reshape(N_SLAB, LANES, S5_TQ).astype(bf16)

    def slab_time(a):
        return a.reshape(S5_T, N_SLAB, SLAB_G, 1, N_STATE).transpose(1, 0, 2, 3, 4)

    back = S5_T - 1 - jnp.arange(S5_T)
    prr, pir = slab_time(pr[back]), slab_time(pi[back])
    bt_re = bb_re.transpose(0, 2, 1).reshape(N_SLAB, 1, SLAB_G, S5_GROUP, N_STATE)
    bt_im = bb_im.transpose(0, 2, 1).reshape(N_SLAB, 1, SLAB_G, S5_GROUP, N_STATE)
    lanes = lambda a, b: jnp.concatenate([a, b], axis=-1)
    w = lanes(prr, prr) * lanes(bt_re, bt_im) + lanes(-pir, pir) * lanes(bt_im, bt_re)
    w = w.reshape(N_SLAB, S5_K, S5_ST)
    wc = w.astype(bf16)
    wlc = (w - wc.astype(f32)).astype(bf16)

    lag = np.arange(S5_TQ) // S5_GROUP
    col = np.arange(S5_TQ) % S5_GROUP
    fill_q = jnp.asarray(np.arange(S5_GROUP)[:, None] == col[None, :], dtype=f32)
    fill_t = jnp.asarray(np.arange(S5_T)[:, None] == lag[None, :], dtype=f32)
    rows = G_A * N_STATE
    ct_re = jnp.dot(cr.transpose(0, 2, 1).reshape(rows, S5_GROUP), fill_q, precision=hp)
    ct_im = jnp.dot(ci.transpose(0, 2, 1).reshape(rows, S5_GROUP), fill_q, precision=hp)
    pt_re = jnp.dot(pr[1:].transpose(1, 2, 0).reshape(rows, S5_T), fill_t, precision=hp)
    pt_im = jnp.dot(pi[1:].transpose(1, 2, 0).reshape(rows, S5_T), fill_t, precision=hp)
    v_re = (ct_re * pt_re - ct_im * pt_im).reshape(G_A, N_STATE, S5_TQ)
    v_im = (-(ct_re * pt_im + ct_im * pt_re)).reshape(G_A, N_STATE, S5_TQ)
    vc = jnp.concatenate([v_re, v_im], axis=1).reshape(N_SLAB, S5_SL, S5_TQ).astype(bf16)

    ar, ai = pr[S5_T], pi[S5_T]
    p_row = jnp.concatenate([ar, ar], axis=1).reshape(N_SLAB, 1, SLAB_G, S5_ST)
    q_row = jnp.concatenate([-ai, ai], axis=1).reshape(N_SLAB, 1, SLAB_G, S5_ST)
    pq = jnp.concatenate([p_row, q_row], axis=1)
    dvec = d_skip.astype(f32).reshape(N_SLAB, 1, LANES)
    return kc, vc, wc, wlc, _replication_matrix(), pq, dvec


def _s5_mixer(u, s_re, s_im, tables, cb):
    nb = s_re.shape[0]
    length = u.shape[1] // nb

    def to_slab(a):
        return a.reshape(nb, N_SLAB, SLAB_G, S5_ST).transpose(1, 2, 0, 3)

    h0 = to_slab(jnp.concatenate([s_re, s_im], axis=-1))
    h0s = to_slab(jnp.concatenate([s_im, s_re], axis=-1))
    if length == S5_T:
        y, sfin = _s5_sample(u, tables, h0, h0s, nb)
    else:
        pad = [(0, 0), (0, 0), (0, SUBLANES - nb), (0, 0)]
        y, sfin = _s5_prompt(u.reshape(N_SLAB, nb, length, LANES), tables,
                             jnp.pad(h0, pad), jnp.pad(h0s, pad), cb)
        y = y.reshape(u.shape)
        sfin = sfin[:, :, :nb]
    sfin = sfin.transpose(2, 0, 1, 3).reshape(nb, G_A, S5_ST)
    return y, sfin[..., :N_STATE], sfin[..., N_STATE:]


def _conv3(ext_ref, w_ref, b_ref, lt, cols=slice(None)):
    lo = CONV_PAD - 2
    out = ext_ref[:, lo:lo + lt, cols] * w_ref[0:1, cols]
    out = out + ext_ref[:, lo + 1:lo + 1 + lt, cols] * w_ref[1:2, cols]
    out = out + ext_ref[:, lo + 2:lo + 2 + lt, cols] * w_ref[2:3, cols]
    return out + b_ref[:, cols]


PERM_BLOCK = SUBLANES * SUBLANES


def _swap_rows(val, scr):
    rows, width = val.shape
    nsl = width // LANES
    for k in range(nsl):
        scr[k] = val[:, k * LANES:(k + 1) * LANES]
    return jnp.concatenate(
        [jnp.concatenate([scr[k, pl.ds(r0 + m, SUBLANES, stride=SUBLANES), :]
                          for r0 in range(0, rows, PERM_BLOCK) for m in range(SUBLANES)], axis=0)
         for k in range(nsl)], axis=1)


def _mix_kernel(ya_ref, gin_ref, cv_ref, gates_ref, x_ref, cache_ref,
                wglu_ref, wsc_ref, wo_ref, cw_ref, cb_ref, g2_ref,
                x1_ref, h2_ref, nsc_ref, ext_scr, mrg_scr, *perm_scr, sb, lt, nchunk):
    i = pl.program_id(1)
    rows = sb * lt
    lo = CONV_PAD - 2

    @pl.when(i == 0)
    def _():
        ext_scr[:, lo:CONV_PAD, :] = cache_ref[...]

    @pl.when(i > 0)
    def _():
        ext_scr[:, lo:CONV_PAD, :] = ext_scr[:, lo + lt:CONV_PAD + lt, :]

    ext_scr[:, CONV_PAD:, :] = cv_ref[...].astype(f32).reshape(sb, lt, D_B)
    nsc_ref[...] = ext_scr[:, lo + lt:CONV_PAD + lt, :]

    conv = _conv3(ext_scr, cw_ref, cb_ref, lt).reshape(rows, D_B)
    gated = (gin_ref[...].astype(f32) * conv).astype(bf16)
    ya = jnp.concatenate([ya_ref[k].reshape(rows, LANES) for k in range(N_SLAB)],
                         axis=1).astype(bf16)
    wc = D_MODEL // nchunk
    for n in range(nchunk):
        cols = slice(n * wc, (n + 1) * wc)
        gcols = slice(D_MODEL + n * wc, D_MODEL + (n + 1) * wc)
        a = _dot(ya, wglu_ref[:, cols])
        gt = _dot(ya, wglu_ref[:, gcols])
        br_a = a * jax.nn.sigmoid(gt)
        br_b = _dot(gated, wsc_ref[:, cols])
        g_a = gates_ref[:, cols].astype(f32)
        g_b = gates_ref[:, gcols].astype(f32)
        mrg_scr[:, cols] = (g_a * br_a + g_b * br_b).astype(bf16)
    x1 = x_ref[...].reshape(rows, D_MODEL) + _dot(mrg_scr[...], wo_ref[...])
    x1_ref[...] = x1.reshape(sb, lt, D_MODEL)
    h2 = _rmsnorm(x1, g2_ref[...])
    if perm_scr:
        h2 = _swap_rows(h2, perm_scr[0])
    h2_ref[...] = h2.astype(bf16)


def _swapped_rows(sb, lt):
    return sb == 1 and lt % PERM_BLOCK == 0


def _swap_scratch(rows, width):
    return pltpu.VMEM((width // LANES, rows, LANES), f32)


def _mix_out(ya, gin, cv, gates, x, cache, wglu, wsc, wo, cw, cbias, g2, sb, lt):
    ns, length, _ = x.shape
    nt = length // lt
    kern = functools.partial(_mix_kernel, sb=sb, lt=lt, nchunk=4)
    perm = [_swap_scratch(sb * lt, D_MODEL)] if _swapped_rows(sb, lt) else []
    tile = lambda c: pl.BlockSpec((sb, lt, c), lambda s, i: (s, i, 0))
    flat = lambda c: pl.BlockSpec((sb * lt, c), lambda s, i: (s * nt + i, 0))
    return pl.pallas_call(
        kern,
        grid=(ns // sb, nt),
        in_specs=[
            pl.BlockSpec((N_SLAB, sb, lt, LANES), lambda s, i: (0, s, i, 0)),
            flat(D_B), flat(D_B), flat(2 * D_MODEL), tile(D_MODEL),
            pl.BlockSpec((sb, 2, D_B), lambda s, i: (s, 0, 0)),
            _const_spec((D_A, 2 * D_MODEL)),
            _const_spec((D_B, D_MODEL)),
            _const_spec((D_MODEL, D_MODEL)),
            _const_spec((3, D_B)),
            _const_spec((1, D_B)),
            _const_spec((1, D_MODEL)),
        ],
        out_specs=[
            tile(D_MODEL),
            flat(D_MODEL),
            pl.BlockSpec((sb, 2, D_B), lambda s, i: (s, 0, 0)),
        ],
        out_shape=[
            jax.ShapeDtypeStruct((ns, length, D_MODEL), f32),
            jax.ShapeDtypeStruct((ns * length, D_MODEL), bf16),
            jax.ShapeDtypeStruct((ns, 2, D_B), f32),
        ],
        scratch_shapes=[
            pltpu.VMEM((sb, lt + CONV_PAD, D_B), f32),
            pltpu.VMEM((sb * lt, D_MODEL), bf16),
        ] + perm,
        compiler_params=_params(("arbitrary", "arbitrary")),
        name="mix_out",
    )(ya, gin, cv, gates, x, cache, wglu, wsc, wo, cw, cbias, g2)


def _ffn_up_kernel(h2_ref, wv_ref, wg_ref, cwv_ref, cwg_ref, cbv_ref, cbg_ref,
                   cachev_ref, cacheg_ref, act_ref, nfv_ref, nfg_ref, extv_scr, extg_scr,
                   *, sb, lt):
    i = pl.program_id(2)
    rows = sb * lt
    lo = CONV_PAD - 2

    @pl.when(i == 0)
    def _():
        extv_scr[:, lo:CONV_PAD, :] = cachev_ref[...]
        extg_scr[:, lo:CONV_PAD, :] = cacheg_ref[...]

    @pl.when(i > 0)
    def _():
        extv_scr[:, lo:CONV_PAD, :] = extv_scr[:, lo + lt:CONV_PAD + lt, :]
        extg_scr[:, lo:CONV_PAD, :] = extg_scr[:, lo + lt:CONV_PAD + lt, :]

    h2 = h2_ref[...]
    extv_scr[:, CONV_PAD:, :] = _dot(h2, wv_ref[...]).reshape(sb, lt, FFN_HALF)
    extg_scr[:, CONV_PAD:, :] = _dot(h2, wg_ref[...]).reshape(sb, lt, FFN_HALF)
    for c in range(FFN_HALF // MXU_DIM):
        cols = slice(c * MXU_DIM, (c + 1) * MXU_DIM)
        val = _conv3(extv_scr, cwv_ref, cbv_ref, lt, cols)
        gt = _conv3(extg_scr, cwg_ref, cbg_ref, lt, cols)
        act_ref[:, cols] = (jax.nn.silu(gt) * val).reshape(rows, MXU_DIM).astype(bf16)
    nfv_ref[...] = extv_scr[:, lo + lt:CONV_PAD + lt, :]
    nfg_ref[...] = extg_scr[:, lo + lt:CONV_PAD + lt, :]


def _conv3_swapped(x, car_ref, w_ref, b_ref, cols):
    x6, x7 = x[:, SUBLANES - 2], x[:, SUBLANES - 1]
    prev6 = jnp.concatenate([car_ref[0, :, cols][None], x6[:-1]], axis=0)
    prev7 = jnp.concatenate([car_ref[1, :, cols][None], x7[:-1]], axis=0)
    car_ref[0, :, cols] = x6[-1]
    car_ref[1, :, cols] = x7[-1]
    first = lax.broadcasted_iota(jnp.int32, x6.shape, 1) == 0
    back2 = jnp.where(first, pltpu.roll(prev6, 1, axis=1), pltpu.roll(x6, 1, axis=1))
    back1 = jnp.where(first, pltpu.roll(prev7, 1, axis=1), pltpu.roll(x7, 1, axis=1))
    s1 = jnp.concatenate([back1[:, None], x[:, :SUBLANES - 1]], axis=1)
    s2 = jnp.concatenate([back2[:, None], back1[:, None], x[:, :SUBLANES - 2]], axis=1)
    return (s2 * w_ref[0:1, cols] + s1 * w_ref[1:2, cols] + x * w_ref[2:3, cols]
            + b_ref[:, cols])


def _ffn_up_swapped_kernel(h2_ref, wv_ref, wg_ref, cwv_ref, cwg_ref, cbv_ref, cbg_ref,
                           cachev_ref, cacheg_ref, act_ref, nfv_ref, nfg_ref,
                           carv_scr, carg_scr, *, lt):
    i = pl.program_id(2)
    nblk = lt // PERM_BLOCK

    @pl.when(i == 0)
    def _():
        for car, cache in ((carv_scr, cachev_ref), (carg_scr, cacheg_ref)):
            for r in range(2):
                car[r] = jnp.broadcast_to(cache[0, r:r + 1, :], (SUBLANES, FFN_HALF))

    h2 = h2_ref[...]
    upv = _dot(h2, wv_ref[...])
    upg = _dot(h2, wg_ref[...])
    for c in range(FFN_HALF // MXU_DIM):
        cols = slice(c * MXU_DIM, (c + 1) * MXU_DIM)
        shape = (nblk, SUBLANES, SUBLANES, MXU_DIM)
        val = _conv3_swapped(upv[:, cols].reshape(shape), carv_scr, cwv_ref, cbv_ref, cols)
        gt = _conv3_swapped(upg[:, cols].reshape(shape), carg_scr, cwg_ref, cbg_ref, cols)
        act_ref[:, cols] = (jax.nn.silu(gt) * val).reshape(lt, MXU_DIM).astype(bf16)
    for car, nf_ref in ((carv_scr, nfv_ref), (carg_scr, nfg_ref)):
        for r in range(2):
            nf_ref[0, r:r + 1, :] = car[r, SUBLANES - 1:SUBLANES, :]


def _ffn_down_kernel(act_ref, x1_ref, wd_ref, gf_ref, y_ref, *swap_scr, sb, lt):
    rows = sb * lt
    delta = _dot(act_ref[...], wd_ref[...])
    if swap_scr:
        delta = _swap_rows(delta, swap_scr[0])
    x2 = x1_ref[...].reshape(rows, D_MODEL) + delta
    y_ref[...] = _rmsnorm(x2, gf_ref[...]).reshape(sb, lt, D_MODEL)


def _ffn(h2, x1, gfin, w_up_p, w_down_p, cw_p, cb_p, cache_p, sb, lt, lt_up):
    ns, length, _ = x1.shape
    nh = D_FF_PAD // FFN_HALF
    nt = length // lt
    ntu = length // lt_up
    resident = dict(pipeline_mode=pl.Buffered(1))
    val = lambda rows, **kw: pl.BlockSpec((rows, FFN_HALF), lambda h, s, i: (0, h), **kw)
    gate = lambda rows, **kw: pl.BlockSpec((rows, FFN_HALF), lambda h, s, i: (0, nh + h), **kw)
    swapped = _swapped_rows(sb, lt)
    if swapped:
        assert _swapped_rows(sb, lt_up)
        up_kernel = functools.partial(_ffn_up_swapped_kernel, lt=lt_up)
        up_scratch = [pltpu.VMEM((2, SUBLANES, FFN_HALF), f32)] * 2
        down_scratch = [_swap_scratch(sb * lt, D_MODEL)]
    else:
        up_kernel = functools.partial(_ffn_up_kernel, sb=sb, lt=lt_up)
        up_scratch = [pltpu.VMEM((sb, lt_up + CONV_PAD, FFN_HALF), f32)] * 2
        down_scratch = []
    act, nfv, nfg = pl.pallas_call(
        up_kernel,
        grid=(nh, ns // sb, ntu),
        in_specs=[
            pl.BlockSpec((sb * lt_up, D_MODEL), lambda h, s, i: (s * ntu + i, 0)),
            val(D_MODEL, **resident), gate(D_MODEL, **resident),
            val(3), gate(3), val(1), gate(1),
            pl.BlockSpec((sb, 2, FFN_HALF), lambda h, s, i: (s, 0, h)),
            pl.BlockSpec((sb, 2, FFN_HALF), lambda h, s, i: (s, 0, nh + h)),
        ],
        out_specs=[
            pl.BlockSpec((sb * lt_up, FFN_HALF), lambda h, s, i: (s * ntu + i, h)),
            pl.BlockSpec((sb, None, 2, FFN_HALF), lambda h, s, i: (s, i, 0, h)),
            pl.BlockSpec((sb, None, 2, FFN_HALF), lambda h, s, i: (s, i, 0, h)),
        ],
        out_shape=[
            jax.ShapeDtypeStruct((ns * length, D_FF_PAD), bf16),
            jax.ShapeDtypeStruct((ns, ntu, 2, D_FF_PAD), f32),
            jax.ShapeDtypeStruct((ns, ntu, 2, D_FF_PAD), f32),
        ],
        scratch_shapes=up_scratch,
        compiler_params=_params(("arbitrary", "arbitrary", "arbitrary")),
        name="ffn_up",
    )(h2, w_up_p, w_up_p, cw_p, cw_p, cb_p, cb_p, cache_p, cache_p)
    y = pl.pallas_call(
        functools.partial(_ffn_down_kernel, sb=sb, lt=lt),
        grid=(ns // sb, nt),
        in_specs=[
            pl.BlockSpec((sb * lt, D_FF_PAD), lambda s, i: (s * nt + i, 0)),
            pl.BlockSpec((sb, lt, D_MODEL), lambda s, i: (s, i, 0)),
            _const_spec((D_FF_PAD, D_MODEL)),
            _const_spec((1, D_MODEL)),
        ],
        out_specs=pl.BlockSpec((sb, lt, D_MODEL), lambda s, i: (s, i, 0)),
        out_shape=jax.ShapeDtypeStruct((ns, length, D_MODEL), f32),
        scratch_shapes=down_scratch,
        compiler_params=_params(("arbitrary", "arbitrary")),
        name="ffn_down",
    )(act, x1, w_down_p, gfin)
    return y, nfv, nfg


def _pad_ff(a, axis):
    pad = [(0, 0)] * a.ndim
    pad[axis] = (0, D_FF_PAD - D_FF)
    if a.shape[axis] == D_FF:
        return jnp.pad(a, pad)
    lo, hi = jnp.split(a, 2, axis=axis)
    return jnp.concatenate([jnp.pad(lo, pad), jnp.pad(hi, pad)], axis=axis)


def _trunk(x, s_re, s_im, sc_buf, ffn_buf, w, cfg):
    ns, length, _ = x.shape
    tm_in, sb_mix, lt_mix, sb_ffn, lt_ffn, lt_up, cb = cfg
    xf = x.reshape(ns * length, D_MODEL)
    u_a, gin, cv, gates = _in_proj(xf, w["norm1_g"], w["w_in"], tm_in)
    ya, n_re, n_im = _s5_mixer(u_a, s_re, s_im, w["s5"], cb)
    x1, h2, new_sc = _mix_out(ya.reshape(N_SLAB, ns, length, LANES), gin, cv, gates,
                              x, sc_buf, w["w_glu"], w["w_sc_out"], w["w_o"],
                              w["sc_conv_w"], w["sc_conv_b"], w["norm2_g"], sb_mix, lt_mix)
    y, nfv, nfg = _ffn(h2, x1, w["final_norm_g"], w["w_up"], w["w_down"],
                       w["ffn_conv_w"], w["ffn_conv_b"], _pad_ff(ffn_buf, 2),
                       sb_ffn, lt_ffn, lt_up)
    new_ffn = jnp.concatenate([nfv[:, -1, :, :D_FF], nfg[:, -1, :, :D_FF]], axis=-1)
    return y, n_re[None], n_im[None], new_sc[None], new_ffn[None]


def kernel(x_prompt, x_sample, state_s5_re, state_s5_im, cache_sc_conv, cache_ffn_conv, norm1_g, w_in, lam_re, lam_im, log_dt, b_re, b_im, c_re, c_im, d_skip, w_glu, sc_conv_w, sc_conv_b, w_sc_out, w_o, norm2_g, w_up, ffn_conv_w, ffn_conv_b, w_down, final_norm_g):
    w = {
        "norm1_g": norm1_g[0].reshape(1, D_MODEL),
        "w_in": w_in[0].astype(bf16),
        "s5": _s5_tables(lam_re[0], lam_im[0], log_dt[0], b_re[0], b_im[0],
                         c_re[0], c_im[0], d_skip[0]),
        "w_glu": w_glu[0].astype(bf16),
        "sc_conv_w": sc_conv_w[0],
        "sc_conv_b": sc_conv_b[0].reshape(1, D_B),
        "w_sc_out": w_sc_out[0].astype(bf16),
        "w_o": w_o[0].astype(bf16),
        "norm2_g": norm2_g[0].reshape(1, D_MODEL),
        "w_up": _pad_ff(w_up[0].astype(bf16), 1),
        "ffn_conv_w": _pad_ff(ffn_conv_w[0], 1),
        "ffn_conv_b": _pad_ff(ffn_conv_b[0].reshape(1, 2 * D_FF), 1),
        "w_down": _pad_ff(w_down[0].astype(bf16), 0),
        "final_norm_g": final_norm_g.reshape(1, D_MODEL),
    }
    bp, lp, _ = x_prompt.shape
    bs, ls, _ = x_sample.shape
    zeros = lambda *s: jnp.zeros(s, f32)
    p_cfg = (256, 1, 256, 1, 256, 512, 64)
    yp, p_re, p_im, p_sc, p_ffn = _trunk(
        x_prompt, zeros(bp, G_A, N_STATE), zeros(bp, G_A, N_STATE),
        zeros(bp, 2, D_B), zeros(bp, 2, 2 * D_FF), w, p_cfg)
    s_cfg = (256, bs // 2, ls, bs // 2, ls, ls, 1)
    ys, s_re, s_im, s_sc, s_ffn = _trunk(
        x_sample, state_s5_re[0], state_s5_im[0], cache_sc_conv[0], cache_ffn_conv[0], w, s_cfg)
    return (yp, ys, p_re, p_im, p_sc, p_ffn, s_re, s_im, s_sc, s_ffn)
```

```python
import functools

import jax
import jax.numpy as jnp
import numpy as np
from jax import lax
from jax.experimental import pallas as pl
from jax.experimental.pallas import tpu as pltpu

D_MODEL = 2048
D_A = D_MODEL // 2
S5_GROUP = 16
G_A = D_A // S5_GROUP
N_STATE = 64
D_B = D_MODEL // 2
D_FF = 5504
EPS = 1e-6
IN_COLS = D_A + 3 * D_B + 2 * D_MODEL

LANES = 128
SUBLANES = 8
MXU_DIM = 256
VMEM_LIMIT_BYTES = 58 * 1024 * 1024

N_SLAB = D_A // LANES
SLAB_G = LANES // S5_GROUP
S5_T = 16
S5_TQ = S5_T * S5_GROUP
S5_K = S5_T * LANES
S5_ST = 2 * N_STATE
S5_SL = SLAB_G * S5_ST
D_FF_PAD = 5632
FFN_HALF = D_FF_PAD // 2
CONV_PAD = SUBLANES

bf16 = jnp.bfloat16
f32 = jnp.float32


def _rmsnorm(x, g):
    ms = jnp.mean(x * x, axis=-1, keepdims=True)
    return x * lax.rsqrt(ms + EPS) * g


def _params(sem):
    return pltpu.CompilerParams(dimension_semantics=sem, vmem_limit_bytes=VMEM_LIMIT_BYTES)


def _const_spec(shape):
    nd = len(shape)
    return pl.BlockSpec(shape, lambda *_: (0,) * nd, pipeline_mode=pl.Buffered(1))


def _dot(a, b):
    return jnp.dot(a, b, preferred_element_type=f32)


def _in_proj_kernel(x_ref, g_ref, w_ref, u_ref, gin_ref, cv_ref, gates_ref):
    h = _rmsnorm(x_ref[...], g_ref[...]).astype(bf16)
    u = _dot(h, w_ref[:, :D_A])
    for k in range(N_SLAB):
        u_ref[k] = u[:, k * LANES:(k + 1) * LANES]
    gin_ref[...] = _dot(h, w_ref[:, D_A:D_A + D_B]).astype(bf16)
    c = _dot(h, w_ref[:, D_A + D_B:D_A + 2 * D_B])
    v = _dot(h, w_ref[:, D_A + 2 * D_B:D_A + 3 * D_B])
    cv_ref[...] = (c * v).astype(bf16)
    g0 = D_A + 3 * D_B
    for n in range(2 * D_MODEL // D_B):
        acc = _dot(h, w_ref[:, g0 + n * D_B:g0 + (n + 1) * D_B])
        gates_ref[:, n * D_B:(n + 1) * D_B] = jax.nn.sigmoid(acc).astype(bf16)


def _in_proj(x, norm_g, w_in_b, tm):
    t = x.shape[0]
    return pl.pallas_call(
        _in_proj_kernel,
        grid=(t // tm,),
        in_specs=[
            pl.BlockSpec((tm, D_MODEL), lambda i: (i, 0)),
            _const_spec((1, D_MODEL)),
            _const_spec((D_MODEL, IN_COLS)),
        ],
        out_specs=[
            pl.BlockSpec((N_SLAB, tm, LANES), lambda i: (0, i, 0)),
            pl.BlockSpec((tm, D_B), lambda i: (i, 0)),
            pl.BlockSpec((tm, D_B), lambda i: (i, 0)),
            pl.BlockSpec((tm, 2 * D_MODEL), lambda i: (i, 0)),
        ],
        out_shape=[
            jax.ShapeDtypeStruct((N_SLAB, t, LANES), f32),
            jax.ShapeDtypeStruct((t, D_B), bf16),
            jax.ShapeDtypeStruct((t, D_B), bf16),
            jax.ShapeDtypeStruct((t, 2 * D_MODEL), bf16),
        ],
        compiler_params=_params(("arbitrary",)),
        name="in_proj",
    )(x, norm_g, w_in_b)


def _group_mask(shape, row0, row_shift, lane_shift):
    r = lax.broadcasted_iota(jnp.int32, shape, 0) + row0
    c = lax.broadcasted_iota(jnp.int32, shape, 1)
    return ((r >> row_shift) & (SLAB_G - 1)) == ((c >> lane_shift) & (SLAB_G - 1))


def _expand_tables(kc_ref, vc_ref, wc_refs, r_ref, m_scr, v_scr, w_scrs):
    rep = r_ref[...]
    bd = _dot(kc_ref[...], rep)
    bd = jnp.where(_group_mask(bd.shape, 0, 4, 4), bd, 0.0).astype(bf16)
    for t in range(S5_T):
        rows = slice(t * LANES, (t + 1) * LANES)
        if t:
            m_scr[rows, :t * LANES] = jnp.zeros((LANES, t * LANES), bf16)
        m_scr[rows, t * LANES:] = bd[:, :(S5_T - t) * LANES]
    step = MXU_DIM
    for r0 in range(0, S5_SL, step):
        vb = _dot(vc_ref[r0:r0 + step, :], rep)
        v_scr[r0:r0 + step, :] = jnp.where(_group_mask(vb.shape, r0, 7, 4), vb, 0.0).astype(bf16)
    for wc_ref, w_scr in zip(wc_refs, w_scrs):
        for r0 in range(0, S5_K, step):
            wt = jnp.concatenate([wc_ref[r0:r0 + step, :].astype(f32)] * SLAB_G, axis=1)
            w_scr[r0:r0 + step, :] = jnp.where(_group_mask(wt.shape, r0, 4, 7), wt, 0.0).astype(bf16)


def _s5_toeplitz(lhs, m_scr):
    return [_dot(lhs[:, :(a + 1) * MXU_DIM],
                 m_scr[:(a + 1) * MXU_DIM, a * MXU_DIM:(a + 1) * MXU_DIM])
            for a in range(S5_K // MXU_DIM)]


def _s5_outputs(toep, sprev, xs, v_scr, d_ref, store):
    per = MXU_DIM // LANES
    for a, part in enumerate(toep):
        y = part + _dot(sprev, v_scr[:, a * MXU_DIM:(a + 1) * MXU_DIM])
        for i in range(per):
            t = a * per + i
            piece = y[:, i * LANES:(i + 1) * LANES] + d_ref[...] * xs[t]
            store(t, jax.nn.gelu(piece))


def _s5_prompt_kernel(u_ref, kc_ref, vc_ref, wc_ref, r_ref, pq_ref, d_ref, h0_ref, h0s_ref,
                      y_ref, sfin_ref,
                      m_scr, v_scr, w_scr, s_scr, t_scr, loc_scr, loct_scr, prev_scr, *, nb, cb):
    ci = pl.program_id(1)

    @pl.when(ci == 0)
    def _():
        _expand_tables(kc_ref, vc_ref, [wc_ref], r_ref, m_scr, v_scr, [w_scr])
        s_scr[...] = h0_ref[...]
        t_scr[...] = h0s_ref[...]
        loc_scr[...] = jnp.zeros_like(loc_scr)
        loct_scr[...] = jnp.zeros_like(loct_scr)

    xs = [jnp.concatenate([u_ref[b, pl.ds(t, cb, stride=S5_T), :] for b in range(nb)], axis=0)
          for t in range(S5_T)]
    lhs = jnp.concatenate(xs, axis=1).astype(bf16)

    loc = _dot(lhs, w_scr[...])
    toep = _s5_toeplitz(lhs, m_scr)
    for j in range(SLAB_G):
        lj = loc[:, j * S5_ST:(j + 1) * S5_ST]
        ljt = pltpu.roll(lj, N_STATE, axis=1)
        for b in range(nb):
            rows = slice(b * cb, (b + 1) * cb)
            loc_scr[j, pl.ds(b, cb, stride=SUBLANES), :] = lj[rows]
            loct_scr[j, pl.ds(b, cb, stride=SUBLANES), :] = ljt[rows]

    ss = [s_scr[j] for j in range(SLAB_G)]
    ts = [t_scr[j] for j in range(SLAB_G)]
    for c in range(cb):
        rows = slice(c * SUBLANES, (c + 1) * SUBLANES)
        for j in range(SLAB_G):
            p = pq_ref[0, j:j + 1, :]
            q = pq_ref[1, j:j + 1, :]
            prev_scr[j, rows, :] = ss[j]
            ss[j], ts[j] = (p * ss[j] + q * ts[j] + loc_scr[j, rows, :],
                            p * ts[j] - q * ss[j] + loct_scr[j, rows, :])
    for j in range(SLAB_G):
        s_scr[j] = ss[j]
        t_scr[j] = ts[j]
        sfin_ref[j] = ss[j]

    sprev = jnp.concatenate(
        [jnp.concatenate([prev_scr[j, pl.ds(b, cb, stride=SUBLANES), :] for b in range(nb)], axis=0)
         for j in range(SLAB_G)], axis=1).astype(bf16)

    def store(t, val):
        for b in range(nb):
            y_ref[b, pl.ds(t, cb, stride=S5_T), :] = val[b * cb:(b + 1) * cb]

    _s5_outputs(toep, sprev, xs, v_scr, d_ref, store)


def _s5_prompt(u, tables, h0, h0s, cb):
    kc, vc, wc, _, rep, pq, dvec = tables
    _, nb, length, _ = u.shape
    nc = length // S5_T
    kern = functools.partial(_s5_prompt_kernel, nb=nb, cb=cb)
    tok = pl.BlockSpec((None, nb, cb * S5_T, LANES), lambda k, ci: (k, 0, ci, 0))
    per_slab = lambda *shape: pl.BlockSpec((None,) + shape, lambda k, ci: (k,) + (0,) * len(shape))
    state = per_slab(SLAB_G, SUBLANES, S5_ST)
    return pl.pallas_call(
        kern,
        grid=(N_SLAB, nc // cb),
        in_specs=[tok, per_slab(LANES, S5_TQ), per_slab(S5_SL, S5_TQ), per_slab(S5_K, S5_ST),
                  _const_spec((S5_TQ, S5_K)),
                  per_slab(2, SLAB_G, S5_ST), per_slab(1, LANES), state, state],
        out_specs=[tok, state],
        out_shape=[
            jax.ShapeDtypeStruct(u.shape, f32),
            jax.ShapeDtypeStruct((N_SLAB, SLAB_G, SUBLANES, S5_ST), f32),
        ],
        scratch_shapes=[
            pltpu.VMEM((S5_K, S5_K), bf16),
            pltpu.VMEM((S5_SL, S5_K), bf16),
            pltpu.VMEM((S5_K, S5_SL), bf16),
            pltpu.VMEM((SLAB_G, SUBLANES, S5_ST), f32),
            pltpu.VMEM((SLAB_G, SUBLANES, S5_ST), f32),
            pltpu.VMEM((SLAB_G, cb * SUBLANES, S5_ST), f32),
            pltpu.VMEM((SLAB_G, cb * SUBLANES, S5_ST), f32),
            pltpu.VMEM((SLAB_G, cb * SUBLANES, S5_ST), f32),
        ],
        compiler_params=_params(("arbitrary", "arbitrary")),
        name="s5_prompt",
    )(u, kc, vc, wc, rep, pq, dvec, h0, h0s)


def _s5_sample_kernel(u_ref, kc_ref, vc_ref, wc_ref, wlc_ref, r_ref, pq_ref, d_ref, h0_ref, h0s_ref,
                      y_ref, sfin_ref, m_scr, v_scr, wh_scr, wl_scr, *, nb):
    _expand_tables(kc_ref, vc_ref, [wc_ref, wlc_ref], r_ref, m_scr, v_scr, [wh_scr, wl_scr])
    xs = [u_ref[pl.ds(t, nb, stride=S5_T), :] for t in range(S5_T)]
    lhs_f = jnp.concatenate(xs, axis=1)
    lhs = lhs_f.astype(bf16)
    lhs_lo = (lhs_f - lhs.astype(f32)).astype(bf16)
    wh = wh_scr[...]
    loc = _dot(lhs, wh) + _dot(lhs_lo, wh) + _dot(lhs, wl_scr[...])
    for j in range(SLAB_G):
        p = pq_ref[0, j:j + 1, :]
        q = pq_ref[1, j:j + 1, :]
        sfin_ref[j] = p * h0_ref[j] + q * h0s_ref[j] + loc[:, j * S5_ST:(j + 1) * S5_ST]
    sprev = jnp.concatenate([h0_ref[j] for j in range(SLAB_G)], axis=1).astype(bf16)

    def store(t, val):
        y_ref[pl.ds(t, nb, stride=S5_T), :] = val

    _s5_outputs(_s5_toeplitz(lhs, m_scr), sprev, xs, v_scr, d_ref, store)


def _s5_sample(u, tables, h0, h0s, nb):
    kc, vc, wc, wlc, rep, pq, dvec = tables
    kern = functools.partial(_s5_sample_kernel, nb=nb)
    per_slab = lambda *shape: pl.BlockSpec((None,) + shape, lambda k: (k,) + (0,) * len(shape))
    tok = per_slab(nb * S5_T, LANES)
    state = per_slab(SLAB_G, nb, S5_ST)
    return pl.pallas_call(
        kern,
        grid=(N_SLAB,),
        in_specs=[tok, per_slab(LANES, S5_TQ), per_slab(S5_SL, S5_TQ), per_slab(S5_K, S5_ST),
                  per_slab(S5_K, S5_ST), _const_spec((S5_TQ, S5_K)),
                  per_slab(2, SLAB_G, S5_ST), per_slab(1, LANES), state, state],
        out_specs=[tok, state],
        out_shape=[
            jax.ShapeDtypeStruct(u.shape, f32),
            jax.ShapeDtypeStruct((N_SLAB, SLAB_G, nb, S5_ST), f32),
        ],
        scratch_shapes=[
            pltpu.VMEM((S5_K, S5_K), bf16),
            pltpu.VMEM((S5_SL, S5_K), bf16),
            pltpu.VMEM((S5_K, S5_SL), bf16),
            pltpu.VMEM((S5_K, S5_SL), bf16),
        ],
        compiler_params=_params(("arbitrary",)),
        name="s5_sample",
    )(u, kc, vc, wc, wlc, rep, pq, dvec, h0, h0s)


def _replication_matrix():
    src = np.arange(S5_TQ)
    dst = np.arange(S5_K)
    same = ((src[:, None] // S5_GROUP == dst[None, :] // LANES)
            & (src[:, None] % S5_GROUP == dst[None, :] % S5_GROUP))
    return jnp.asarray(same, dtype=bf16)


def _s5_tables(lam_re, lam_im, log_dt, b_re, b_im, c_re, c_im, d_skip):
    hp = lax.Precision.HIGHEST
    dt = jnp.exp(log_dt.astype(f32))[:, None]
    lr, li = lam_re.astype(f32), lam_im.astype(f32)
    mag = jnp.exp(lr * dt)
    ab_re = mag * jnp.cos(li * dt)
    ab_im = mag * jnp.sin(li * dt)
    den = lr * lr + li * li
    nr, ni = ab_re - 1.0, ab_im
    f_re = (nr * lr + ni * li) / den
    f_im = (ni * lr - nr * li) / den
    br, bi = b_re.astype(f32), b_im.astype(f32)
    bb_re = f_re[..., None] * br - f_im[..., None] * bi
    bb_im = f_re[..., None] * bi + f_im[..., None] * br
    cr, ci = c_re.astype(f32), c_im.astype(f32)

    k = jnp.arange(S5_T + 1, dtype=f32)[:, None, None]
    pm = jnp.exp(lr * dt * k)
    pr = pm * jnp.cos(li * dt * k)
    pi = pm * jnp.sin(li * dt * k)

    lag = np.arange(S5_TQ) // S5_GROUP
    col = np.arange(S5_TQ) % S5_GROUP
    fill_q = jnp.asarray(np.arange(S5_GROUP)[:, None] == col[None, :], dtype=f32)
    fill_t = jnp.asarray(np.arange(S5_T)[:, None] == lag[None, :], dtype=f32)
    rows = G_A * N_STATE

    def on_lanes(a, fill):
        return jnp.dot(a.reshape(rows, a.shape[-1]), fill, precision=hp)

    ct_re = on_lanes(cr.transpose(0, 2, 1), fill_q)
    ct_im = on_lanes(ci.transpose(0, 2, 1), fill_q)

    def c_times_powers(p_re, p_im):
        pt_re = on_lanes(p_re.transpose(1, 2, 0), fill_t)
        pt_im = on_lanes(p_im.transpose(1, 2, 0), fill_t)
        return ((ct_re * pt_re - ct_im * pt_im).reshape(G_A, N_STATE, S5_TQ),
                (ct_re * pt_im + ct_im * pt_re).reshape(G_A, N_STATE, S5_TQ))

    e_re, e_im = c_times_powers(pr[:S5_T], pi[:S5_T])
    kk = jnp.einsum('gpn,gnx->gpx',
                    jnp.concatenate([bb_re.transpose(0, 2, 1), -bb_im.transpose(0, 2, 1)], axis=2),
                    jnp.concatenate([e_re, e_im], axis=1), precision=hp)
    kc = kk.reshape(N_SLAB, LANES, S5_TQ).astype(bf16)

    def slab_time(a):
        return a.reshape(S5_T, N_SLAB, SLAB_G, 1, N_STATE).transpose(1, 0, 2, 3, 4)

    back = S5_T - 1 - jnp.arange(S5_T)
    prr, pir = slab_time(pr[back]), slab_time(pi[back])
    bt_re = bb_re.transpose(0, 2, 1).reshape(N_SLAB, 1, SLAB_G, S5_GROUP, N_STATE)
    bt_im = bb_im.transpose(0, 2, 1).reshape(N_SLAB, 1, SLAB_G, S5_GROUP, N_STATE)
    lanes = lambda a, b: jnp.concatenate([a, b], axis=-1)
    w = lanes(prr, prr) * lanes(bt_re, bt_im) + lanes(-pir, pir) * lanes(bt_im, bt_re)
    w = w.reshape(N_SLAB, S5_K, S5_ST)
    wc = w.astype(bf16)
    wlc = (w - wc.astype(f32)).astype(bf16)

    v_re, v_im = c_times_powers(pr[1:], pi[1:])
    vc = jnp.concatenate([v_re, -v_im], axis=1).reshape(N_SLAB, S5_SL, S5_TQ).astype(bf16)

    ar, ai = pr[S5_T], pi[S5_T]
    p_row = jnp.concatenate([ar, ar], axis=1).reshape(N_SLAB, 1, SLAB_G, S5_ST)
    q_row = jnp.concatenate([-ai, ai], axis=1).reshape(N_SLAB, 1, SLAB_G, S5_ST)
    pq = jnp.concatenate([p_row, q_row], axis=1)
    dvec = d_skip.astype(f32).reshape(N_SLAB, 1, LANES)
    return kc, vc, wc, wlc, _replication_matrix(), pq, dvec


def _s5_mixer(u, s_re, s_im, tables, cb):
    nb = s_re.shape[0]
    length = u.shape[1] // nb

    def to_slab(a):
        return a.reshape(nb, N_SLAB, SLAB_G, S5_ST).transpose(1, 2, 0, 3)

    h0 = to_slab(jnp.concatenate([s_re, s_im], axis=-1))
    h0s = to_slab(jnp.concatenate([s_im, s_re], axis=-1))
    if length == S5_T:
        y, sfin = _s5_sample(u, tables, h0, h0s, nb)
    else:
        pad = [(0, 0), (0, 0), (0, SUBLANES - nb), (0, 0)]
        y, sfin = _s5_prompt(u.reshape(N_SLAB, nb, length, LANES), tables,
                             jnp.pad(h0, pad), jnp.pad(h0s, pad), cb)
        y = y.reshape(u.shape)
        sfin = sfin[:, :, :nb]
    sfin = sfin.transpose(2, 0, 1, 3).reshape(nb, G_A, S5_ST)
    return y, sfin[..., :N_STATE], sfin[..., N_STATE:]


def _conv3(ext_ref, w_ref, b_ref, lt, cols=slice(None)):
    lo = CONV_PAD - 2
    out = ext_ref[:, lo:lo + lt, cols] * w_ref[0:1, cols]
    out = out + ext_ref[:, lo + 1:lo + 1 + lt, cols] * w_ref[1:2, cols]
    out = out + ext_ref[:, lo + 2:lo + 2 + lt, cols] * w_ref[2:3, cols]
    return out + b_ref[:, cols]


PERM_BLOCK = SUBLANES * SUBLANES


def _swap_rows(val, scr):
    rows, width = val.shape
    nsl = width // LANES
    for k in range(nsl):
        scr[k] = val[:, k * LANES:(k + 1) * LANES]
    return jnp.concatenate(
        [jnp.concatenate([scr[k, pl.ds(r0 + m, SUBLANES, stride=SUBLANES), :]
                          for r0 in range(0, rows, PERM_BLOCK) for m in range(SUBLANES)], axis=0)
         for k in range(nsl)], axis=1)


def _mix_kernel(ya_ref, gin_ref, cv_ref, gates_ref, x_ref, cache_ref,
                wglu_ref, wsc_ref, wo_ref, cw_ref, cb_ref, g2_ref,
                x1_ref, h2_ref, nsc_ref, ext_scr, mrg_scr, *perm_scr, sb, lt, nchunk):
    i = pl.program_id(1)
    rows = sb * lt
    lo = CONV_PAD - 2

    @pl.when(i == 0)
    def _():
        ext_scr[:, lo:CONV_PAD, :] = cache_ref[...]

    @pl.when(i > 0)
    def _():
        ext_scr[:, lo:CONV_PAD, :] = ext_scr[:, lo + lt:CONV_PAD + lt, :]

    ext_scr[:, CONV_PAD:, :] = cv_ref[...].astype(f32).reshape(sb, lt, D_B)
    nsc_ref[...] = ext_scr[:, lo + lt:CONV_PAD + lt, :]

    conv = _conv3(ext_scr, cw_ref, cb_ref, lt).reshape(rows, D_B)
    gated = (gin_ref[...].astype(f32) * conv).astype(bf16)
    ya = jnp.concatenate([ya_ref[k].reshape(rows, LANES) for k in range(N_SLAB)],
                         axis=1).astype(bf16)
    wc = D_MODEL // nchunk
    for n in range(nchunk):
        cols = slice(n * wc, (n + 1) * wc)
        gcols = slice(D_MODEL + n * wc, D_MODEL + (n + 1) * wc)
        a = _dot(ya, wglu_ref[:, cols])
        gt = _dot(ya, wglu_ref[:, gcols])
        br_a = a * jax.nn.sigmoid(gt)
        br_b = _dot(gated, wsc_ref[:, cols])
        g_a = gates_ref[:, cols].astype(f32)
        g_b = gates_ref[:, gcols].astype(f32)
        mrg_scr[:, cols] = (g_a * br_a + g_b * br_b).astype(bf16)
    x1 = x_ref[...].reshape(rows, D_MODEL) + _dot(mrg_scr[...], wo_ref[...])
    x1_ref[...] = x1.reshape(sb, lt, D_MODEL)
    h2 = _rmsnorm(x1, g2_ref[...])
    if perm_scr:
        h2 = _swap_rows(h2, perm_scr[0])
    h2_ref[...] = h2.astype(bf16)


def _swapped_rows(sb, lt):
    return sb == 1 and lt % PERM_BLOCK == 0


def _swap_scratch(rows, width):
    return pltpu.VMEM((width // LANES, rows, LANES), f32)


def _mix_out(ya, gin, cv, gates, x, cache, wglu, wsc, wo, cw, cbias, g2, sb, lt):
    ns, length, _ = x.shape
    nt = length // lt
    kern = functools.partial(_mix_kernel, sb=sb, lt=lt, nchunk=4)
    perm = [_swap_scratch(sb * lt, D_MODEL)] if _swapped_rows(sb, lt) else []
    tile = lambda c: pl.BlockSpec((sb, lt, c), lambda s, i: (s, i, 0))
    flat = lambda c: pl.BlockSpec((sb * lt, c), lambda s, i: (s * nt + i, 0))
    return pl.pallas_call(
        kern,
        grid=(ns // sb, nt),
        in_specs=[
            pl.BlockSpec((N_SLAB, sb, lt, LANES), lambda s, i: (0, s, i, 0)),
            flat(D_B), flat(D_B), flat(2 * D_MODEL), tile(D_MODEL),
            pl.BlockSpec((sb, 2, D_B), lambda s, i: (s, 0, 0)),
            _const_spec((D_A, 2 * D_MODEL)),
            _const_spec((D_B, D_MODEL)),
            _const_spec((D_MODEL, D_MODEL)),
            _const_spec((3, D_B)),
            _const_spec((1, D_B)),
            _const_spec((1, D_MODEL)),
        ],
        out_specs=[
            tile(D_MODEL),
            flat(D_MODEL),
            pl.BlockSpec((sb, 2, D_B), lambda s, i: (s, 0, 0)),
        ],
        out_shape=[
            jax.ShapeDtypeStruct((ns, length, D_MODEL), f32),
            jax.ShapeDtypeStruct((ns * length, D_MODEL), bf16),
            jax.ShapeDtypeStruct((ns, 2, D_B), f32),
        ],
        scratch_shapes=[
            pltpu.VMEM((sb, lt + CONV_PAD, D_B), f32),
            pltpu.VMEM((sb * lt, D_MODEL), bf16),
        ] + perm,
        compiler_params=_params(("arbitrary", "arbitrary")),
        name="mix_out",
    )(ya, gin, cv, gates, x, cache, wglu, wsc, wo, cw, cbias, g2)


def _ffn_up_kernel(h2_ref, wv_ref, wg_ref, cwv_ref, cwg_ref, cbv_ref, cbg_ref,
                   cachev_ref, cacheg_ref, act_ref, nfv_ref, nfg_ref, extv_scr, extg_scr,
                   *, sb, lt):
    i = pl.program_id(2)
    rows = sb * lt
    lo = CONV_PAD - 2

    @pl.when(i == 0)
    def _():
        extv_scr[:, lo:CONV_PAD, :] = cachev_ref[...]
        extg_scr[:, lo:CONV_PAD, :] = cacheg_ref[...]

    @pl.when(i > 0)
    def _():
        extv_scr[:, lo:CONV_PAD, :] = extv_scr[:, lo + lt:CONV_PAD + lt, :]
        extg_scr[:, lo:CONV_PAD, :] = extg_scr[:, lo + lt:CONV_PAD + lt, :]

    h2 = h2_ref[...]
    extv_scr[:, CONV_PAD:, :] = _dot(h2, wv_ref[...]).reshape(sb, lt, FFN_HALF)
    extg_scr[:, CONV_PAD:, :] = _dot(h2, wg_ref[...]).reshape(sb, lt, FFN_HALF)
    for c in range(FFN_HALF // MXU_DIM):
        cols = slice(c * MXU_DIM, (c + 1) * MXU_DIM)
        val = _conv3(extv_scr, cwv_ref, cbv_ref, lt, cols)
        gt = _conv3(extg_scr, cwg_ref, cbg_ref, lt, cols)
        act_ref[:, cols] = (jax.nn.silu(gt) * val).reshape(rows, MXU_DIM).astype(bf16)
    nfv_ref[...] = extv_scr[:, lo + lt:CONV_PAD + lt, :]
    nfg_ref[...] = extg_scr[:, lo + lt:CONV_PAD + lt, :]


def _conv3_swapped(x, car_ref, w_ref, b_ref, cols):
    x6, x7 = x[:, SUBLANES - 2], x[:, SUBLANES - 1]
    prev6 = jnp.concatenate([car_ref[0, :, cols][None], x6[:-1]], axis=0)
    prev7 = jnp.concatenate([car_ref[1, :, cols][None], x7[:-1]], axis=0)
    car_ref[0, :, cols] = x6[-1]
    car_ref[1, :, cols] = x7[-1]
    first = lax.broadcasted_iota(jnp.int32, x6.shape, 1) == 0
    back2 = jnp.where(first, pltpu.roll(prev6, 1, axis=1), pltpu.roll(x6, 1, axis=1))
    back1 = jnp.where(first, pltpu.roll(prev7, 1, axis=1), pltpu.roll(x7, 1, axis=1))
    s1 = jnp.concatenate([back1[:, None], x[:, :SUBLANES - 1]], axis=1)
    s2 = jnp.concatenate([back2[:, None], back1[:, None], x[:, :SUBLANES - 2]], axis=1)
    return (s2 * w_ref[0:1, cols] + s1 * w_ref[1:2, cols] + x * w_ref[2:3, cols]
            + b_ref[:, cols])


def _ffn_up_swapped_kernel(h2_ref, wv_ref, wg_ref, cwv_ref, cwg_ref, cbv_ref, cbg_ref,
                           cachev_ref, cacheg_ref, act_ref, nfv_ref, nfg_ref,
                           carv_scr, carg_scr, *, lt):
    i = pl.program_id(2)
    nblk = lt // PERM_BLOCK

    @pl.when(i == 0)
    def _():
        for car, cache in ((carv_scr, cachev_ref), (carg_scr, cacheg_ref)):
            for r in range(2):
                car[r] = jnp.broadcast_to(cache[0, r:r + 1, :], (SUBLANES, FFN_HALF))

    h2 = h2_ref[...]
    upv = _dot(h2, wv_ref[...])
    upg = _dot(h2, wg_ref[...])
    for c in range(FFN_HALF // MXU_DIM):
        cols = slice(c * MXU_DIM, (c + 1) * MXU_DIM)
        shape = (nblk, SUBLANES, SUBLANES, MXU_DIM)
        val = _conv3_swapped(upv[:, cols].reshape(shape), carv_scr, cwv_ref, cbv_ref, cols)
        gt = _conv3_swapped(upg[:, cols].reshape(shape), carg_scr, cwg_ref, cbg_ref, cols)
        act_ref[:, cols] = (jax.nn.silu(gt) * val).reshape(lt, MXU_DIM).astype(bf16)
    for car, nf_ref in ((carv_scr, nfv_ref), (carg_scr, nfg_ref)):
        for r in range(2):
            nf_ref[0, r:r + 1, :] = car[r, SUBLANES - 1:SUBLANES, :]


def _ffn_down_kernel(act_ref, x1_ref, wd_ref, gf_ref, y_ref, *swap_scr, sb, lt):
    rows = sb * lt
    delta = _dot(act_ref[...], wd_ref[...])
    if swap_scr:
        delta = _swap_rows(delta, swap_scr[0])
    x2 = x1_ref[...].reshape(rows, D_MODEL) + delta
    y_ref[...] = _rmsnorm(x2, gf_ref[...]).reshape(sb, lt, D_MODEL)


def _ffn(h2, x1, gfin, w_up_p, w_down_p, cw_p, cb_p, cache_p, sb, lt, lt_up):
    ns, length, _ = x1.shape
    nh = D_FF_PAD // FFN_HALF
    nt = length // lt
    ntu = length // lt_up
    resident = dict(pipeline_mode=pl.Buffered(1))
    val = lambda rows, **kw: pl.BlockSpec((rows, FFN_HALF), lambda h, s, i: (0, h), **kw)
    gate = lambda rows, **kw: pl.BlockSpec((rows, FFN_HALF), lambda h, s, i: (0, nh + h), **kw)
    swapped = _swapped_rows(sb, lt)
    if swapped:
        assert _swapped_rows(sb, lt_up)
        up_kernel = functools.partial(_ffn_up_swapped_kernel, lt=lt_up)
        up_scratch = [pltpu.VMEM((2, SUBLANES, FFN_HALF), f32)] * 2
        down_scratch = [_swap_scratch(sb * lt, D_MODEL)]
    else:
        up_kernel = functools.partial(_ffn_up_kernel, sb=sb, lt=lt_up)
        up_scratch = [pltpu.VMEM((sb, lt_up + CONV_PAD, FFN_HALF), f32)] * 2
        down_scratch = []
    act, nfv, nfg = pl.pallas_call(
        up_kernel,
        grid=(nh, ns // sb, ntu),
        in_specs=[
            pl.BlockSpec((sb * lt_up, D_MODEL), lambda h, s, i: (s * ntu + i, 0)),
            val(D_MODEL, **resident), gate(D_MODEL, **resident),
            val(3), gate(3), val(1), gate(1),
            pl.BlockSpec((sb, 2, FFN_HALF), lambda h, s, i: (s, 0, h)),
            pl.BlockSpec((sb, 2, FFN_HALF), lambda h, s, i: (s, 0, nh + h)),
        ],
        out_specs=[
            pl.BlockSpec((sb * lt_up, FFN_HALF), lambda h, s, i: (s * ntu + i, h)),
            pl.BlockSpec((sb, None, 2, FFN_HALF), lambda h, s, i: (s, i, 0, h)),
            pl.BlockSpec((sb, None, 2, FFN_HALF), lambda h, s, i: (s, i, 0, h)),
        ],
        out_shape=[
            jax.ShapeDtypeStruct((ns * length, D_FF_PAD), bf16),
            jax.ShapeDtypeStruct((ns, ntu, 2, D_FF_PAD), f32),
            jax.ShapeDtypeStruct((ns, ntu, 2, D_FF_PAD), f32),
        ],
        scratch_shapes=up_scratch,
        compiler_params=_params(("arbitrary", "arbitrary", "arbitrary")),
        name="ffn_up",
    )(h2, w_up_p, w_up_p, cw_p, cw_p, cb_p, cb_p, cache_p, cache_p)
    y = pl.pallas_call(
        functools.partial(_ffn_down_kernel, sb=sb, lt=lt),
        grid=(ns // sb, nt),
        in_specs=[
            pl.BlockSpec((sb * lt, D_FF_PAD), lambda s, i: (s * nt + i, 0)),
            pl.BlockSpec((sb, lt, D_MODEL), lambda s, i: (s, i, 0)),
            _const_spec((D_FF_PAD, D_MODEL)),
            _const_spec((1, D_MODEL)),
        ],
        out_specs=pl.BlockSpec((sb, lt, D_MODEL), lambda s, i: (s, i, 0)),
        out_shape=jax.ShapeDtypeStruct((ns, length, D_MODEL), f32),
        scratch_shapes=down_scratch,
        compiler_params=_params(("arbitrary", "arbitrary")),
        name="ffn_down",
    )(act, x1, w_down_p, gfin)
    return y, nfv, nfg


def _pad_ff(a, axis):
    pad = [(0, 0)] * a.ndim
    pad[axis] = (0, D_FF_PAD - D_FF)
    if a.shape[axis] == D_FF:
        return jnp.pad(a, pad)
    lo, hi = jnp.split(a, 2, axis=axis)
    return jnp.concatenate([jnp.pad(lo, pad), jnp.pad(hi, pad)], axis=axis)


def _trunk(x, s_re, s_im, sc_buf, ffn_buf, w, cfg):
    ns, length, _ = x.shape
    tm_in, sb_mix, lt_mix, sb_ffn, lt_ffn, lt_up, cb = cfg
    xf = x.reshape(ns * length, D_MODEL)
    u_a, gin, cv, gates = _in_proj(xf, w["norm1_g"], w["w_in"], tm_in)
    ya, n_re, n_im = _s5_mixer(u_a, s_re, s_im, w["s5"], cb)
    x1, h2, new_sc = _mix_out(ya.reshape(N_SLAB, ns, length, LANES), gin, cv, gates,
                              x, sc_buf, w["w_glu"], w["w_sc_out"], w["w_o"],
                              w["sc_conv_w"], w["sc_conv_b"], w["norm2_g"], sb_mix, lt_mix)
    y, nfv, nfg = _ffn(h2, x1, w["final_norm_g"], w["w_up"], w["w_down"],
                       w["ffn_conv_w"], w["ffn_conv_b"], _pad_ff(ffn_buf, 2),
                       sb_ffn, lt_ffn, lt_up)
    new_ffn = jnp.concatenate([nfv[:, -1, :, :D_FF], nfg[:, -1, :, :D_FF]], axis=-1)
    return y, n_re[None], n_im[None], new_sc[None], new_ffn[None]


def kernel(x_prompt, x_sample, state_s5_re, state_s5_im, cache_sc_conv, cache_ffn_conv, norm1_g, w_in, lam_re, lam_im, log_dt, b_re, b_im, c_re, c_im, d_skip, w_glu, sc_conv_w, sc_conv_b, w_sc_out, w_o, norm2_g, w_up, ffn_conv_w, ffn_conv_b, w_down, final_norm_g):
    w = {
        "norm1_g": norm1_g[0].reshape(1, D_MODEL),
        "w_in": w_in[0].astype(bf16),
        "s5": _s5_tables(lam_re[0], lam_im[0], log_dt[0], b_re[0], b_im[0],
                         c_re[0], c_im[0], d_skip[0]),
        "w_glu": w_glu[0].astype(bf16),
        "sc_conv_w": sc_conv_w[0],
        "sc_conv_b": sc_conv_b[0].reshape(1, D_B),
        "w_sc_out": w_sc_out[0].astype(bf16),
        "w_o": w_o[0].astype(bf16),
        "norm2_g": norm2_g[0].reshape(1, D_MODEL),
        "w_up": _pad_ff(w_up[0].astype(bf16), 1),
        "ffn_conv_w": _pad_ff(ffn_conv_w[0], 1),
        "ffn_conv_b": _pad_ff(ffn_conv_b[0].reshape(1, 2 * D_FF), 1),
        "w_down": _pad_ff(w_down[0].astype(bf16), 0),
        "final_norm_g": final_norm_g.reshape(1, D_MODEL),
    }
    bp, lp, _ = x_prompt.shape
    bs, ls, _ = x_sample.shape
    zeros = lambda *s: jnp.zeros(s, f32)
    p_cfg = (256, 1, 256, 1, 256, 512, 64)
    yp, p_re, p_im, p_sc, p_ffn = _trunk(
        x_prompt, zeros(bp, G_A, N_STATE), zeros(bp, G_A, N_STATE),
        zeros(bp, 2, D_B), zeros(bp, 2, 2 * D_FF), w, p_cfg)
    s_cfg = (256, bs // 2, ls, bs // 2, ls, ls, 1)
    ys, s_re, s_im, s_sc, s_ffn = _trunk(
        x_sample, state_s5_re[0], state_s5_im[0], cache_sc_conv[0], cache_ffn_conv[0], w, s_cfg)
    return (yp, ys, p_re, p_im, p_sc, p_ffn, s_re, s_im, s_sc, s_ffn)
```

```python
import functools

import jax
import jax.numpy as jnp
import numpy as np
from jax import lax
from jax.experimental import pallas as pl
from jax.experimental.pallas import tpu as pltpu

D_MODEL = 2048
D_A = D_MODEL // 2
S5_GROUP = 16
G_A = D_A // S5_GROUP
N_STATE = 64
D_B = D_MODEL // 2
D_FF = 5504
EPS = 1e-6
IN_COLS = D_A + 3 * D_B + 2 * D_MODEL

LANES = 128
SUBLANES = 8
MXU_DIM = 256
VMEM_LIMIT_BYTES = 58 * 1024 * 1024

N_SLAB = D_A // LANES
SLAB_G = LANES // S5_GROUP
S5_T = 16
S5_TQ = S5_T * S5_GROUP
S5_K = S5_T * LANES
S5_ST = 2 * N_STATE
S5_SL = SLAB_G * S5_ST
D_FF_PAD = 5632
FFN_HALF = D_FF_PAD // 2
CONV_PAD = SUBLANES

bf16 = jnp.bfloat16
f32 = jnp.float32


def _rmsnorm(x, g):
    ms = jnp.mean(x * x, axis=-1, keepdims=True)
    return x * lax.rsqrt(ms + EPS) * g


def _params(sem):
    return pltpu.CompilerParams(dimension_semantics=sem, vmem_limit_bytes=VMEM_LIMIT_BYTES)


def _const_spec(shape):
    nd = len(shape)
    return pl.BlockSpec(shape, lambda *_: (0,) * nd, pipeline_mode=pl.Buffered(1))


def _dot(a, b):
    return jnp.dot(a, b, preferred_element_type=f32)


def _in_proj_kernel(x_ref, g_ref, w_ref, u_ref, gin_ref, cv_ref, gates_ref):
    h = _rmsnorm(x_ref[...], g_ref[...]).astype(bf16)
    u = _dot(h, w_ref[:, :D_A])
    for k in range(N_SLAB):
        u_ref[k] = u[:, k * LANES:(k + 1) * LANES]
    gin_ref[...] = _dot(h, w_ref[:, D_A:D_A + D_B]).astype(bf16)
    c = _dot(h, w_ref[:, D_A + D_B:D_A + 2 * D_B])
    v = _dot(h, w_ref[:, D_A + 2 * D_B:D_A + 3 * D_B])
    cv_ref[...] = (c * v).astype(bf16)
    g0 = D_A + 3 * D_B
    for n in range(2 * D_MODEL // D_B):
        acc = _dot(h, w_ref[:, g0 + n * D_B:g0 + (n + 1) * D_B])
        gates_ref[:, n * D_B:(n + 1) * D_B] = jax.nn.sigmoid(acc).astype(bf16)


def _in_proj(x, norm_g, w_in_b, tm):
    t = x.shape[0]
    return pl.pallas_call(
        _in_proj_kernel,
        grid=(t // tm,),
        in_specs=[
            pl.BlockSpec((tm, D_MODEL), lambda i: (i, 0)),
            _const_spec((1, D_MODEL)),
            _const_spec((D_MODEL, IN_COLS)),
        ],
        out_specs=[
            pl.BlockSpec((N_SLAB, tm, LANES), lambda i: (0, i, 0)),
            pl.BlockSpec((tm, D_B), lambda i: (i, 0)),
            pl.BlockSpec((tm, D_B), lambda i: (i, 0)),
            pl.BlockSpec((tm, 2 * D_MODEL), lambda i: (i, 0)),
        ],
        out_shape=[
            jax.ShapeDtypeStruct((N_SLAB, t, LANES), f32),
            jax.ShapeDtypeStruct((t, D_B), bf16),
            jax.ShapeDtypeStruct((t, D_B), bf16),
            jax.ShapeDtypeStruct((t, 2 * D_MODEL), bf16),
        ],
        compiler_params=_params(("arbitrary",)),
        name="in_proj",
    )(x, norm_g, w_in_b)


def _group_mask(shape, row0, row_shift, lane_shift):
    r = lax.broadcasted_iota(jnp.int32, shape, 0) + row0
    c = lax.broadcasted_iota(jnp.int32, shape, 1)
    return ((r >> row_shift) & (SLAB_G - 1)) == ((c >> lane_shift) & (SLAB_G - 1))


def _expand_tables(kc_ref, vc_ref, wc_refs, r_ref, m_scr, v_scr, w_scrs):
    rep = r_ref[...]
    bd = _dot(kc_ref[...], rep)
    bd = jnp.where(_group_mask(bd.shape, 0, 4, 4), bd, 0.0).astype(bf16)
    for t in range(S5_T):
        rows = slice(t * LANES, (t + 1) * LANES)
        if t:
            m_scr[rows, :t * LANES] = jnp.zeros((LANES, t * LANES), bf16)
        m_scr[rows, t * LANES:] = bd[:, :(S5_T - t) * LANES]
    step = MXU_DIM
    for r0 in range(0, S5_SL, step):
        vb = _dot(vc_ref[r0:r0 + step, :], rep)
        v_scr[r0:r0 + step, :] = jnp.where(_group_mask(vb.shape, r0, 7, 4), vb, 0.0).astype(bf16)
    for wc_ref, w_scr in zip(wc_refs, w_scrs):
        for r0 in range(0, S5_K, step):
            wt = jnp.concatenate([wc_ref[r0:r0 + step, :].astype(f32)] * SLAB_G, axis=1)
            w_scr[r0:r0 + step, :] = jnp.where(_group_mask(wt.shape, r0, 4, 7), wt, 0.0).astype(bf16)


def _s5_toeplitz(lhs, m_scr):
    return [_dot(lhs[:, :(a + 1) * MXU_DIM],
                 m_scr[:(a + 1) * MXU_DIM, a * MXU_DIM:(a + 1) * MXU_DIM])
            for a in range(S5_K // MXU_DIM)]


def _s5_outputs(toep, sprev, xs, v_scr, d_ref, store):
    per = MXU_DIM // LANES
    for a, part in enumerate(toep):
        y = part + _dot(sprev, v_scr[:, a * MXU_DIM:(a + 1) * MXU_DIM])
        for i in range(per):
            t = a * per + i
            piece = y[:, i * LANES:(i + 1) * LANES] + d_ref[...] * xs[t]
            store(t, jax.nn.gelu(piece))


def _s5_prompt_kernel(u_ref, kc_ref, vc_ref, wc_ref, r_ref, pq_ref, d_ref, h0_ref, h0s_ref,
                      y_ref, sfin_ref,
                      m_scr, v_scr, w_scr, s_scr, t_scr, loc_scr, loct_scr, prev_scr, *, nb, cb):
    ci = pl.program_id(1)

    @pl.when(ci == 0)
    def _():
        _expand_tables(kc_ref, vc_ref, [wc_ref], r_ref, m_scr, v_scr, [w_scr])
        s_scr[...] = h0_ref[...]
        t_scr[...] = h0s_ref[...]
        loc_scr[...] = jnp.zeros_like(loc_scr)
        loct_scr[...] = jnp.zeros_like(loct_scr)

    xs = [jnp.concatenate([u_ref[b, pl.ds(t, cb, stride=S5_T), :] for b in range(nb)], axis=0)
          for t in range(S5_T)]
    lhs = jnp.concatenate(xs, axis=1).astype(bf16)

    loc = _dot(lhs, w_scr[...])
    toep = _s5_toeplitz(lhs, m_scr)
    for j in range(SLAB_G):
        lj = loc[:, j * S5_ST:(j + 1) * S5_ST]
        ljt = pltpu.roll(lj, N_STATE, axis=1)
        for b in range(nb):
            rows = slice(b * cb, (b + 1) * cb)
            loc_scr[j, pl.ds(b, cb, stride=SUBLANES), :] = lj[rows]
            loct_scr[j, pl.ds(b, cb, stride=SUBLANES), :] = ljt[rows]

    ss = [s_scr[j] for j in range(SLAB_G)]
    ts = [t_scr[j] for j in range(SLAB_G)]
    for c in range(cb):
        rows = slice(c * SUBLANES, (c + 1) * SUBLANES)
        for j in range(SLAB_G):
            p = pq_ref[0, j:j + 1, :]
            q = pq_ref[1, j:j + 1, :]
            prev_scr[j, rows, :] = ss[j]
            ss[j], ts[j] = (p * ss[j] + q * ts[j] + loc_scr[j, rows, :],
                            p * ts[j] - q * ss[j] + loct_scr[j, rows, :])
    for j in range(SLAB_G):
        s_scr[j] = ss[j]
        t_scr[j] = ts[j]
        sfin_ref[j] = ss[j]

    sprev = jnp.concatenate(
        [jnp.concatenate([prev_scr[j, pl.ds(b, cb, stride=SUBLANES), :] for b in range(nb)], axis=0)
         for j in range(SLAB_G)], axis=1).astype(bf16)

    def store(t, val):
        for b in range(nb):
            y_ref[b, pl.ds(t, cb, stride=S5_T), :] = val[b * cb:(b + 1) * cb]

    _s5_outputs(toep, sprev, xs, v_scr, d_ref, store)


def _s5_prompt(u, tables, h0, h0s, cb):
    kc, vc, wc, _, rep, pq, dvec = tables
    _, nb, length, _ = u.shape
    nc = length // S5_T
    kern = functools.partial(_s5_prompt_kernel, nb=nb, cb=cb)
    tok = pl.BlockSpec((None, nb, cb * S5_T, LANES), lambda k, ci: (k, 0, ci, 0))
    per_slab = lambda *shape: pl.BlockSpec((None,) + shape, lambda k, ci: (k,) + (0,) * len(shape))
    state = per_slab(SLAB_G, SUBLANES, S5_ST)
    return pl.pallas_call(
        kern,
        grid=(N_SLAB, nc // cb),
        in_specs=[tok, per_slab(LANES, S5_TQ), per_slab(S5_SL, S5_TQ), per_slab(S5_K, S5_ST),
                  _const_spec((S5_TQ, S5_K)),
                  per_slab(2, SLAB_G, S5_ST), per_slab(1, LANES), state, state],
        out_specs=[tok, state],
        out_shape=[
            jax.ShapeDtypeStruct(u.shape, f32),
            jax.ShapeDtypeStruct((N_SLAB, SLAB_G, SUBLANES, S5_ST), f32),
        ],
        scratch_shapes=[
            pltpu.VMEM((S5_K, S5_K), bf16),
            pltpu.VMEM((S5_SL, S5_K), bf16),
            pltpu.VMEM((S5_K, S5_SL), bf16),
            pltpu.VMEM((SLAB_G, SUBLANES, S5_ST), f32),
            pltpu.VMEM((SLAB_G, SUBLANES, S5_ST), f32),
            pltpu.VMEM((SLAB_G, cb * SUBLANES, S5_ST), f32),
            pltpu.VMEM((SLAB_G, cb * SUBLANES, S5_ST), f32),
            pltpu.VMEM((SLAB_G, cb * SUBLANES, S5_ST), f32),
        ],
        compiler_params=_params(("arbitrary", "arbitrary")),
        name="s5_prompt",
    )(u, kc, vc, wc, rep, pq, dvec, h0, h0s)


def _s5_sample_kernel(u_ref, kc_ref, vc_ref, wc_ref, wlc_ref, r_ref, pq_ref, d_ref, h0_ref, h0s_ref,
                      y_ref, sfin_ref, m_scr, v_scr, wh_scr, wl_scr, *, nb):
    _expand_tables(kc_ref, vc_ref, [wc_ref, wlc_ref], r_ref, m_scr, v_scr, [wh_scr, wl_scr])
    xs = [u_ref[pl.ds(t, nb, stride=S5_T), :] for t in range(S5_T)]
    lhs_f = jnp.concatenate(xs, axis=1)
    lhs = lhs_f.astype(bf16)
    lhs_lo = (lhs_f - lhs.astype(f32)).astype(bf16)
    wh = wh_scr[...]
    loc = _dot(lhs, wh) + _dot(lhs_lo, wh) + _dot(lhs, wl_scr[...])
    for j in range(SLAB_G):
        p = pq_ref[0, j:j + 1, :]
        q = pq_ref[1, j:j + 1, :]
        sfin_ref[j] = p * h0_ref[j] + q * h0s_ref[j] + loc[:, j * S5_ST:(j + 1) * S5_ST]
    sprev = jnp.concatenate([h0_ref[j] for j in range(SLAB_G)], axis=1).astype(bf16)

    def store(t, val):
        y_ref[pl.ds(t, nb, stride=S5_T), :] = val

    _s5_outputs(_s5_toeplitz(lhs, m_scr), sprev, xs, v_scr, d_ref, store)


def _s5_sample(u, tables, h0, h0s, nb):
    kc, vc, wc, wlc, rep, pq, dvec = tables
    kern = functools.partial(_s5_sample_kernel, nb=nb)
    per_slab = lambda *shape: pl.BlockSpec((None,) + shape, lambda k: (k,) + (0,) * len(shape))
    tok = per_slab(nb * S5_T, LANES)
    state = per_slab(SLAB_G, nb, S5_ST)
    return pl.pallas_call(
        kern,
        grid=(N_SLAB,),
        in_specs=[tok, per_slab(LANES, S5_TQ), per_slab(S5_SL, S5_TQ), per_slab(S5_K, S5_ST),
                  per_slab(S5_K, S5_ST), _const_spec((S5_TQ, S5_K)),
                  per_slab(2, SLAB_G, S5_ST), per_slab(1, LANES), state, state],
        out_specs=[tok, state],
        out_shape=[
            jax.ShapeDtypeStruct(u.shape, f32),
            jax.ShapeDtypeStruct((N_SLAB, SLAB_G, nb, S5_ST), f32),
        ],
        scratch_shapes=[
            pltpu.VMEM((S5_K, S5_K), bf16),
            pltpu.VMEM((S5_SL, S5_K), bf16),
            pltpu.VMEM((S5_K, S5_SL), bf16),
            pltpu.VMEM((S5_K, S5_SL), bf16),
        ],
        compiler_params=_params(("arbitrary",)),
        name="s5_sample",
    )(u, kc, vc, wc, wlc, rep, pq, dvec, h0, h0s)


def _replication_matrix():
    src = np.arange(S5_TQ)
    dst = np.arange(S5_K)
    same = ((src[:, None] // S5_GROUP == dst[None, :] // LANES)
            & (src[:, None] % S5_GROUP == dst[None, :] % S5_GROUP))
    return jnp.asarray(same, dtype=bf16)


def _s5_tables(lam_re, lam_im, log_dt, b_re, b_im, c_re, c_im, d_skip):
    hp = lax.Precision.HIGHEST
    dt = jnp.exp(log_dt.astype(f32))[:, None]
    lr, li = lam_re.astype(f32), lam_im.astype(f32)
    mag = jnp.exp(lr * dt)
    ab_re = mag * jnp.cos(li * dt)
    ab_im = mag * jnp.sin(li * dt)
    den = lr * lr + li * li
    nr, ni = ab_re - 1.0, ab_im
    f_re = (nr * lr + ni * li) / den
    f_im = (ni * lr - nr * li) / den
    br, bi = b_re.astype(f32), b_im.astype(f32)
    bb_re = f_re[..., None] * br - f_im[..., None] * bi
    bb_im = f_re[..., None] * bi + f_im[..., None] * br
    cr, ci = c_re.astype(f32), c_im.astype(f32)

    k = jnp.arange(S5_T + 1, dtype=f32)[:, None, None]
    pm = jnp.exp(lr * dt * k)
    pr = pm * jnp.cos(li * dt * k)
    pi = pm * jnp.sin(li * dt * k)

    lag = np.arange(S5_TQ) // S5_GROUP
    col = np.arange(S5_TQ) % S5_GROUP
    fill_q = jnp.asarray(np.arange(S5_GROUP)[:, None] == col[None, :], dtype=f32)
    fill_t = jnp.asarray(np.arange(S5_T)[:, None] == lag[None, :], dtype=f32)
    rows = G_A * N_STATE

    def on_lanes(a, fill):
        return jnp.dot(a.reshape(rows, a.shape[-1]), fill, precision=hp)

    ct_re = on_lanes(cr.transpose(0, 2, 1), fill_q)
    ct_im = on_lanes(ci.transpose(0, 2, 1), fill_q)

    def c_times_powers(p_re, p_im):
        pt_re = on_lanes(p_re.transpose(1, 2, 0), fill_t)
        pt_im = on_lanes(p_im.transpose(1, 2, 0), fill_t)
        return ((ct_re * pt_re - ct_im * pt_im).reshape(G_A, N_STATE, S5_TQ),
                (ct_re * pt_im + ct_im * pt_re).reshape(G_A, N_STATE, S5_TQ))

    e_re, e_im = c_times_powers(pr[:S5_T], pi[:S5_T])
    kk = jnp.einsum('gpn,gnx->gpx',
                    jnp.concatenate([bb_re.transpose(0, 2, 1), -bb_im.transpose(0, 2, 1)], axis=2),
                    jnp.concatenate([e_re, e_im], axis=1), precision=hp)
    kc = kk.reshape(N_SLAB, LANES, S5_TQ).astype(bf16)

    def slab_time(a):
        return a.reshape(S5_T, N_SLAB, SLAB_G, 1, N_STATE).transpose(1, 0, 2, 3, 4)

    back = S5_T - 1 - jnp.arange(S5_T)
    prr, pir = slab_time(pr[back]), slab_time(pi[back])
    bt_re = bb_re.transpose(0, 2, 1).reshape(N_SLAB, 1, SLAB_G, S5_GROUP, N_STATE)
    bt_im = bb_im.transpose(0, 2, 1).reshape(N_SLAB, 1, SLAB_G, S5_GROUP, N_STATE)
    lanes = lambda a, b: jnp.concatenate([a, b], axis=-1)
    w = lanes(prr, prr) * lanes(bt_re, bt_im) + lanes(-pir, pir) * lanes(bt_im, bt_re)
    w = w.reshape(N_SLAB, S5_K, S5_ST)
    wc = w.astype(bf16)
    wlc = (w - wc.astype(f32)).astype(bf16)

    v_re, v_im = c_times_powers(pr[1:], pi[1:])
    vc = jnp.concatenate([v_re, -v_im], axis=1).reshape(N_SLAB, S5_SL, S5_TQ).astype(bf16)

    ar, ai = pr[S5_T], pi[S5_T]
    p_row = jnp.concatenate([ar, ar], axis=1).reshape(N_SLAB, 1, SLAB_G, S5_ST)
    q_row = jnp.concatenate([-ai, ai], axis=1).reshape(N_SLAB, 1, SLAB_G, S5_ST)
    pq = jnp.concatenate([p_row, q_row], axis=1)
    dvec = d_skip.astype(f32).reshape(N_SLAB, 1, LANES)
    return kc, vc, wc, wlc, _replication_matrix(), pq, dvec


def _s5_mixer(u, s_re, s_im, tables, cb):
    nb = s_re.shape[0]
    length = u.shape[1] // nb

    def to_slab(a):
        return a.reshape(nb, N_SLAB, SLAB_G, S5_ST).transpose(1, 2, 0, 3)

    h0 = to_slab(jnp.concatenate([s_re, s_im], axis=-1))
    h0s = to_slab(jnp.concatenate([s_im, s_re], axis=-1))
    if length == S5_T:
        y, sfin = _s5_sample(u, tables, h0, h0s, nb)
    else:
        pad = [(0, 0), (0, 0), (0, SUBLANES - nb), (0, 0)]
        y, sfin = _s5_prompt(u.reshape(N_SLAB, nb, length, LANES), tables,
                             jnp.pad(h0, pad), jnp.pad(h0s, pad), cb)
        y = y.reshape(u.shape)
        sfin = sfin[:, :, :nb]
    sfin = sfin.transpose(2, 0, 1, 3).reshape(nb, G_A, S5_ST)
    return y, sfin[..., :N_STATE], sfin[..., N_STATE:]


def _conv3(ext_ref, w_ref, b_ref, lt, cols=slice(None)):
    lo = CONV_PAD - 2
    out = ext_ref[:, lo:lo + lt, cols] * w_ref[0:1, cols]
    out = out + ext_ref[:, lo + 1:lo + 1 + lt, cols] * w_ref[1:2, cols]
    out = out + ext_ref[:, lo + 2:lo + 2 + lt, cols] * w_ref[2:3, cols]
    return out + b_ref[:, cols]


PERM_BLOCK = SUBLANES * SUBLANES


def _swap_rows(val, scr):
    rows, width = val.shape
    nsl = width // LANES
    for k in range(nsl):
        scr[k] = val[:, k * LANES:(k + 1) * LANES]
    return jnp.concatenate(
        [jnp.concatenate([scr[k, pl.ds(r0 + m, SUBLANES, stride=SUBLANES), :]
                          for r0 in range(0, rows, PERM_BLOCK) for m in range(SUBLANES)], axis=0)
         for k in range(nsl)], axis=1)


def _mix_kernel(ya_ref, gin_ref, cv_ref, gates_ref, x_ref, cache_ref,
                wglu_ref, wsc_ref, wo_ref, cw_ref, cb_ref, g2_ref,
                x1_ref, h2_ref, nsc_ref, ext_scr, mrg_scr, *perm_scr, sb, lt, nchunk):
    i = pl.program_id(1)
    rows = sb * lt
    lo = CONV_PAD - 2

    @pl.when(i == 0)
    def _():
        ext_scr[:, lo:CONV_PAD, :] = cache_ref[...]

    @pl.when(i > 0)
    def _():
        ext_scr[:, lo:CONV_PAD, :] = ext_scr[:, lo + lt:CONV_PAD + lt, :]

    ext_scr[:, CONV_PAD:, :] = cv_ref[...].astype(f32).reshape(sb, lt, D_B)
    nsc_ref[...] = ext_scr[:, lo + lt:CONV_PAD + lt, :]

    conv = _conv3(ext_scr, cw_ref, cb_ref, lt).reshape(rows, D_B)
    gated = (gin_ref[...].astype(f32) * conv).astype(bf16)
    ya = jnp.concatenate([ya_ref[k].reshape(rows, LANES) for k in range(N_SLAB)],
                         axis=1).astype(bf16)
    wc = D_MODEL // nchunk
    for n in range(nchunk):
        cols = slice(n * wc, (n + 1) * wc)
        gcols = slice(D_MODEL + n * wc, D_MODEL + (n + 1) * wc)
        a = _dot(ya, wglu_ref[:, cols])
        gt = _dot(ya, wglu_ref[:, gcols])
        br_a = a * jax.nn.sigmoid(gt)
        br_b = _dot(gated, wsc_ref[:, cols])
        g_a = gates_ref[:, cols].astype(f32)
        g_b = gates_ref[:, gcols].astype(f32)
        mrg_scr[:, cols] = (g_a * br_a + g_b * br_b).astype(bf16)
    x1 = x_ref[...].reshape(rows, D_MODEL) + _dot(mrg_scr[...], wo_ref[...])
    x1_ref[...] = x1.reshape(sb, lt, D_MODEL)
    h2 = _rmsnorm(x1, g2_ref[...])
    if perm_scr:
        h2 = _swap_rows(h2, perm_scr[0])
    h2_ref[...] = h2.astype(bf16)


def _swapped_rows(sb, lt):
    return sb == 1 and lt % PERM_BLOCK == 0


def _swap_scratch(rows, width):
    return pltpu.VMEM((width // LANES, rows, LANES), f32)


def _mix_out(ya, gin, cv, gates, x, cache, wglu, wsc, wo, cw, cbias, g2, sb, lt):
    ns, length, _ = x.shape
    nt = length // lt
    kern = functools.partial(_mix_kernel, sb=sb, lt=lt, nchunk=4)
    perm = [_swap_scratch(sb * lt, D_MODEL)] if _swapped_rows(sb, lt) else []
    tile = lambda c: pl.BlockSpec((sb, lt, c), lambda s, i: (s, i, 0))
    flat = lambda c: pl.BlockSpec((sb * lt, c), lambda s, i: (s * nt + i, 0))
    return pl.pallas_call(
        kern,
        grid=(ns // sb, nt),
        in_specs=[
            pl.BlockSpec((N_SLAB, sb, lt, LANES), lambda s, i: (0, s, i, 0)),
            flat(D_B), flat(D_B), flat(2 * D_MODEL), tile(D_MODEL),
            pl.BlockSpec((sb, 2, D_B), lambda s, i: (s, 0, 0)),
            _const_spec((D_A, 2 * D_MODEL)),
            _const_spec((D_B, D_MODEL)),
            _const_spec((D_MODEL, D_MODEL)),
            _const_spec((3, D_B)),
            _const_spec((1, D_B)),
            _const_spec((1, D_MODEL)),
        ],
        out_specs=[
            tile(D_MODEL),
            flat(D_MODEL),
            pl.BlockSpec((sb, 2, D_B), lambda s, i: (s, 0, 0)),
        ],
        out_shape=[
            jax.ShapeDtypeStruct((ns, length, D_MODEL), f32),
            jax.ShapeDtypeStruct((ns * length, D_MODEL), bf16),
            jax.ShapeDtypeStruct((ns, 2, D_B), f32),
        ],
        scratch_shapes=[
            pltpu.VMEM((sb, lt + CONV_PAD, D_B), f32),
            pltpu.VMEM((sb * lt, D_MODEL), bf16),
        ] + perm,
        compiler_params=_params(("arbitrary", "arbitrary")),
        name="mix_out",
    )(ya, gin, cv, gates, x, cache, wglu, wsc, wo, cw, cbias, g2)


def _ffn_up_kernel(h2_ref, wv_ref, wg_ref, cwv_ref, cwg_ref, cbv_ref, cbg_ref,
                   cachev_ref, cacheg_ref, act_ref, nfv_ref, nfg_ref, extv_scr, extg_scr,
                   *, sb, lt):
    i = pl.program_id(2)
    rows = sb * lt
    lo = CONV_PAD - 2

    @pl.when(i == 0)
    def _():
        extv_scr[:, lo:CONV_PAD, :] = cachev_ref[...]
        extg_scr[:, lo:CONV_PAD, :] = cacheg_ref[...]

    @pl.when(i > 0)
    def _():
        extv_scr[:, lo:CONV_PAD, :] = extv_scr[:, lo + lt:CONV_PAD + lt, :]
        extg_scr[:, lo:CONV_PAD, :] = extg_scr[:, lo + lt:CONV_PAD + lt, :]

    h2 = h2_ref[...]
    extv_scr[:, CONV_PAD:, :] = _dot(h2, wv_ref[...]).reshape(sb, lt, FFN_HALF)
    extg_scr[:, CONV_PAD:, :] = _dot(h2, wg_ref[...]).reshape(sb, lt, FFN_HALF)
    for c in range(FFN_HALF // MXU_DIM):
        cols = slice(c * MXU_DIM, (c + 1) * MXU_DIM)
        val = _conv3(extv_scr, cwv_ref, cbv_ref, lt, cols)
        gt = _conv3(extg_scr, cwg_ref, cbg_ref, lt, cols)
        act_ref[:, cols] = (jax.nn.silu(gt) * val).reshape(rows, MXU_DIM).astype(bf16)
    nfv_ref[...] = extv_scr[:, lo + lt:CONV_PAD + lt, :]
    nfg_ref[...] = extg_scr[:, lo + lt:CONV_PAD + lt, :]


def _conv3_swapped(x, car_ref, w_ref, b_ref, cols):
    x6, x7 = x[:, SUBLANES - 2], x[:, SUBLANES - 1]
    prev6 = jnp.concatenate([car_ref[0, :, cols][None], x6[:-1]], axis=0)
    prev7 = jnp.concatenate([car_ref[1, :, cols][None], x7[:-1]], axis=0)
    car_ref[0, :, cols] = x6[-1]
    car_ref[1, :, cols] = x7[-1]
    first = lax.broadcasted_iota(jnp.int32, x6.shape, 1) == 0
    back2 = jnp.where(first, pltpu.roll(prev6, 1, axis=1), pltpu.roll(x6, 1, axis=1))
    back1 = jnp.where(first, pltpu.roll(prev7, 1, axis=1), pltpu.roll(x7, 1, axis=1))
    s1 = jnp.concatenate([back1[:, None], x[:, :SUBLANES - 1]], axis=1)
    s2 = jnp.concatenate([back2[:, None], back1[:, None], x[:, :SUBLANES - 2]], axis=1)
    return (s2 * w_ref[0:1, cols] + s1 * w_ref[1:2, cols] + x * w_ref[2:3, cols]
            + b_ref[:, cols])


def _ffn_up_swapped_kernel(h2_ref, wv_ref, wg_ref, cwv_ref, cwg_ref, cbv_ref, cbg_ref,
                           cachev_ref, cacheg_ref, act_ref, nfv_ref, nfg_ref,
                           carv_scr, carg_scr, *, lt):
    i = pl.program_id(2)
    nblk = lt // PERM_BLOCK

    @pl.when(i == 0)
    def _():
        for car, cache in ((carv_scr, cachev_ref), (carg_scr, cacheg_ref)):
            for r in range(2):
                car[r] = jnp.broadcast_to(cache[0, r:r + 1, :], (SUBLANES, FFN_HALF))

    h2 = h2_ref[...]
    upv = _dot(h2, wv_ref[...])
    upg = _dot(h2, wg_ref[...])
    for c in range(FFN_HALF // MXU_DIM):
        cols = slice(c * MXU_DIM, (c + 1) * MXU_DIM)
        shape = (nblk, SUBLANES, SUBLANES, MXU_DIM)
        val = _conv3_swapped(upv[:, cols].reshape(shape), carv_scr, cwv_ref, cbv_ref, cols)
        gt = _conv3_swapped(upg[:, cols].reshape(shape), carg_scr, cwg_ref, cbg_ref, cols)
        act_ref[:, cols] = (jax.nn.silu(gt) * val).reshape(lt, MXU_DIM).astype(bf16)
    for car, nf_ref in ((carv_scr, nfv_ref), (carg_scr, nfg_ref)):
        for r in range(2):
            nf_ref[0, r:r + 1, :] = car[r, SUBLANES - 1:SUBLANES, :]


def _ffn_down_kernel(act_ref, x1_ref, wd_ref, gf_ref, y_ref, *swap_scr, sb, lt):
    rows = sb * lt
    delta = _dot(act_ref[...], wd_ref[...])
    if swap_scr:
        delta = _swap_rows(delta, swap_scr[0])
    x2 = x1_ref[...].reshape(rows, D_MODEL) + delta
    y_ref[...] = _rmsnorm(x2, gf_ref[...]).reshape(sb, lt, D_MODEL)


def _ffn(h2, x1, gfin, w_val_p, w_gate_p, w_down_p, cw_p, cb_p, cache_p, sb, lt, lt_up):
    ns, length, _ = x1.shape
    nh = D_FF_PAD // FFN_HALF
    nt = length // lt
    ntu = length // lt_up
    resident = dict(pipeline_mode=pl.Buffered(1))
    val = lambda rows, **kw: pl.BlockSpec((rows, FFN_HALF), lambda h, s, i: (0, h), **kw)
    gate = lambda rows, **kw: pl.BlockSpec((rows, FFN_HALF), lambda h, s, i: (0, nh + h), **kw)
    swapped = _swapped_rows(sb, lt)
    if swapped:
        assert _swapped_rows(sb, lt_up)
        up_kernel = functools.partial(_ffn_up_swapped_kernel, lt=lt_up)
        up_scratch = [pltpu.VMEM((2, SUBLANES, FFN_HALF), f32)] * 2
        down_scratch = [_swap_scratch(sb * lt, D_MODEL)]
    else:
        up_kernel = functools.partial(_ffn_up_kernel, sb=sb, lt=lt_up)
        up_scratch = [pltpu.VMEM((sb, lt_up + CONV_PAD, FFN_HALF), f32)] * 2
        down_scratch = []
    act, nfv, nfg = pl.pallas_call(
        up_kernel,
        grid=(nh, ns // sb, ntu),
        in_specs=[
            pl.BlockSpec((sb * lt_up, D_MODEL), lambda h, s, i: (s * ntu + i, 0)),
            val(D_MODEL, **resident), val(D_MODEL, **resident),
            val(3), gate(3), val(1), gate(1),
            pl.BlockSpec((sb, 2, FFN_HALF), lambda h, s, i: (s, 0, h)),
            pl.BlockSpec((sb, 2, FFN_HALF), lambda h, s, i: (s, 0, nh + h)),
        ],
        out_specs=[
            pl.BlockSpec((sb * lt_up, FFN_HALF), lambda h, s, i: (s * ntu + i, h)),
            pl.BlockSpec((sb, None, 2, FFN_HALF), lambda h, s, i: (s, i, 0, h)),
            pl.BlockSpec((sb, None, 2, FFN_HALF), lambda h, s, i: (s, i, 0, h)),
        ],
        out_shape=[
            jax.ShapeDtypeStruct((ns * length, D_FF_PAD), bf16),
            jax.ShapeDtypeStruct((ns, ntu, 2, D_FF_PAD), f32),
            jax.ShapeDtypeStruct((ns, ntu, 2, D_FF_PAD), f32),
        ],
        scratch_shapes=up_scratch,
        compiler_params=_params(("arbitrary", "arbitrary", "arbitrary")),
        name="ffn_up",
    )(h2, w_val_p, w_gate_p, cw_p, cw_p, cb_p, cb_p, cache_p, cache_p)
    y = pl.pallas_call(
        functools.partial(_ffn_down_kernel, sb=sb, lt=lt),
        grid=(ns // sb, nt),
        in_specs=[
            pl.BlockSpec((sb * lt, D_FF_PAD), lambda s, i: (s * nt + i, 0)),
            pl.BlockSpec((sb, lt, D_MODEL), lambda s, i: (s, i, 0)),
            _const_spec((D_FF_PAD, D_MODEL)),
            _const_spec((1, D_MODEL)),
        ],
        out_specs=pl.BlockSpec((sb, lt, D_MODEL), lambda s, i: (s, i, 0)),
        out_shape=jax.ShapeDtypeStruct((ns, length, D_MODEL), f32),
        scratch_shapes=down_scratch,
        compiler_params=_params(("arbitrary", "arbitrary")),
        name="ffn_down",
    )(act, x1, w_down_p, gfin)
    return y, nfv, nfg


def _pad_ff(a, axis):
    pad = [(0, 0)] * a.ndim
    pad[axis] = (0, D_FF_PAD - D_FF)
    if a.shape[axis] == D_FF:
        return jnp.pad(a, pad)
    lo, hi = jnp.split(a, 2, axis=axis)
    return jnp.concatenate([jnp.pad(lo, pad), jnp.pad(hi, pad)], axis=axis)


def _trunk(x, s_re, s_im, sc_buf, ffn_buf, w, cfg):
    ns, length, _ = x.shape
    tm_in, sb_mix, lt_mix, sb_ffn, lt_ffn, lt_up, cb = cfg
    xf = x.reshape(ns * length, D_MODEL)
    u_a, gin, cv, gates = _in_proj(xf, w["norm1_g"], w["w_in"], tm_in)
    ya, n_re, n_im = _s5_mixer(u_a, s_re, s_im, w["s5"], cb)
    x1, h2, new_sc = _mix_out(ya.reshape(N_SLAB, ns, length, LANES), gin, cv, gates,
                              x, sc_buf, w["w_glu"], w["w_sc_out"], w["w_o"],
                              w["sc_conv_w"], w["sc_conv_b"], w["norm2_g"], sb_mix, lt_mix)
    y, nfv, nfg = _ffn(h2, x1, w["final_norm_g"], w["w_val"], w["w_gate"], w["w_down"],
                       w["ffn_conv_w"], w["ffn_conv_b"], _pad_ff(ffn_buf, 2),
                       sb_ffn, lt_ffn, lt_up)
    new_ffn = jnp.concatenate([nfv[:, -1, :, :D_FF], nfg[:, -1, :, :D_FF]], axis=-1)
    return y, n_re[None], n_im[None], new_sc[None], new_ffn[None]


def kernel(x_prompt, x_sample, state_s5_re, state_s5_im, cache_sc_conv, cache_ffn_conv, norm1_g, w_in, lam_re, lam_im, log_dt, b_re, b_im, c_re, c_im, d_skip, w_glu, sc_conv_w, sc_conv_b, w_sc_out, w_o, norm2_g, w_up, ffn_conv_w, ffn_conv_b, w_down, final_norm_g):
    w = {
        "norm1_g": norm1_g[0].reshape(1, D_MODEL),
        "w_in": w_in[0].astype(bf16),
        "s5": _s5_tables(lam_re[0], lam_im[0], log_dt[0], b_re[0], b_im[0],
                         c_re[0], c_im[0], d_skip[0]),
        "w_glu": w_glu[0].astype(bf16),
        "sc_conv_w": sc_conv_w[0],
        "sc_conv_b": sc_conv_b[0].reshape(1, D_B),
        "w_sc_out": w_sc_out[0].astype(bf16),
        "w_o": w_o[0].astype(bf16),
        "norm2_g": norm2_g[0].reshape(1, D_MODEL),
        "w_val": _pad_ff(w_up[0][:, :D_FF], 1).astype(bf16),
        "w_gate": _pad_ff(w_up[0][:, D_FF:], 1).astype(bf16),
        "ffn_conv_w": _pad_ff(ffn_conv_w[0], 1),
        "ffn_conv_b": _pad_ff(ffn_conv_b[0].reshape(1, 2 * D_FF), 1),
        "w_down": _pad_ff(w_down[0], 0).astype(bf16),
        "final_norm_g": final_norm_g.reshape(1, D_MODEL),
    }
    bp, lp, _ = x_prompt.shape
    bs, ls, _ = x_sample.shape
    zeros = lambda *s: jnp.zeros(s, f32)
    p_cfg = (256, 1, 256, 1, 256, 512, 64)
    yp, p_re, p_im, p_sc, p_ffn = _trunk(
        x_prompt, zeros(bp, G_A, N_STATE), zeros(bp, G_A, N_STATE),
        zeros(bp, 2, D_B), zeros(bp, 2, 2 * D_FF), w, p_cfg)
    s_cfg = (256, bs // 2, ls, bs // 2, ls, ls, 1)
    ys, s_re, s_im, s_sc, s_ffn = _trunk(
        x_sample, state_s5_re[0], state_s5_im[0], cache_sc_conv[0], cache_ffn_conv[0], w, s_cfg)
    return (yp, ys, p_re, p_im, p_sc, p_ffn, s_re, s_im, s_sc, s_ffn)
```

```python
import functools

import jax
import jax.numpy as jnp
import numpy as np
from jax import lax
from jax.experimental import pallas as pl
from jax.experimental.pallas import tpu as pltpu

D_MODEL = 2048
D_A = D_MODEL // 2
S5_GROUP = 16
G_A = D_A // S5_GROUP
N_STATE = 64
D_B = D_MODEL // 2
D_FF = 5504
EPS = 1e-6
IN_COLS = D_A + 3 * D_B + 2 * D_MODEL

LANES = 128
SUBLANES = 8
MXU_DIM = 256
VMEM_LIMIT_BYTES = 58 * 1024 * 1024

N_SLAB = D_A // LANES
SLAB_G = LANES // S5_GROUP
S5_T = 16
S5_TQ = S5_T * S5_GROUP
S5_K = S5_T * LANES
S5_ST = 2 * N_STATE
S5_SL = SLAB_G * S5_ST
D_FF_PAD = 5632
FFN_HALF = D_FF_PAD // 2
CONV_PAD = SUBLANES

bf16 = jnp.bfloat16
f32 = jnp.float32


def _rmsnorm(x, g):
    ms = jnp.mean(x * x, axis=-1, keepdims=True)
    return x * lax.rsqrt(ms + EPS) * g


def _params(sem):
    return pltpu.CompilerParams(dimension_semantics=sem, vmem_limit_bytes=VMEM_LIMIT_BYTES)


def _const_spec(shape):
    nd = len(shape)
    return pl.BlockSpec(shape, lambda *_: (0,) * nd, pipeline_mode=pl.Buffered(1))


def _dot(a, b):
    return jnp.dot(a, b, preferred_element_type=f32)


def _in_proj_kernel(x_ref, g_ref, w_ref, u_ref, gin_ref, cv_ref, gates_ref):
    h = _rmsnorm(x_ref[...], g_ref[...]).astype(bf16)
    u = _dot(h, w_ref[:, :D_A])
    for k in range(N_SLAB):
        u_ref[k] = u[:, k * LANES:(k + 1) * LANES]
    gin_ref[...] = _dot(h, w_ref[:, D_A:D_A + D_B]).astype(bf16)
    c = _dot(h, w_ref[:, D_A + D_B:D_A + 2 * D_B])
    v = _dot(h, w_ref[:, D_A + 2 * D_B:D_A + 3 * D_B])
    cv_ref[...] = (c * v).astype(bf16)
    g0 = D_A + 3 * D_B
    for n in range(2 * D_MODEL // D_B):
        acc = _dot(h, w_ref[:, g0 + n * D_B:g0 + (n + 1) * D_B])
        gates_ref[:, n * D_B:(n + 1) * D_B] = jax.nn.sigmoid(acc).astype(bf16)


def _in_proj(x, norm_g, w_in_b, tm):
    t = x.shape[0]
    return pl.pallas_call(
        _in_proj_kernel,
        grid=(t // tm,),
        in_specs=[
            pl.BlockSpec((tm, D_MODEL), lambda i: (i, 0)),
            _const_spec((1, D_MODEL)),
            _const_spec((D_MODEL, IN_COLS)),
        ],
        out_specs=[
            pl.BlockSpec((N_SLAB, tm, LANES), lambda i: (0, i, 0)),
            pl.BlockSpec((tm, D_B), lambda i: (i, 0)),
            pl.BlockSpec((tm, D_B), lambda i: (i, 0)),
            pl.BlockSpec((tm, 2 * D_MODEL), lambda i: (i, 0)),
        ],
        out_shape=[
            jax.ShapeDtypeStruct((N_SLAB, t, LANES), f32),
            jax.ShapeDtypeStruct((t, D_B), bf16),
            jax.ShapeDtypeStruct((t, D_B), bf16),
            jax.ShapeDtypeStruct((t, 2 * D_MODEL), bf16),
        ],
        compiler_params=_params(("arbitrary",)),
        name="in_proj",
    )(x, norm_g, w_in_b)


def _group_mask(shape, row0, row_shift, lane_shift):
    r = lax.broadcasted_iota(jnp.int32, shape, 0) + row0
    c = lax.broadcasted_iota(jnp.int32, shape, 1)
    return ((r >> row_shift) & (SLAB_G - 1)) == ((c >> lane_shift) & (SLAB_G - 1))


def _expand_tables(kc_ref, vc_ref, wc_refs, r_ref, m_scr, v_scr, w_scrs):
    rep = r_ref[...]
    bd = _dot(kc_ref[...], rep)
    bd = jnp.where(_group_mask(bd.shape, 0, 4, 4), bd, 0.0).astype(bf16)
    for t in range(S5_T):
        rows = slice(t * LANES, (t + 1) * LANES)
        if t:
            m_scr[rows, :t * LANES] = jnp.zeros((LANES, t * LANES), bf16)
        m_scr[rows, t * LANES:] = bd[:, :(S5_T - t) * LANES]
    step = MXU_DIM
    for r0 in range(0, S5_SL, step):
        vb = _dot(vc_ref[r0:r0 + step, :], rep)
        v_scr[r0:r0 + step, :] = jnp.where(_group_mask(vb.shape, r0, 7, 4), vb, 0.0).astype(bf16)
    for wc_ref, w_scr in zip(wc_refs, w_scrs):
        for r0 in range(0, S5_K, step):
            wt = jnp.concatenate([wc_ref[r0:r0 + step, :].astype(f32)] * SLAB_G, axis=1)
            w_scr[r0:r0 + step, :] = jnp.where(_group_mask(wt.shape, r0, 4, 7), wt, 0.0).astype(bf16)


def _s5_toeplitz(lhs, m_scr):
    return [_dot(lhs[:, :(a + 1) * MXU_DIM],
                 m_scr[:(a + 1) * MXU_DIM, a * MXU_DIM:(a + 1) * MXU_DIM])
            for a in range(S5_K // MXU_DIM)]


def _s5_outputs(toep, sprev, xs, v_scr, d_ref, store):
    per = MXU_DIM // LANES
    for a, part in enumerate(toep):
        y = part + _dot(sprev, v_scr[:, a * MXU_DIM:(a + 1) * MXU_DIM])
        for i in range(per):
            t = a * per + i
            piece = y[:, i * LANES:(i + 1) * LANES] + d_ref[...] * xs[t]
            store(t, jax.nn.gelu(piece))


def _s5_prompt_kernel(u_ref, kc_ref, vc_ref, wc_ref, r_ref, pq_ref, d_ref, h0_ref, h0s_ref,
                      y_ref, sfin_ref,
                      m_scr, v_scr, w_scr, s_scr, t_scr, loc_scr, loct_scr, prev_scr, *, nb, cb):
    ci = pl.program_id(1)

    @pl.when(ci == 0)
    def _():
        _expand_tables(kc_ref, vc_ref, [wc_ref], r_ref, m_scr, v_scr, [w_scr])
        s_scr[...] = h0_ref[...]
        t_scr[...] = h0s_ref[...]
        loc_scr[...] = jnp.zeros_like(loc_scr)
        loct_scr[...] = jnp.zeros_like(loct_scr)

    xs = [jnp.concatenate([u_ref[b, pl.ds(t, cb, stride=S5_T), :] for b in range(nb)], axis=0)
          for t in range(S5_T)]
    lhs = jnp.concatenate(xs, axis=1).astype(bf16)

    loc = _dot(lhs, w_scr[...])
    toep = _s5_toeplitz(lhs, m_scr)
    for j in range(SLAB_G):
        lj = loc[:, j * S5_ST:(j + 1) * S5_ST]
        ljt = pltpu.roll(lj, N_STATE, axis=1)
        for b in range(nb):
            rows = slice(b * cb, (b + 1) * cb)
            loc_scr[j, pl.ds(b, cb, stride=SUBLANES), :] = lj[rows]
            loct_scr[j, pl.ds(b, cb, stride=SUBLANES), :] = ljt[rows]

    ss = [s_scr[j] for j in range(SLAB_G)]
    ts = [t_scr[j] for j in range(SLAB_G)]
    for c in range(cb):
        rows = slice(c * SUBLANES, (c + 1) * SUBLANES)
        for j in range(SLAB_G):
            p = pq_ref[0, j:j + 1, :]
            q = pq_ref[1, j:j + 1, :]
            prev_scr[j, rows, :] = ss[j]
            ss[j], ts[j] = (p * ss[j] + q * ts[j] + loc_scr[j, rows, :],
                            p * ts[j] - q * ss[j] + loct_scr[j, rows, :])
    for j in range(SLAB_G):
        s_scr[j] = ss[j]
        t_scr[j] = ts[j]
        sfin_ref[j] = ss[j]

    sprev = jnp.concatenate(
        [jnp.concatenate([prev_scr[j, pl.ds(b, cb, stride=SUBLANES), :] for b in range(nb)], axis=0)
         for j in range(SLAB_G)], axis=1).astype(bf16)

    def store(t, val):
        for b in range(nb):
            y_ref[b, pl.ds(t, cb, stride=S5_T), :] = val[b * cb:(b + 1) * cb]

    _s5_outputs(toep, sprev, xs, v_scr, d_ref, store)


def _s5_prompt(u, tables, h0, h0s, cb):
    kc, vc, wc, _, rep, pq, dvec = tables
    _, nb, length, _ = u.shape
    nc = length // S5_T
    kern = functools.partial(_s5_prompt_kernel, nb=nb, cb=cb)
    tok = pl.BlockSpec((None, nb, cb * S5_T, LANES), lambda k, ci: (k, 0, ci, 0))
    per_slab = lambda *shape: pl.BlockSpec((None,) + shape, lambda k, ci: (k,) + (0,) * len(shape))
    state = per_slab(SLAB_G, SUBLANES, S5_ST)
    return pl.pallas_call(
        kern,
        grid=(N_SLAB, nc // cb),
        in_specs=[tok, per_slab(LANES, S5_TQ), per_slab(S5_SL, S5_TQ), per_slab(S5_K, S5_ST),
                  _const_spec((S5_TQ, S5_K)),
                  per_slab(2, SLAB_G, S5_ST), per_slab(1, LANES), state, state],
        out_specs=[tok, state],
        out_shape=[
            jax.ShapeDtypeStruct(u.shape, f32),
            jax.ShapeDtypeStruct((N_SLAB, SLAB_G, SUBLANES, S5_ST), f32),
        ],
        scratch_shapes=[
            pltpu.VMEM((S5_K, S5_K), bf16),
            pltpu.VMEM((S5_SL, S5_K), bf16),
            pltpu.VMEM((S5_K, S5_SL), bf16),
            pltpu.VMEM((SLAB_G, SUBLANES, S5_ST), f32),
            pltpu.VMEM((SLAB_G, SUBLANES, S5_ST), f32),
            pltpu.VMEM((SLAB_G, cb * SUBLANES, S5_ST), f32),
            pltpu.VMEM((SLAB_G, cb * SUBLANES, S5_ST), f32),
            pltpu.VMEM((SLAB_G, cb * SUBLANES, S5_ST), f32),
        ],
        compiler_params=_params(("arbitrary", "arbitrary")),
        name="s5_prompt",
    )(u, kc, vc, wc, rep, pq, dvec, h0, h0s)


def _s5_sample_kernel(u_ref, kc_ref, vc_ref, wc_ref, wlc_ref, r_ref, pq_ref, d_ref, h0_ref, h0s_ref,
                      y_ref, sfin_ref, m_scr, v_scr, wh_scr, wl_scr, *, nb):
    _expand_tables(kc_ref, vc_ref, [wc_ref, wlc_ref], r_ref, m_scr, v_scr, [wh_scr, wl_scr])
    xs = [u_ref[pl.ds(t, nb, stride=S5_T), :] for t in range(S5_T)]
    lhs_f = jnp.concatenate(xs, axis=1)
    lhs = lhs_f.astype(bf16)
    lhs_lo = (lhs_f - lhs.astype(f32)).astype(bf16)
    wh = wh_scr[...]
    loc = _dot(lhs, wh) + _dot(lhs_lo, wh) + _dot(lhs, wl_scr[...])
    for j in range(SLAB_G):
        p = pq_ref[0, j:j + 1, :]
        q = pq_ref[1, j:j + 1, :]
        sfin_ref[j] = p * h0_ref[j] + q * h0s_ref[j] + loc[:, j * S5_ST:(j + 1) * S5_ST]
    sprev = jnp.concatenate([h0_ref[j] for j in range(SLAB_G)], axis=1).astype(bf16)

    def store(t, val):
        y_ref[pl.ds(t, nb, stride=S5_T), :] = val

    _s5_outputs(_s5_toeplitz(lhs, m_scr), sprev, xs, v_scr, d_ref, store)


def _s5_sample(u, tables, h0, h0s, nb):
    kc, vc, wc, wlc, rep, pq, dvec = tables
    kern = functools.partial(_s5_sample_kernel, nb=nb)
    per_slab = lambda *shape: pl.BlockSpec((None,) + shape, lambda k: (k,) + (0,) * len(shape))
    tok = per_slab(nb * S5_T, LANES)
    state = per_slab(SLAB_G, nb, S5_ST)
    return pl.pallas_call(
        kern,
        grid=(N_SLAB,),
        in_specs=[tok, per_slab(LANES, S5_TQ), per_slab(S5_SL, S5_TQ), per_slab(S5_K, S5_ST),
                  per_slab(S5_K, S5_ST), _const_spec((S5_TQ, S5_K)),
                  per_slab(2, SLAB_G, S5_ST), per_slab(1, LANES), state, state],
        out_specs=[tok, state],
        out_shape=[
            jax.ShapeDtypeStruct(u.shape, f32),
            jax.ShapeDtypeStruct((N_SLAB, SLAB_G, nb, S5_ST), f32),
        ],
        scratch_shapes=[
            pltpu.VMEM((S5_K, S5_K), bf16),
            pltpu.VMEM((S5_SL, S5_K), bf16),
            pltpu.VMEM((S5_K, S5_SL), bf16),
            pltpu.VMEM((S5_K, S5_SL), bf16),
        ],
        compiler_params=_params(("arbitrary",)),
        name="s5_sample",
    )(u, kc, vc, wc, wlc, rep, pq, dvec, h0, h0s)


def _replication_matrix():
    src = np.arange(S5_TQ)
    dst = np.arange(S5_K)
    same = ((src[:, None] // S5_GROUP == dst[None, :] // LANES)
            & (src[:, None] % S5_GROUP == dst[None, :] % S5_GROUP))
    return jnp.asarray(same, dtype=bf16)


def _s5_tables(lam_re, lam_im, log_dt, b_re, b_im, c_re, c_im, d_skip):
    hp = lax.Precision.HIGHEST
    dt = jnp.exp(log_dt.astype(f32))[:, None]
    lr, li = lam_re.astype(f32), lam_im.astype(f32)
    mag = jnp.exp(lr * dt)
    ab_re = mag * jnp.cos(li * dt)
    ab_im = mag * jnp.sin(li * dt)
    den = lr * lr + li * li
    nr, ni = ab_re - 1.0, ab_im
    f_re = (nr * lr + ni * li) / den
    f_im = (ni * lr - nr * li) / den
    br, bi = b_re.astype(f32), b_im.astype(f32)
    bb_re = f_re[..., None] * br - f_im[..., None] * bi
    bb_im = f_re[..., None] * bi + f_im[..., None] * br
    cr, ci = c_re.astype(f32), c_im.astype(f32)

    k = jnp.arange(S5_T + 1, dtype=f32)[:, None, None]
    pm = jnp.exp(lr * dt * k)
    pr = pm * jnp.cos(li * dt * k)
    pi = pm * jnp.sin(li * dt * k)

    lag = np.arange(S5_TQ) // S5_GROUP
    col = np.arange(S5_TQ) % S5_GROUP
    fill_q = jnp.asarray(np.arange(S5_GROUP)[:, None] == col[None, :], dtype=f32)
    fill_t = jnp.asarray(np.arange(S5_T)[:, None] == lag[None, :], dtype=f32)
    rows = G_A * N_STATE

    def on_lanes(a, fill):
        return jnp.dot(a.reshape(rows, a.shape[-1]), fill, precision=hp)

    ct_re = on_lanes(cr.transpose(0, 2, 1), fill_q)
    ct_im = on_lanes(ci.transpose(0, 2, 1), fill_q)

    def c_times_powers(p_re, p_im):
        pt_re = on_lanes(p_re.transpose(1, 2, 0), fill_t)
        pt_im = on_lanes(p_im.transpose(1, 2, 0), fill_t)
        return ((ct_re * pt_re - ct_im * pt_im).reshape(G_A, N_STATE, S5_TQ),
                (ct_re * pt_im + ct_im * pt_re).reshape(G_A, N_STATE, S5_TQ))

    e_re, e_im = c_times_powers(pr[:S5_T], pi[:S5_T])
    kk = jnp.einsum('gpn,gnx->gpx',
                    jnp.concatenate([bb_re.transpose(0, 2, 1), -bb_im.transpose(0, 2, 1)], axis=2),
                    jnp.concatenate([e_re, e_im], axis=1), precision=hp)
    kc = kk.reshape(N_SLAB, LANES, S5_TQ).astype(bf16)

    def slab_time(a):
        return a.reshape(S5_T, N_SLAB, SLAB_G, 1, N_STATE).transpose(1, 0, 2, 3, 4)

    back = S5_T - 1 - jnp.arange(S5_T)
    prr, pir = slab_time(pr[back]), slab_time(pi[back])
    bt_re = bb_re.transpose(0, 2, 1).reshape(N_SLAB, 1, SLAB_G, S5_GROUP, N_STATE)
    bt_im = bb_im.transpose(0, 2, 1).reshape(N_SLAB, 1, SLAB_G, S5_GROUP, N_STATE)
    lanes = lambda a, b: jnp.concatenate([a, b], axis=-1)
    w = lanes(prr, prr) * lanes(bt_re, bt_im) + lanes(-pir, pir) * lanes(bt_im, bt_re)
    w = w.reshape(N_SLAB, S5_K, S5_ST)
    wc = w.astype(bf16)
    wlc = (w - wc.astype(f32)).astype(bf16)

    v_re, v_im = c_times_powers(pr[1:], pi[1:])
    vc = jnp.concatenate([v_re, -v_im], axis=1).reshape(N_SLAB, S5_SL, S5_TQ).astype(bf16)

    ar, ai = pr[S5_T], pi[S5_T]
    p_row = jnp.concatenate([ar, ar], axis=1).reshape(N_SLAB, 1, SLAB_G, S5_ST)
    q_row = jnp.concatenate([-ai, ai], axis=1).reshape(N_SLAB, 1, SLAB_G, S5_ST)
    pq = jnp.concatenate([p_row, q_row], axis=1)
    dvec = d_skip.astype(f32).reshape(N_SLAB, 1, LANES)
    return kc, vc, wc, wlc, _replication_matrix(), pq, dvec


def _s5_mixer(u, s_re, s_im, tables, cb):
    nb = s_re.shape[0]
    length = u.shape[1] // nb

    def to_slab(a):
        return a.reshape(nb, N_SLAB, SLAB_G, S5_ST).transpose(1, 2, 0, 3)

    h0 = to_slab(jnp.concatenate([s_re, s_im], axis=-1))
    h0s = to_slab(jnp.concatenate([s_im, s_re], axis=-1))
    if length == S5_T:
        y, sfin = _s5_sample(u, tables, h0, h0s, nb)
    else:
        pad = [(0, 0), (0, 0), (0, SUBLANES - nb), (0, 0)]
        y, sfin = _s5_prompt(u.reshape(N_SLAB, nb, length, LANES), tables,
                             jnp.pad(h0, pad), jnp.pad(h0s, pad), cb)
        y = y.reshape(u.shape)
        sfin = sfin[:, :, :nb]
    sfin = sfin.transpose(2, 0, 1, 3).reshape(nb, G_A, S5_ST)
    return y, sfin[..., :N_STATE], sfin[..., N_STATE:]


def _conv3(ext_ref, w_ref, b_ref, lt, cols=slice(None)):
    lo = CONV_PAD - 2
    out = ext_ref[:, lo:lo + lt, cols] * w_ref[0:1, cols]
    out = out + ext_ref[:, lo + 1:lo + 1 + lt, cols] * w_ref[1:2, cols]
    out = out + ext_ref[:, lo + 2:lo + 2 + lt, cols] * w_ref[2:3, cols]
    return out + b_ref[:, cols]


PERM_BLOCK = SUBLANES * SUBLANES


def _swap_rows(val, scr):
    rows, width = val.shape
    nsl = width // LANES
    for k in range(nsl):
        scr[k] = val[:, k * LANES:(k + 1) * LANES]
    return jnp.concatenate(
        [jnp.concatenate([scr[k, pl.ds(r0 + m, SUBLANES, stride=SUBLANES), :]
                          for r0 in range(0, rows, PERM_BLOCK) for m in range(SUBLANES)], axis=0)
         for k in range(nsl)], axis=1)


def _mix_kernel(ya_ref, gin_ref, cv_ref, gates_ref, x_ref, cache_ref,
                wglu_ref, wsc_ref, wo_ref, cw_ref, cb_ref, g2_ref,
                x1_ref, h2_ref, nsc_ref, ext_scr, mrg_scr, *perm_scr, sb, lt, nchunk):
    i = pl.program_id(1)
    rows = sb * lt
    lo = CONV_PAD - 2

    @pl.when(i == 0)
    def _():
        ext_scr[:, lo:CONV_PAD, :] = cache_ref[...]

    @pl.when(i > 0)
    def _():
        ext_scr[:, lo:CONV_PAD, :] = ext_scr[:, lo + lt:CONV_PAD + lt, :]

    ext_scr[:, CONV_PAD:, :] = cv_ref[...].astype(f32).reshape(sb, lt, D_B)
    nsc_ref[...] = ext_scr[:, lo + lt:CONV_PAD + lt, :]

    conv = _conv3(ext_scr, cw_ref, cb_ref, lt).reshape(rows, D_B)
    gated = (gin_ref[...].astype(f32) * conv).astype(bf16)
    ya = jnp.concatenate([ya_ref[k].reshape(rows, LANES) for k in range(N_SLAB)],
                         axis=1).astype(bf16)
    wc = D_MODEL // nchunk
    for n in range(nchunk):
        cols = slice(n * wc, (n + 1) * wc)
        gcols = slice(D_MODEL + n * wc, D_MODEL + (n + 1) * wc)
        a = _dot(ya, wglu_ref[:, cols])
        gt = _dot(ya, wglu_ref[:, gcols])
        br_a = a * jax.nn.sigmoid(gt)
        br_b = _dot(gated, wsc_ref[:, cols])
        g_a = gates_ref[:, cols].astype(f32)
        g_b = gates_ref[:, gcols].astype(f32)
        mrg_scr[:, cols] = (g_a * br_a + g_b * br_b).astype(bf16)
    x1 = x_ref[...].reshape(rows, D_MODEL) + _dot(mrg_scr[...], wo_ref[...])
    x1_ref[...] = x1.reshape(sb, lt, D_MODEL)
    h2 = _rmsnorm(x1, g2_ref[...])
    if perm_scr:
        h2 = _swap_rows(h2, perm_scr[0])
    h2_ref[...] = h2.astype(bf16)


def _swapped_rows(sb, lt):
    return sb == 1 and lt % PERM_BLOCK == 0


def _swap_scratch(rows, width):
    return pltpu.VMEM((width // LANES, rows, LANES), f32)


def _mix_out(ya, gin, cv, gates, x, cache, wglu, wsc, wo, cw, cbias, g2, sb, lt):
    ns, length, _ = x.shape
    nt = length // lt
    kern = functools.partial(_mix_kernel, sb=sb, lt=lt, nchunk=4)
    perm = [_swap_scratch(sb * lt, D_MODEL)] if _swapped_rows(sb, lt) else []
    tile = lambda c: pl.BlockSpec((sb, lt, c), lambda s, i: (s, i, 0))
    flat = lambda c: pl.BlockSpec((sb * lt, c), lambda s, i: (s * nt + i, 0))
    return pl.pallas_call(
        kern,
        grid=(ns // sb, nt),
        in_specs=[
            pl.BlockSpec((N_SLAB, sb, lt, LANES), lambda s, i: (0, s, i, 0)),
            flat(D_B), flat(D_B), flat(2 * D_MODEL), tile(D_MODEL),
            pl.BlockSpec((sb, 2, D_B), lambda s, i: (s, 0, 0)),
            _const_spec((D_A, 2 * D_MODEL)),
            _const_spec((D_B, D_MODEL)),
            _const_spec((D_MODEL, D_MODEL)),
            _const_spec((3, D_B)),
            _const_spec((1, D_B)),
            _const_spec((1, D_MODEL)),
        ],
        out_specs=[
            tile(D_MODEL),
            flat(D_MODEL),
            pl.BlockSpec((sb, 2, D_B), lambda s, i: (s, 0, 0)),
        ],
        out_shape=[
            jax.ShapeDtypeStruct((ns, length, D_MODEL), f32),
            jax.ShapeDtypeStruct((ns * length, D_MODEL), bf16),
            jax.ShapeDtypeStruct((ns, 2, D_B), f32),
        ],
        scratch_shapes=[
            pltpu.VMEM((sb, lt + CONV_PAD, D_B), f32),
            pltpu.VMEM((sb * lt, D_MODEL), bf16),
        ] + perm,
        compiler_params=_params(("arbitrary", "arbitrary")),
        name="mix_out",
    )(ya, gin, cv, gates, x, cache, wglu, wsc, wo, cw, cbias, g2)


def _ffn_up_kernel(h2_ref, wv_ref, wg_ref, cwv_ref, cwg_ref, cbv_ref, cbg_ref,
                   cachev_ref, cacheg_ref, act_ref, nfv_ref, nfg_ref, extv_scr, extg_scr,
                   *, sb, lt):
    i = pl.program_id(2)
    rows = sb * lt
    lo = CONV_PAD - 2

    @pl.when(i == 0)
    def _():
        extv_scr[:, lo:CONV_PAD, :] = cachev_ref[...]
        extg_scr[:, lo:CONV_PAD, :] = cacheg_ref[...]

    @pl.when(i > 0)
    def _():
        extv_scr[:, lo:CONV_PAD, :] = extv_scr[:, lo + lt:CONV_PAD + lt, :]
        extg_scr[:, lo:CONV_PAD, :] = extg_scr[:, lo + lt:CONV_PAD + lt, :]

    h2 = h2_ref[...]
    extv_scr[:, CONV_PAD:, :] = _dot(h2, wv_ref[...]).reshape(sb, lt, FFN_HALF)
    extg_scr[:, CONV_PAD:, :] = _dot(h2, wg_ref[...]).reshape(sb, lt, FFN_HALF)
    for c in range(FFN_HALF // MXU_DIM):
        cols = slice(c * MXU_DIM, (c + 1) * MXU_DIM)
        val = _conv3(extv_scr, cwv_ref, cbv_ref, lt, cols)
        gt = _conv3(extg_scr, cwg_ref, cbg_ref, lt, cols)
        act_ref[:, cols] = (jax.nn.silu(gt) * val).reshape(rows, MXU_DIM).astype(bf16)
    nfv_ref[...] = extv_scr[:, lo + lt:CONV_PAD + lt, :]
    nfg_ref[...] = extg_scr[:, lo + lt:CONV_PAD + lt, :]


def _conv3_swapped(x, car_ref, w_ref, b_ref, cols):
    x6, x7 = x[:, SUBLANES - 2], x[:, SUBLANES - 1]
    prev6 = jnp.concatenate([car_ref[0, :, cols][None], x6[:-1]], axis=0)
    prev7 = jnp.concatenate([car_ref[1, :, cols][None], x7[:-1]], axis=0)
    car_ref[0, :, cols] = x6[-1]
    car_ref[1, :, cols] = x7[-1]
    first = lax.broadcasted_iota(jnp.int32, x6.shape, 1) == 0
    back2 = jnp.where(first, pltpu.roll(prev6, 1, axis=1), pltpu.roll(x6, 1, axis=1))
    back1 = jnp.where(first, pltpu.roll(prev7, 1, axis=1), pltpu.roll(x7, 1, axis=1))
    s1 = jnp.concatenate([back1[:, None], x[:, :SUBLANES - 1]], axis=1)
    s2 = jnp.concatenate([back2[:, None], back1[:, None], x[:, :SUBLANES - 2]], axis=1)
    return (s2 * w_ref[0:1, cols] + s1 * w_ref[1:2, cols] + x * w_ref[2:3, cols]
            + b_ref[:, cols])


def _ffn_up_swapped_kernel(h2_ref, wv_ref, wg_ref, cwv_ref, cwg_ref, cbv_ref, cbg_ref,
                           cachev_ref, cacheg_ref, act_ref, nfv_ref, nfg_ref,
                           carv_scr, carg_scr, *, lt):
    i = pl.program_id(2)
    nblk = lt // PERM_BLOCK

    @pl.when(i == 0)
    def _():
        for car, cache in ((carv_scr, cachev_ref), (carg_scr, cacheg_ref)):
            for r in range(2):
                car[r] = jnp.broadcast_to(cache[0, r:r + 1, :], (SUBLANES, FFN_HALF))

    h2 = h2_ref[...]
    upv = _dot(h2, wv_ref[...])
    upg = _dot(h2, wg_ref[...])
    for c in range(FFN_HALF // MXU_DIM):
        cols = slice(c * MXU_DIM, (c + 1) * MXU_DIM)
        shape = (nblk, SUBLANES, SUBLANES, MXU_DIM)
        val = _conv3_swapped(upv[:, cols].reshape(shape), carv_scr, cwv_ref, cbv_ref, cols)
        gt = _conv3_swapped(upg[:, cols].reshape(shape), carg_scr, cwg_ref, cbg_ref, cols)
        act_ref[:, cols] = (jax.nn.silu(gt) * val).reshape(lt, MXU_DIM).astype(bf16)
    for car, nf_ref in ((carv_scr, nfv_ref), (carg_scr, nfg_ref)):
        for r in range(2):
            nf_ref[0, r:r + 1, :] = car[r, SUBLANES - 1:SUBLANES, :]


def _ffn_down_kernel(act_ref, x1_ref, wd_ref, gf_ref, y_ref, *swap_scr, sb, lt):
    rows = sb * lt
    delta = _dot(act_ref[...], wd_ref[...])
    if swap_scr:
        delta = _swap_rows(delta, swap_scr[0])
    x2 = x1_ref[...].reshape(rows, D_MODEL) + delta
    y_ref[...] = _rmsnorm(x2, gf_ref[...]).reshape(sb, lt, D_MODEL)


def _ffn(h2, x1, gfin, w_val_p, w_gate_p, w_down_p, cw_p, cb_p, cache_p, sb, lt, lt_up):
    ns, length, _ = x1.shape
    nh = D_FF_PAD // FFN_HALF
    nt = length // lt
    ntu = length // lt_up
    resident = dict(pipeline_mode=pl.Buffered(1))
    val = lambda rows, **kw: pl.BlockSpec((rows, FFN_HALF), lambda h, s, i: (0, h), **kw)
    gate = lambda rows, **kw: pl.BlockSpec((rows, FFN_HALF), lambda h, s, i: (0, nh + h), **kw)
    swapped = _swapped_rows(sb, lt)
    if swapped:
        assert _swapped_rows(sb, lt_up)
        up_kernel = functools.partial(_ffn_up_swapped_kernel, lt=lt_up)
        up_scratch = [pltpu.VMEM((2, SUBLANES, FFN_HALF), f32)] * 2
        down_scratch = [_swap_scratch(sb * lt, D_MODEL)]
    else:
        up_kernel = functools.partial(_ffn_up_kernel, sb=sb, lt=lt_up)
        up_scratch = [pltpu.VMEM((sb, lt_up + CONV_PAD, FFN_HALF), f32)] * 2
        down_scratch = []
    act, nfv, nfg = pl.pallas_call(
        up_kernel,
        grid=(nh, ns // sb, ntu),
        in_specs=[
            pl.BlockSpec((sb * lt_up, D_MODEL), lambda h, s, i: (s * ntu + i, 0)),
            val(D_MODEL, **resident), val(D_MODEL, **resident),
            val(3), gate(3), val(1), gate(1),
            pl.BlockSpec((sb, 2, FFN_HALF), lambda h, s, i: (s, 0, h)),
            pl.BlockSpec((sb, 2, FFN_HALF), lambda h, s, i: (s, 0, nh + h)),
        ],
        out_specs=[
            pl.BlockSpec((sb * lt_up, FFN_HALF), lambda h, s, i: (s * ntu + i, h)),
            pl.BlockSpec((sb, None, 2, FFN_HALF), lambda h, s, i: (s, i, 0, h)),
            pl.BlockSpec((sb, None, 2, FFN_HALF), lambda h, s, i: (s, i, 0, h)),
        ],
        out_shape=[
            jax.ShapeDtypeStruct((ns * length, D_FF_PAD), bf16),
            jax.ShapeDtypeStruct((ns, ntu, 2, D_FF_PAD), f32),
            jax.ShapeDtypeStruct((ns, ntu, 2, D_FF_PAD), f32),
        ],
        scratch_shapes=up_scratch,
        compiler_params=_params(("arbitrary", "arbitrary", "arbitrary")),
        name="ffn_up",
    )(h2, w_val_p, w_gate_p, cw_p, cw_p, cb_p, cb_p, cache_p, cache_p)
    y = pl.pallas_call(
        functools.partial(_ffn_down_kernel, sb=sb, lt=lt),
        grid=(ns // sb, nt),
        in_specs=[
            pl.BlockSpec((sb * lt, D_FF_PAD), lambda s, i: (s * nt + i, 0)),
            pl.BlockSpec((sb, lt, D_MODEL), lambda s, i: (s, i, 0)),
            _const_spec((D_FF_PAD, D_MODEL)),
            _const_spec((1, D_MODEL)),
        ],
        out_specs=pl.BlockSpec((sb, lt, D_MODEL), lambda s, i: (s, i, 0)),
        out_shape=jax.ShapeDtypeStruct((ns, length, D_MODEL), f32),
        scratch_shapes=down_scratch,
        compiler_params=_params(("arbitrary", "arbitrary")),
        name="ffn_down",
    )(act, x1, w_down_p, gfin)
    return y, nfv, nfg


def _pad_ff(a, axis):
    pad = [(0, 0)] * a.ndim
    pad[axis] = (0, D_FF_PAD - D_FF)
    if a.shape[axis] == D_FF:
        return jnp.pad(a, pad)
    lo, hi = jnp.split(a, 2, axis=axis)
    return jnp.concatenate([jnp.pad(lo, pad), jnp.pad(hi, pad)], axis=axis)


def _cast_up_kernel(w_ref, val_ref, gate_ref):
    zeros = jnp.zeros((w_ref.shape[0], D_FF_PAD - D_FF), bf16)
    val_ref[:, :D_FF] = w_ref[:, :D_FF].astype(bf16)
    val_ref[:, D_FF:] = zeros
    gate_ref[:, :D_FF] = w_ref[:, D_FF:].astype(bf16)
    gate_ref[:, D_FF:] = zeros


def _cast_up(w_up, tr=256):
    out = pl.BlockSpec((tr, D_FF_PAD), lambda i: (i, 0))
    return pl.pallas_call(
        _cast_up_kernel,
        grid=(D_MODEL // tr,),
        in_specs=[pl.BlockSpec((tr, 2 * D_FF), lambda i: (i, 0))],
        out_specs=[out, out],
        out_shape=[jax.ShapeDtypeStruct((D_MODEL, D_FF_PAD), bf16)] * 2,
        compiler_params=_params(("arbitrary",)),
        name="cast_up",
    )(w_up)


def _cast_down_kernel(w_ref, o_ref, *, last):
    o_ref[...] = jnp.where(pl.program_id(0) <= last, w_ref[...], 0.0).astype(bf16)


def _cast_down(w_down, tr=LANES):
    last = D_FF // tr - 1
    return pl.pallas_call(
        functools.partial(_cast_down_kernel, last=last),
        grid=(D_FF_PAD // tr,),
        in_specs=[pl.BlockSpec((tr, D_MODEL), lambda i: (jnp.minimum(i, last), 0))],
        out_specs=pl.BlockSpec((tr, D_MODEL), lambda i: (i, 0)),
        out_shape=jax.ShapeDtypeStruct((D_FF_PAD, D_MODEL), bf16),
        compiler_params=_params(("arbitrary",)),
        name="cast_down",
    )(w_down)


def _trunk(x, s_re, s_im, sc_buf, ffn_buf, w, cfg):
    ns, length, _ = x.shape
    tm_in, sb_mix, lt_mix, sb_ffn, lt_ffn, lt_up, cb = cfg
    xf = x.reshape(ns * length, D_MODEL)
    u_a, gin, cv, gates = _in_proj(xf, w["norm1_g"], w["w_in"], tm_in)
    ya, n_re, n_im = _s5_mixer(u_a, s_re, s_im, w["s5"], cb)
    x1, h2, new_sc = _mix_out(ya.reshape(N_SLAB, ns, length, LANES), gin, cv, gates,
                              x, sc_buf, w["w_glu"], w["w_sc_out"], w["w_o"],
                              w["sc_conv_w"], w["sc_conv_b"], w["norm2_g"], sb_mix, lt_mix)
    y, nfv, nfg = _ffn(h2, x1, w["final_norm_g"], w["w_val"], w["w_gate"], w["w_down"],
                       w["ffn_conv_w"], w["ffn_conv_b"], _pad_ff(ffn_buf, 2),
                       sb_ffn, lt_ffn, lt_up)
    new_ffn = jnp.concatenate([nfv[:, -1, :, :D_FF], nfg[:, -1, :, :D_FF]], axis=-1)
    return y, n_re[None], n_im[None], new_sc[None], new_ffn[None]


def kernel(x_prompt, x_sample, state_s5_re, state_s5_im, cache_sc_conv, cache_ffn_conv, norm1_g, w_in, lam_re, lam_im, log_dt, b_re, b_im, c_re, c_im, d_skip, w_glu, sc_conv_w, sc_conv_b, w_sc_out, w_o, norm2_g, w_up, ffn_conv_w, ffn_conv_b, w_down, final_norm_g):
    w_val, w_gate = _cast_up(w_up[0])
    w = {
        "norm1_g": norm1_g[0].reshape(1, D_MODEL),
        "w_in": w_in[0].astype(bf16),
        "s5": _s5_tables(lam_re[0], lam_im[0], log_dt[0], b_re[0], b_im[0],
                         c_re[0], c_im[0], d_skip[0]),
        "w_glu": w_glu[0].astype(bf16),
        "sc_conv_w": sc_conv_w[0],
        "sc_conv_b": sc_conv_b[0].reshape(1, D_B),
        "w_sc_out": w_sc_out[0].astype(bf16),
        "w_o": w_o[0].astype(bf16),
        "norm2_g": norm2_g[0].reshape(1, D_MODEL),
        "w_val": w_val,
        "w_gate": w_gate,
        "ffn_conv_w": _pad_ff(ffn_conv_w[0], 1),
        "ffn_conv_b": _pad_ff(ffn_conv_b[0].reshape(1, 2 * D_FF), 1),
        "w_down": _cast_down(w_down[0]),
        "final_norm_g": final_norm_g.reshape(1, D_MODEL),
    }
    bp, lp, _ = x_prompt.shape
    bs, ls, _ = x_sample.shape
    zeros = lambda *s: jnp.zeros(s, f32)
    p_cfg = (256, 1, 256, 1, 256, 512, 64)
    yp, p_re, p_im, p_sc, p_ffn = _trunk(
        x_prompt, zeros(bp, G_A, N_STATE), zeros(bp, G_A, N_STATE),
        zeros(bp, 2, D_B), zeros(bp, 2, 2 * D_FF), w, p_cfg)
    s_cfg = (256, bs // 2, ls, bs // 2, ls, ls, 1)
    ys, s_re, s_im, s_sc, s_ffn = _trunk(
        x_sample, state_s5_re[0], state_s5_im[0], cache_sc_conv[0], cache_ffn_conv[0], w, s_cfg)
    return (yp, ys, p_re, p_im, p_sc, p_ffn, s_re, s_im, s_sc, s_ffn)
```

```python
import functools

import jax
import jax.numpy as jnp
import numpy as np
from jax import lax
from jax.experimental import pallas as pl
from jax.experimental.pallas import tpu as pltpu

D_MODEL = 2048
D_A = D_MODEL // 2
S5_GROUP = 16
G_A = D_A // S5_GROUP
N_STATE = 64
D_B = D_MODEL // 2
D_FF = 5504
EPS = 1e-6
IN_COLS = D_A + 3 * D_B + 2 * D_MODEL

LANES = 128
SUBLANES = 8
MXU_DIM = 256
VMEM_LIMIT_BYTES = 58 * 1024 * 1024

N_SLAB = D_A // LANES
SLAB_G = LANES // S5_GROUP
S5_T = 16
S5_TQ = S5_T * S5_GROUP
S5_K = S5_T * LANES
S5_ST = 2 * N_STATE
S5_SL = SLAB_G * S5_ST
D_FF_PAD = 5632
FFN_HALF = D_FF_PAD // 2
CONV_PAD = SUBLANES

bf16 = jnp.bfloat16
f32 = jnp.float32


def _rmsnorm(x, g):
    ms = jnp.mean(x * x, axis=-1, keepdims=True)
    return x * lax.rsqrt(ms + EPS) * g


def _params(sem):
    return pltpu.CompilerParams(dimension_semantics=sem, vmem_limit_bytes=VMEM_LIMIT_BYTES)


def _const_spec(shape):
    nd = len(shape)
    return pl.BlockSpec(shape, lambda *_: (0,) * nd, pipeline_mode=pl.Buffered(1))


def _dot(a, b):
    return jnp.dot(a, b, preferred_element_type=f32)


def _in_proj_kernel(x_ref, g_ref, w_ref, u_ref, gin_ref, cv_ref, gates_ref):
    h = _rmsnorm(x_ref[...], g_ref[...]).astype(bf16)
    u = _dot(h, w_ref[:, :D_A])
    for k in range(N_SLAB):
        u_ref[k] = u[:, k * LANES:(k + 1) * LANES]
    gin_ref[...] = _dot(h, w_ref[:, D_A:D_A + D_B]).astype(bf16)
    c = _dot(h, w_ref[:, D_A + D_B:D_A + 2 * D_B])
    v = _dot(h, w_ref[:, D_A + 2 * D_B:D_A + 3 * D_B])
    cv_ref[...] = (c * v).astype(bf16)
    g0 = D_A + 3 * D_B
    for n in range(2 * D_MODEL // D_B):
        acc = _dot(h, w_ref[:, g0 + n * D_B:g0 + (n + 1) * D_B])
        gates_ref[:, n * D_B:(n + 1) * D_B] = jax.nn.sigmoid(acc).astype(bf16)


def _in_proj(x, norm_g, w_in_b, tm):
    t = x.shape[0]
    return pl.pallas_call(
        _in_proj_kernel,
        grid=(t // tm,),
        in_specs=[
            pl.BlockSpec((tm, D_MODEL), lambda i: (i, 0)),
            _const_spec((1, D_MODEL)),
            _const_spec((D_MODEL, IN_COLS)),
        ],
        out_specs=[
            pl.BlockSpec((N_SLAB, tm, LANES), lambda i: (0, i, 0)),
            pl.BlockSpec((tm, D_B), lambda i: (i, 0)),
            pl.BlockSpec((tm, D_B), lambda i: (i, 0)),
            pl.BlockSpec((tm, 2 * D_MODEL), lambda i: (i, 0)),
        ],
        out_shape=[
            jax.ShapeDtypeStruct((N_SLAB, t, LANES), f32),
            jax.ShapeDtypeStruct((t, D_B), bf16),
            jax.ShapeDtypeStruct((t, D_B), bf16),
            jax.ShapeDtypeStruct((t, 2 * D_MODEL), bf16),
        ],
        compiler_params=_params(("arbitrary",)),
        name="in_proj",
    )(x, norm_g, w_in_b)


def _group_mask(shape, row0, row_shift, lane_shift):
    r = lax.broadcasted_iota(jnp.int32, shape, 0) + row0
    c = lax.broadcasted_iota(jnp.int32, shape, 1)
    return ((r >> row_shift) & (SLAB_G - 1)) == ((c >> lane_shift) & (SLAB_G - 1))


def _expand_tables(kc_ref, vc_ref, wc_refs, r_ref, m_scr, v_scr, w_scrs):
    rep = r_ref[...]
    bd = _dot(kc_ref[...], rep)
    bd = jnp.where(_group_mask(bd.shape, 0, 4, 4), bd, 0.0).astype(bf16)
    for t in range(S5_T):
        rows = slice(t * LANES, (t + 1) * LANES)
        if t:
            m_scr[rows, :t * LANES] = jnp.zeros((LANES, t * LANES), bf16)
        m_scr[rows, t * LANES:] = bd[:, :(S5_T - t) * LANES]
    step = MXU_DIM
    for r0 in range(0, S5_SL, step):
        vb = _dot(vc_ref[r0:r0 + step, :], rep)
        v_scr[r0:r0 + step, :] = jnp.where(_group_mask(vb.shape, r0, 7, 4), vb, 0.0).astype(bf16)
    for wc_ref, w_scr in zip(wc_refs, w_scrs):
        for r0 in range(0, S5_K, step):
            wt = jnp.concatenate([wc_ref[r0:r0 + step, :].astype(f32)] * SLAB_G, axis=1)
            w_scr[r0:r0 + step, :] = jnp.where(_group_mask(wt.shape, r0, 4, 7), wt, 0.0).astype(bf16)


def _s5_toeplitz(lhs, m_scr):
    return [_dot(lhs[:, :(a + 1) * MXU_DIM],
                 m_scr[:(a + 1) * MXU_DIM, a * MXU_DIM:(a + 1) * MXU_DIM])
            for a in range(S5_K // MXU_DIM)]


def _s5_outputs(toep, sprev, xs, v_scr, d_ref, store):
    per = MXU_DIM // LANES
    for a, part in enumerate(toep):
        y = part + _dot(sprev, v_scr[:, a * MXU_DIM:(a + 1) * MXU_DIM])
        for i in range(per):
            t = a * per + i
            piece = y[:, i * LANES:(i + 1) * LANES] + d_ref[...] * xs[t]
            store(t, jax.nn.gelu(piece))


def _s5_prompt_kernel(u_ref, kc_ref, vc_ref, wc_ref, r_ref, pq_ref, d_ref, h0_ref, h0s_ref,
                      y_ref, sfin_ref,
                      m_scr, v_scr, w_scr, s_scr, t_scr, loc_scr, loct_scr, prev_scr, *, nb, cb):
    ci = pl.program_id(1)

    @pl.when(ci == 0)
    def _():
        _expand_tables(kc_ref, vc_ref, [wc_ref], r_ref, m_scr, v_scr, [w_scr])
        s_scr[...] = h0_ref[...]
        t_scr[...] = h0s_ref[...]
        loc_scr[...] = jnp.zeros_like(loc_scr)
        loct_scr[...] = jnp.zeros_like(loct_scr)

    xs = [jnp.concatenate([u_ref[b, pl.ds(t, cb, stride=S5_T), :] for b in range(nb)], axis=0)
          for t in range(S5_T)]
    lhs = jnp.concatenate(xs, axis=1).astype(bf16)

    loc = _dot(lhs, w_scr[...])
    toep = _s5_toeplitz(lhs, m_scr)
    for j in range(SLAB_G):
        lj = loc[:, j * S5_ST:(j + 1) * S5_ST]
        ljt = pltpu.roll(lj, N_STATE, axis=1)
        for b in range(nb):
            rows = slice(b * cb, (b + 1) * cb)
            loc_scr[j, pl.ds(b, cb, stride=SUBLANES), :] = lj[rows]
            loct_scr[j, pl.ds(b, cb, stride=SUBLANES), :] = ljt[rows]

    ss = [s_scr[j] for j in range(SLAB_G)]
    ts = [t_scr[j] for j in range(SLAB_G)]
    for c in range(cb):
        rows = slice(c * SUBLANES, (c + 1) * SUBLANES)
        for j in range(SLAB_G):
            p = pq_ref[0, j:j + 1, :]
            q = pq_ref[1, j:j + 1, :]
            prev_scr[j, rows, :] = ss[j]
            ss[j], ts[j] = (p * ss[j] + q * ts[j] + loc_scr[j, rows, :],
                            p * ts[j] - q * ss[j] + loct_scr[j, rows, :])
    for j in range(SLAB_G):
        s_scr[j] = ss[j]
        t_scr[j] = ts[j]
        sfin_ref[j] = ss[j]

    sprev = jnp.concatenate(
        [jnp.concatenate([prev_scr[j, pl.ds(b, cb, stride=SUBLANES), :] for b in range(nb)], axis=0)
         for j in range(SLAB_G)], axis=1).astype(bf16)

    def store(t, val):
        for b in range(nb):
            y_ref[b, pl.ds(t, cb, stride=S5_T), :] = val[b * cb:(b + 1) * cb]

    _s5_outputs(toep, sprev, xs, v_scr, d_ref, store)


def _s5_prompt(u, tables, h0, h0s, cb):
    kc, vc, wc, _, rep, pq, dvec = tables
    _, nb, length, _ = u.shape
    nc = length // S5_T
    kern = functools.partial(_s5_prompt_kernel, nb=nb, cb=cb)
    tok = pl.BlockSpec((None, nb, cb * S5_T, LANES), lambda k, ci: (k, 0, ci, 0))
    per_slab = lambda *shape: pl.BlockSpec((None,) + shape, lambda k, ci: (k,) + (0,) * len(shape))
    state = per_slab(SLAB_G, SUBLANES, S5_ST)
    return pl.pallas_call(
        kern,
        grid=(N_SLAB, nc // cb),
        in_specs=[tok, per_slab(LANES, S5_TQ), per_slab(S5_SL, S5_TQ), per_slab(S5_K, S5_ST),
                  _const_spec((S5_TQ, S5_K)),
                  per_slab(2, SLAB_G, S5_ST), per_slab(1, LANES), state, state],
        out_specs=[tok, state],
        out_shape=[
            jax.ShapeDtypeStruct(u.shape, f32),
            jax.ShapeDtypeStruct((N_SLAB, SLAB_G, SUBLANES, S5_ST), f32),
        ],
        scratch_shapes=[
            pltpu.VMEM((S5_K, S5_K), bf16),
            pltpu.VMEM((S5_SL, S5_K), bf16),
            pltpu.VMEM((S5_K, S5_SL), bf16),
            pltpu.VMEM((SLAB_G, SUBLANES, S5_ST), f32),
            pltpu.VMEM((SLAB_G, SUBLANES, S5_ST), f32),
            pltpu.VMEM((SLAB_G, cb * SUBLANES, S5_ST), f32),
            pltpu.VMEM((SLAB_G, cb * SUBLANES, S5_ST), f32),
            pltpu.VMEM((SLAB_G, cb * SUBLANES, S5_ST), f32),
        ],
        compiler_params=_params(("arbitrary", "arbitrary")),
        name="s5_prompt",
    )(u, kc, vc, wc, rep, pq, dvec, h0, h0s)


def _s5_sample_kernel(u_ref, kc_ref, vc_ref, wc_ref, wlc_ref, r_ref, pq_ref, d_ref, h0_ref, h0s_ref,
                      y_ref, sfin_ref, m_scr, v_scr, wh_scr, wl_scr, *, nb):
    _expand_tables(kc_ref, vc_ref, [wc_ref, wlc_ref], r_ref, m_scr, v_scr, [wh_scr, wl_scr])
    xs = [u_ref[pl.ds(t, nb, stride=S5_T), :] for t in range(S5_T)]
    lhs_f = jnp.concatenate(xs, axis=1)
    lhs = lhs_f.astype(bf16)
    lhs_lo = (lhs_f - lhs.astype(f32)).astype(bf16)
    wh = wh_scr[...]
    loc = _dot(lhs, wh) + _dot(lhs_lo, wh) + _dot(lhs, wl_scr[...])
    for j in range(SLAB_G):
        p = pq_ref[0, j:j + 1, :]
        q = pq_ref[1, j:j + 1, :]
        sfin_ref[j] = p * h0_ref[j] + q * h0s_ref[j] + loc[:, j * S5_ST:(j + 1) * S5_ST]
    sprev = jnp.concatenate([h0_ref[j] for j in range(SLAB_G)], axis=1).astype(bf16)

    def store(t, val):
        y_ref[pl.ds(t, nb, stride=S5_T), :] = val

    _s5_outputs(_s5_toeplitz(lhs, m_scr), sprev, xs, v_scr, d_ref, store)


def _s5_sample(u, tables, h0, h0s, nb):
    kc, vc, wc, wlc, rep, pq, dvec = tables
    kern = functools.partial(_s5_sample_kernel, nb=nb)
    per_slab = lambda *shape: pl.BlockSpec((None,) + shape, lambda k: (k,) + (0,) * len(shape))
    tok = per_slab(nb * S5_T, LANES)
    state = per_slab(SLAB_G, nb, S5_ST)
    return pl.pallas_call(
        kern,
        grid=(N_SLAB,),
        in_specs=[tok, per_slab(LANES, S5_TQ), per_slab(S5_SL, S5_TQ), per_slab(S5_K, S5_ST),
                  per_slab(S5_K, S5_ST), _const_spec((S5_TQ, S5_K)),
                  per_slab(2, SLAB_G, S5_ST), per_slab(1, LANES), state, state],
        out_specs=[tok, state],
        out_shape=[
            jax.ShapeDtypeStruct(u.shape, f32),
            jax.ShapeDtypeStruct((N_SLAB, SLAB_G, nb, S5_ST), f32),
        ],
        scratch_shapes=[
            pltpu.VMEM((S5_K, S5_K), bf16),
            pltpu.VMEM((S5_SL, S5_K), bf16),
            pltpu.VMEM((S5_K, S5_SL), bf16),
            pltpu.VMEM((S5_K, S5_SL), bf16),
        ],
        compiler_params=_params(("arbitrary",)),
        name="s5_sample",
    )(u, kc, vc, wc, wlc, rep, pq, dvec, h0, h0s)


def _replication_matrix():
    src = np.arange(S5_TQ)
    dst = np.arange(S5_K)
    same = ((src[:, None] // S5_GROUP == dst[None, :] // LANES)
            & (src[:, None] % S5_GROUP == dst[None, :] % S5_GROUP))
    return jnp.asarray(same, dtype=bf16)


def _s5_tables(lam_re, lam_im, log_dt, b_re, b_im, c_re, c_im, d_skip):
    hp = lax.Precision.HIGHEST
    dt = jnp.exp(log_dt.astype(f32))[:, None]
    lr, li = lam_re.astype(f32), lam_im.astype(f32)
    mag = jnp.exp(lr * dt)
    ab_re = mag * jnp.cos(li * dt)
    ab_im = mag * jnp.sin(li * dt)
    den = lr * lr + li * li
    nr, ni = ab_re - 1.0, ab_im
    f_re = (nr * lr + ni * li) / den
    f_im = (ni * lr - nr * li) / den
    br, bi = b_re.astype(f32), b_im.astype(f32)
    bb_re = f_re[..., None] * br - f_im[..., None] * bi
    bb_im = f_re[..., None] * bi + f_im[..., None] * br
    cr, ci = c_re.astype(f32), c_im.astype(f32)

    k = jnp.arange(S5_T + 1, dtype=f32)[:, None, None]
    pm = jnp.exp(lr * dt * k)
    pr = pm * jnp.cos(li * dt * k)
    pi = pm * jnp.sin(li * dt * k)

    lag = np.arange(S5_TQ) // S5_GROUP
    col = np.arange(S5_TQ) % S5_GROUP
    fill_q = jnp.asarray(np.arange(S5_GROUP)[:, None] == col[None, :], dtype=f32)
    fill_t = jnp.asarray(np.arange(S5_T)[:, None] == lag[None, :], dtype=f32)
    rows = G_A * N_STATE

    def on_lanes(a, fill):
        return jnp.dot(a.reshape(rows, a.shape[-1]), fill, precision=hp)

    ct_re = on_lanes(cr.transpose(0, 2, 1), fill_q)
    ct_im = on_lanes(ci.transpose(0, 2, 1), fill_q)

    def c_times_powers(p_re, p_im):
        pt_re = on_lanes(p_re.transpose(1, 2, 0), fill_t)
        pt_im = on_lanes(p_im.transpose(1, 2, 0), fill_t)
        return ((ct_re * pt_re - ct_im * pt_im).reshape(G_A, N_STATE, S5_TQ),
                (ct_re * pt_im + ct_im * pt_re).reshape(G_A, N_STATE, S5_TQ))

    e_re, e_im = c_times_powers(pr[:S5_T], pi[:S5_T])
    kk = jnp.einsum('gpn,gnx->gpx',
                    jnp.concatenate([bb_re.transpose(0, 2, 1), -bb_im.transpose(0, 2, 1)], axis=2),
                    jnp.concatenate([e_re, e_im], axis=1), precision=hp)
    kc = kk.reshape(N_SLAB, LANES, S5_TQ).astype(bf16)

    def slab_time(a):
        return a.reshape(S5_T, N_SLAB, SLAB_G, 1, N_STATE).transpose(1, 0, 2, 3, 4)

    back = S5_T - 1 - jnp.arange(S5_T)
    prr, pir = slab_time(pr[back]), slab_time(pi[back])
    bt_re = bb_re.transpose(0, 2, 1).reshape(N_SLAB, 1, SLAB_G, S5_GROUP, N_STATE)
    bt_im = bb_im.transpose(0, 2, 1).reshape(N_SLAB, 1, SLAB_G, S5_GROUP, N_STATE)
    lanes = lambda a, b: jnp.concatenate([a, b], axis=-1)
    w = lanes(prr, prr) * lanes(bt_re, bt_im) + lanes(-pir, pir) * lanes(bt_im, bt_re)
    w = w.reshape(N_SLAB, S5_K, S5_ST)
    wc = w.astype(bf16)
    wlc = (w - wc.astype(f32)).astype(bf16)

    v_re, v_im = c_times_powers(pr[1:], pi[1:])
    vc = jnp.concatenate([v_re, -v_im], axis=1).reshape(N_SLAB, S5_SL, S5_TQ).astype(bf16)

    ar, ai = pr[S5_T], pi[S5_T]
    p_row = jnp.concatenate([ar, ar], axis=1).reshape(N_SLAB, 1, SLAB_G, S5_ST)
    q_row = jnp.concatenate([-ai, ai], axis=1).reshape(N_SLAB, 1, SLAB_G, S5_ST)
    pq = jnp.concatenate([p_row, q_row], axis=1)
    dvec = d_skip.astype(f32).reshape(N_SLAB, 1, LANES)
    return kc, vc, wc, wlc, _replication_matrix(), pq, dvec


def _s5_mixer(u, s_re, s_im, tables, cb):
    nb = s_re.shape[0]
    length = u.shape[1] // nb

    def to_slab(a):
        return a.reshape(nb, N_SLAB, SLAB_G, S5_ST).transpose(1, 2, 0, 3)

    h0 = to_slab(jnp.concatenate([s_re, s_im], axis=-1))
    h0s = to_slab(jnp.concatenate([s_im, s_re], axis=-1))
    if length == S5_T:
        y, sfin = _s5_sample(u, tables, h0, h0s, nb)
    else:
        pad = [(0, 0), (0, 0), (0, SUBLANES - nb), (0, 0)]
        y, sfin = _s5_prompt(u.reshape(N_SLAB, nb, length, LANES), tables,
                             jnp.pad(h0, pad), jnp.pad(h0s, pad), cb)
        y = y.reshape(u.shape)
        sfin = sfin[:, :, :nb]
    sfin = sfin.transpose(2, 0, 1, 3).reshape(nb, G_A, S5_ST)
    return y, sfin[..., :N_STATE], sfin[..., N_STATE:]


def _conv3(ext_ref, w_ref, b_ref, lt, cols=slice(None)):
    lo = CONV_PAD - 2
    out = ext_ref[:, lo:lo + lt, cols] * w_ref[0:1, cols]
    out = out + ext_ref[:, lo + 1:lo + 1 + lt, cols] * w_ref[1:2, cols]
    out = out + ext_ref[:, lo + 2:lo + 2 + lt, cols] * w_ref[2:3, cols]
    return out + b_ref[:, cols]


PERM_BLOCK = SUBLANES * SUBLANES


def _swap_rows(val, scr):
    rows, width = val.shape
    nsl = width // LANES
    for k in range(nsl):
        scr[k] = val[:, k * LANES:(k + 1) * LANES]
    return jnp.concatenate(
        [jnp.concatenate([scr[k, pl.ds(r0 + m, SUBLANES, stride=SUBLANES), :]
                          for r0 in range(0, rows, PERM_BLOCK) for m in range(SUBLANES)], axis=0)
         for k in range(nsl)], axis=1)


def _mix_kernel(ya_ref, gin_ref, cv_ref, gates_ref, x_ref, cache_ref,
                wglu_ref, wsc_ref, wo_ref, cw_ref, cb_ref, g2_ref,
                x1_ref, h2_ref, nsc_ref, ext_scr, mrg_scr, *perm_scr, sb, lt, nchunk):
    i = pl.program_id(1)
    rows = sb * lt
    lo = CONV_PAD - 2

    @pl.when(i == 0)
    def _():
        ext_scr[:, lo:CONV_PAD, :] = cache_ref[...]

    @pl.when(i > 0)
    def _():
        ext_scr[:, lo:CONV_PAD, :] = ext_scr[:, lo + lt:CONV_PAD + lt, :]

    ext_scr[:, CONV_PAD:, :] = cv_ref[...].astype(f32).reshape(sb, lt, D_B)
    nsc_ref[...] = ext_scr[:, lo + lt:CONV_PAD + lt, :]

    conv = _conv3(ext_scr, cw_ref, cb_ref, lt).reshape(rows, D_B)
    gated = (gin_ref[...].astype(f32) * conv).astype(bf16)
    ya = jnp.concatenate([ya_ref[k].reshape(rows, LANES) for k in range(N_SLAB)],
                         axis=1).astype(bf16)
    wc = D_MODEL // nchunk
    for n in range(nchunk):
        cols = slice(n * wc, (n + 1) * wc)
        gcols = slice(D_MODEL + n * wc, D_MODEL + (n + 1) * wc)
        a = _dot(ya, wglu_ref[:, cols])
        gt = _dot(ya, wglu_ref[:, gcols])
        br_a = a * jax.nn.sigmoid(gt)
        br_b = _dot(gated, wsc_ref[:, cols])
        g_a = gates_ref[:, cols].astype(f32)
        g_b = gates_ref[:, gcols].astype(f32)
        mrg_scr[:, cols] = (g_a * br_a + g_b * br_b).astype(bf16)
    x1 = x_ref[...].reshape(rows, D_MODEL) + _dot(mrg_scr[...], wo_ref[...])
    x1_ref[...] = x1.reshape(sb, lt, D_MODEL)
    h2 = _rmsnorm(x1, g2_ref[...])
    if perm_scr:
        h2 = _swap_rows(h2, perm_scr[0])
    h2_ref[...] = h2.astype(bf16)


def _swapped_rows(sb, lt):
    return sb == 1 and lt % PERM_BLOCK == 0


def _swap_scratch(rows, width):
    return pltpu.VMEM((width // LANES, rows, LANES), f32)


def _mix_out(ya, gin, cv, gates, x, cache, wglu, wsc, wo, cw, cbias, g2, sb, lt):
    ns, length, _ = x.shape
    nt = length // lt
    kern = functools.partial(_mix_kernel, sb=sb, lt=lt, nchunk=4)
    perm = [_swap_scratch(sb * lt, D_MODEL)] if _swapped_rows(sb, lt) else []
    tile = lambda c: pl.BlockSpec((sb, lt, c), lambda s, i: (s, i, 0))
    flat = lambda c: pl.BlockSpec((sb * lt, c), lambda s, i: (s * nt + i, 0))
    return pl.pallas_call(
        kern,
        grid=(ns // sb, nt),
        in_specs=[
            pl.BlockSpec((N_SLAB, sb, lt, LANES), lambda s, i: (0, s, i, 0)),
            flat(D_B), flat(D_B), flat(2 * D_MODEL), tile(D_MODEL),
            pl.BlockSpec((sb, 2, D_B), lambda s, i: (s, 0, 0)),
            _const_spec((D_A, 2 * D_MODEL)),
            _const_spec((D_B, D_MODEL)),
            _const_spec((D_MODEL, D_MODEL)),
            _const_spec((3, D_B)),
            _const_spec((1, D_B)),
            _const_spec((1, D_MODEL)),
        ],
        out_specs=[
            tile(D_MODEL),
            flat(D_MODEL),
            pl.BlockSpec((sb, 2, D_B), lambda s, i: (s, 0, 0)),
        ],
        out_shape=[
            jax.ShapeDtypeStruct((ns, length, D_MODEL), f32),
            jax.ShapeDtypeStruct((ns * length, D_MODEL), bf16),
            jax.ShapeDtypeStruct((ns, 2, D_B), f32),
        ],
        scratch_shapes=[
            pltpu.VMEM((sb, lt + CONV_PAD, D_B), f32),
            pltpu.VMEM((sb * lt, D_MODEL), bf16),
        ] + perm,
        compiler_params=_params(("arbitrary", "arbitrary")),
        name="mix_out",
    )(ya, gin, cv, gates, x, cache, wglu, wsc, wo, cw, cbias, g2)


def _ffn_up_kernel(h2_ref, wv_ref, wg_ref, cwv_ref, cwg_ref, cbv_ref, cbg_ref,
                   cachev_ref, cacheg_ref, act_ref, nfv_ref, nfg_ref, extv_scr, extg_scr,
                   *, sb, lt):
    i = pl.program_id(2)
    rows = sb * lt
    lo = CONV_PAD - 2

    @pl.when(i == 0)
    def _():
        extv_scr[:, lo:CONV_PAD, :] = cachev_ref[...]
        extg_scr[:, lo:CONV_PAD, :] = cacheg_ref[...]

    @pl.when(i > 0)
    def _():
        extv_scr[:, lo:CONV_PAD, :] = extv_scr[:, lo + lt:CONV_PAD + lt, :]
        extg_scr[:, lo:CONV_PAD, :] = extg_scr[:, lo + lt:CONV_PAD + lt, :]

    h2 = h2_ref[...]
    extv_scr[:, CONV_PAD:, :] = _dot(h2, wv_ref[...]).reshape(sb, lt, FFN_HALF)
    extg_scr[:, CONV_PAD:, :] = _dot(h2, wg_ref[...]).reshape(sb, lt, FFN_HALF)
    for c in range(FFN_HALF // MXU_DIM):
        cols = slice(c * MXU_DIM, (c + 1) * MXU_DIM)
        val = _conv3(extv_scr, cwv_ref, cbv_ref, lt, cols)
        gt = _conv3(extg_scr, cwg_ref, cbg_ref, lt, cols)
        act_ref[:, cols] = (jax.nn.silu(gt) * val).reshape(rows, MXU_DIM).astype(bf16)
    nfv_ref[...] = extv_scr[:, lo + lt:CONV_PAD + lt, :]
    nfg_ref[...] = extg_scr[:, lo + lt:CONV_PAD + lt, :]


def _conv3_swapped(x, car_ref, w_ref, b_ref, cols):
    x6, x7 = x[:, SUBLANES - 2], x[:, SUBLANES - 1]
    prev6 = jnp.concatenate([car_ref[0, :, cols][None], x6[:-1]], axis=0)
    prev7 = jnp.concatenate([car_ref[1, :, cols][None], x7[:-1]], axis=0)
    car_ref[0, :, cols] = x6[-1]
    car_ref[1, :, cols] = x7[-1]
    first = lax.broadcasted_iota(jnp.int32, x6.shape, 1) == 0
    back2 = jnp.where(first, pltpu.roll(prev6, 1, axis=1), pltpu.roll(x6, 1, axis=1))
    back1 = jnp.where(first, pltpu.roll(prev7, 1, axis=1), pltpu.roll(x7, 1, axis=1))
    s1 = jnp.concatenate([back1[:, None], x[:, :SUBLANES - 1]], axis=1)
    s2 = jnp.concatenate([back2[:, None], back1[:, None], x[:, :SUBLANES - 2]], axis=1)
    return (s2 * w_ref[0:1, cols] + s1 * w_ref[1:2, cols] + x * w_ref[2:3, cols]
            + b_ref[:, cols])


def _ffn_up_swapped_kernel(h2_ref, wv_ref, wg_ref, cwv_ref, cwg_ref, cbv_ref, cbg_ref,
                           cachev_ref, cacheg_ref, act_ref, nfv_ref, nfg_ref,
                           carv_scr, carg_scr, *, lt):
    i = pl.program_id(2)
    nblk = lt // PERM_BLOCK

    @pl.when(i == 0)
    def _():
        for car, cache in ((carv_scr, cachev_ref), (carg_scr, cacheg_ref)):
            for r in range(2):
                car[r] = jnp.broadcast_to(cache[0, r:r + 1, :], (SUBLANES, FFN_HALF))

    h2 = h2_ref[...]
    upv = _dot(h2, wv_ref[...])
    upg = _dot(h2, wg_ref[...])
    for c in range(FFN_HALF // MXU_DIM):
        cols = slice(c * MXU_DIM, (c + 1) * MXU_DIM)
        shape = (nblk, SUBLANES, SUBLANES, MXU_DIM)
        val = _conv3_swapped(upv[:, cols].reshape(shape), carv_scr, cwv_ref, cbv_ref, cols)
        gt = _conv3_swapped(upg[:, cols].reshape(shape), carg_scr, cwg_ref, cbg_ref, cols)
        act_ref[:, cols] = (jax.nn.silu(gt) * val).reshape(lt, MXU_DIM).astype(bf16)
    for car, nf_ref in ((carv_scr, nfv_ref), (carg_scr, nfg_ref)):
        for r in range(2):
            nf_ref[0, r:r + 1, :] = car[r, SUBLANES - 1:SUBLANES, :]


def _ffn_down_kernel(act_ref, x1_ref, wd_ref, gf_ref, y_ref, *swap_scr, sb, lt):
    rows = sb * lt
    delta = _dot(act_ref[:, :D_FF], wd_ref[...])
    if swap_scr:
        delta = _swap_rows(delta, swap_scr[0])
    x2 = x1_ref[...].reshape(rows, D_MODEL) + delta
    y_ref[...] = _rmsnorm(x2, gf_ref[...]).reshape(sb, lt, D_MODEL)


def _ffn(h2, x1, gfin, w_val_p, w_gate_p, w_down_p, cw_p, cb_p, cache_p, sb, lt, lt_up):
    ns, length, _ = x1.shape
    nh = D_FF_PAD // FFN_HALF
    nt = length // lt
    ntu = length // lt_up
    resident = dict(pipeline_mode=pl.Buffered(1))
    val = lambda rows, **kw: pl.BlockSpec((rows, FFN_HALF), lambda h, s, i: (0, h), **kw)
    gate = lambda rows, **kw: pl.BlockSpec((rows, FFN_HALF), lambda h, s, i: (0, nh + h), **kw)
    swapped = _swapped_rows(sb, lt)
    if swapped:
        assert _swapped_rows(sb, lt_up)
        up_kernel = functools.partial(_ffn_up_swapped_kernel, lt=lt_up)
        up_scratch = [pltpu.VMEM((2, SUBLANES, FFN_HALF), f32)] * 2
        down_scratch = [_swap_scratch(sb * lt, D_MODEL)]
    else:
        up_kernel = functools.partial(_ffn_up_kernel, sb=sb, lt=lt_up)
        up_scratch = [pltpu.VMEM((sb, lt_up + CONV_PAD, FFN_HALF), f32)] * 2
        down_scratch = []
    act, nfv, nfg = pl.pallas_call(
        up_kernel,
        grid=(nh, ns // sb, ntu),
        in_specs=[
            pl.BlockSpec((sb * lt_up, D_MODEL), lambda h, s, i: (s * ntu + i, 0)),
            val(D_MODEL, **resident), val(D_MODEL, **resident),
            val(3), gate(3), val(1), gate(1),
            pl.BlockSpec((sb, 2, FFN_HALF), lambda h, s, i: (s, 0, h)),
            pl.BlockSpec((sb, 2, FFN_HALF), lambda h, s, i: (s, 0, nh + h)),
        ],
        out_specs=[
            pl.BlockSpec((sb * lt_up, FFN_HALF), lambda h, s, i: (s * ntu + i, h)),
            pl.BlockSpec((sb, None, 2, FFN_HALF), lambda h, s, i: (s, i, 0, h)),
            pl.BlockSpec((sb, None, 2, FFN_HALF), lambda h, s, i: (s, i, 0, h)),
        ],
        out_shape=[
            jax.ShapeDtypeStruct((ns * length, D_FF_PAD), bf16),
            jax.ShapeDtypeStruct((ns, ntu, 2, D_FF_PAD), f32),
            jax.ShapeDtypeStruct((ns, ntu, 2, D_FF_PAD), f32),
        ],
        scratch_shapes=up_scratch,
        compiler_params=_params(("arbitrary", "arbitrary", "arbitrary")),
        name="ffn_up",
    )(h2, w_val_p, w_gate_p, cw_p, cw_p, cb_p, cb_p, cache_p, cache_p)
    y = pl.pallas_call(
        functools.partial(_ffn_down_kernel, sb=sb, lt=lt),
        grid=(ns // sb, nt),
        in_specs=[
            pl.BlockSpec((sb * lt, D_FF_PAD), lambda s, i: (s * nt + i, 0)),
            pl.BlockSpec((sb, lt, D_MODEL), lambda s, i: (s, i, 0)),
            _const_spec((D_FF, D_MODEL)),
            _const_spec((1, D_MODEL)),
        ],
        out_specs=pl.BlockSpec((sb, lt, D_MODEL), lambda s, i: (s, i, 0)),
        out_shape=jax.ShapeDtypeStruct((ns, length, D_MODEL), f32),
        scratch_shapes=down_scratch,
        compiler_params=_params(("arbitrary", "arbitrary")),
        name="ffn_down",
    )(act, x1, w_down_p, gfin)
    return y, nfv, nfg


def _pad_ff(a, axis):
    pad = [(0, 0)] * a.ndim
    pad[axis] = (0, D_FF_PAD - D_FF)
    if a.shape[axis] == D_FF:
        return jnp.pad(a, pad)
    lo, hi = jnp.split(a, 2, axis=axis)
    return jnp.concatenate([jnp.pad(lo, pad), jnp.pad(hi, pad)], axis=axis)


def _cast_up_kernel(w_ref, val_ref, gate_ref):
    zeros = jnp.zeros((w_ref.shape[0], D_FF_PAD - D_FF), bf16)
    val_ref[:, :D_FF] = w_ref[:, :D_FF].astype(bf16)
    val_ref[:, D_FF:] = zeros
    gate_ref[:, :D_FF] = w_ref[:, D_FF:].astype(bf16)
    gate_ref[:, D_FF:] = zeros


def _cast_up(w_up, tr=256):
    out = pl.BlockSpec((tr, D_FF_PAD), lambda i: (i, 0))
    return pl.pallas_call(
        _cast_up_kernel,
        grid=(D_MODEL // tr,),
        in_specs=[pl.BlockSpec((tr, 2 * D_FF), lambda i: (i, 0))],
        out_specs=[out, out],
        out_shape=[jax.ShapeDtypeStruct((D_MODEL, D_FF_PAD), bf16)] * 2,
        compiler_params=_params(("arbitrary",)),
        name="cast_up",
    )(w_up)


def _trunk(x, s_re, s_im, sc_buf, ffn_buf, w, cfg):
    ns, length, _ = x.shape
    tm_in, sb_mix, lt_mix, sb_ffn, lt_ffn, lt_up, cb = cfg
    xf = x.reshape(ns * length, D_MODEL)
    u_a, gin, cv, gates = _in_proj(xf, w["norm1_g"], w["w_in"], tm_in)
    ya, n_re, n_im = _s5_mixer(u_a, s_re, s_im, w["s5"], cb)
    x1, h2, new_sc = _mix_out(ya.reshape(N_SLAB, ns, length, LANES), gin, cv, gates,
                              x, sc_buf, w["w_glu"], w["w_sc_out"], w["w_o"],
                              w["sc_conv_w"], w["sc_conv_b"], w["norm2_g"], sb_mix, lt_mix)
    y, nfv, nfg = _ffn(h2, x1, w["final_norm_g"], w["w_val"], w["w_gate"], w["w_down"],
                       w["ffn_conv_w"], w["ffn_conv_b"], _pad_ff(ffn_buf, 2),
                       sb_ffn, lt_ffn, lt_up)
    new_ffn = jnp.concatenate([nfv[:, -1, :, :D_FF], nfg[:, -1, :, :D_FF]], axis=-1)
    return y, n_re[None], n_im[None], new_sc[None], new_ffn[None]


def kernel(x_prompt, x_sample, state_s5_re, state_s5_im, cache_sc_conv, cache_ffn_conv, norm1_g, w_in, lam_re, lam_im, log_dt, b_re, b_im, c_re, c_im, d_skip, w_glu, sc_conv_w, sc_conv_b, w_sc_out, w_o, norm2_g, w_up, ffn_conv_w, ffn_conv_b, w_down, final_norm_g):
    w_val, w_gate = _cast_up(w_up[0])
    w = {
        "norm1_g": norm1_g[0].reshape(1, D_MODEL),
        "w_in": w_in[0].astype(bf16),
        "s5": _s5_tables(lam_re[0], lam_im[0], log_dt[0], b_re[0], b_im[0],
                         c_re[0], c_im[0], d_skip[0]),
        "w_glu": w_glu[0].astype(bf16),
        "sc_conv_w": sc_conv_w[0],
        "sc_conv_b": sc_conv_b[0].reshape(1, D_B),
        "w_sc_out": w_sc_out[0].astype(bf16),
        "w_o": w_o[0].astype(bf16),
        "norm2_g": norm2_g[0].reshape(1, D_MODEL),
        "w_val": w_val,
        "w_gate": w_gate,
        "ffn_conv_w": _pad_ff(ffn_conv_w[0], 1),
        "ffn_conv_b": _pad_ff(ffn_conv_b[0].reshape(1, 2 * D_FF), 1),
        "w_down": w_down[0].astype(bf16),
        "final_norm_g": final_norm_g.reshape(1, D_MODEL),
    }
    bp, lp, _ = x_prompt.shape
    bs, ls, _ = x_sample.shape
    zeros = lambda *s: jnp.zeros(s, f32)
    p_cfg = (256, 1, 256, 1, 256, 512, 64)
    yp, p_re, p_im, p_sc, p_ffn = _trunk(
        x_prompt, zeros(bp, G_A, N_STATE), zeros(bp, G_A, N_STATE),
        zeros(bp, 2, D_B), zeros(bp, 2, 2 * D_FF), w, p_cfg)
    s_cfg = (256, bs // 2, ls, bs // 2, ls, ls, 1)
    ys, s_re, s_im, s_sc, s_ffn = _trunk(
        x_sample, state_s5_re[0], state_s5_im[0], cache_sc_conv[0], cache_ffn_conv[0], w, s_cfg)
    return (yp, ys, p_re, p_im, p_sc, p_ffn, s_re, s_im, s_sc, s_ffn)
```

```python
import functools

import jax
import jax.numpy as jnp
import numpy as np
from jax import lax
from jax.experimental import pallas as pl
from jax.experimental.pallas import tpu as pltpu

D_MODEL = 2048
D_A = D_MODEL // 2
S5_GROUP = 16
G_A = D_A // S5_GROUP
N_STATE = 64
D_B = D_MODEL // 2
D_FF = 5504
EPS = 1e-6
IN_COLS = D_A + 3 * D_B + 2 * D_MODEL

LANES = 128
SUBLANES = 8
MXU_DIM = 256
VMEM_LIMIT_BYTES = 58 * 1024 * 1024

N_SLAB = D_A // LANES
SLAB_G = LANES // S5_GROUP
S5_T = 16
S5_TQ = S5_T * S5_GROUP
S5_K = S5_T * LANES
S5_ST = 2 * N_STATE
S5_SL = SLAB_G * S5_ST
D_FF_PAD = 5632
FFN_HALF = D_FF_PAD // 2
CONV_PAD = SUBLANES

bf16 = jnp.bfloat16
f32 = jnp.float32


def _rmsnorm(x, g):
    ms = jnp.mean(x * x, axis=-1, keepdims=True)
    return x * lax.rsqrt(ms + EPS) * g


def _params(sem):
    return pltpu.CompilerParams(dimension_semantics=sem, vmem_limit_bytes=VMEM_LIMIT_BYTES)


def _const_spec(shape):
    nd = len(shape)
    return pl.BlockSpec(shape, lambda *_: (0,) * nd, pipeline_mode=pl.Buffered(1))


def _dot(a, b):
    return jnp.dot(a, b, preferred_element_type=f32)


def _in_proj_kernel(x_ref, g_ref, w_ref, u_ref, gin_ref, cv_ref, gates_ref):
    h = _rmsnorm(x_ref[...], g_ref[...]).astype(bf16)
    g0 = D_A + 3 * D_B
    for n in range(2 * D_MODEL // D_B):
        acc = _dot(h, w_ref[:, g0 + n * D_B:g0 + (n + 1) * D_B])
        gates_ref[:, n * D_B:(n + 1) * D_B] = jax.nn.sigmoid(acc).astype(bf16)
    c = _dot(h, w_ref[:, D_A + D_B:D_A + 2 * D_B])
    v = _dot(h, w_ref[:, D_A + 2 * D_B:D_A + 3 * D_B])
    cv_ref[...] = (c * v).astype(bf16)
    gin_ref[...] = _dot(h, w_ref[:, D_A:D_A + D_B]).astype(bf16)
    u = _dot(h, w_ref[:, :D_A])
    for k in range(N_SLAB):
        u_ref[k] = u[:, k * LANES:(k + 1) * LANES]


def _in_proj(x, norm_g, w_in_b, tm):
    t = x.shape[0]
    return pl.pallas_call(
        _in_proj_kernel,
        grid=(t // tm,),
        in_specs=[
            pl.BlockSpec((tm, D_MODEL), lambda i: (i, 0)),
            _const_spec((1, D_MODEL)),
            _const_spec((D_MODEL, IN_COLS)),
        ],
        out_specs=[
            pl.BlockSpec((N_SLAB, tm, LANES), lambda i: (0, i, 0)),
            pl.BlockSpec((tm, D_B), lambda i: (i, 0)),
            pl.BlockSpec((tm, D_B), lambda i: (i, 0)),
            pl.BlockSpec((tm, 2 * D_MODEL), lambda i: (i, 0)),
        ],
        out_shape=[
            jax.ShapeDtypeStruct((N_SLAB, t, LANES), f32),
            jax.ShapeDtypeStruct((t, D_B), bf16),
            jax.ShapeDtypeStruct((t, D_B), bf16),
            jax.ShapeDtypeStruct((t, 2 * D_MODEL), bf16),
        ],
        compiler_params=_params(("arbitrary",)),
        name="in_proj",
    )(x, norm_g, w_in_b)


def _group_mask(shape, row0, row_shift, lane_shift):
    r = lax.broadcasted_iota(jnp.int32, shape, 0) + row0
    c = lax.broadcasted_iota(jnp.int32, shape, 1)
    return ((r >> row_shift) & (SLAB_G - 1)) == ((c >> lane_shift) & (SLAB_G - 1))


def _expand_tables(kc_ref, vc_ref, wc_refs, r_ref, m_scr, v_scr, w_scrs):
    rep = r_ref[...]
    bd = _dot(kc_ref[...], rep)
    bd = jnp.where(_group_mask(bd.shape, 0, 4, 4), bd, 0.0).astype(bf16)
    for t in range(S5_T):
        rows = slice(t * LANES, (t + 1) * LANES)
        if t:
            m_scr[rows, :t * LANES] = jnp.zeros((LANES, t * LANES), bf16)
        m_scr[rows, t * LANES:] = bd[:, :(S5_T - t) * LANES]
    step = MXU_DIM
    for r0 in range(0, S5_SL, step):
        vb = _dot(vc_ref[r0:r0 + step, :], rep)
        v_scr[r0:r0 + step, :] = jnp.where(_group_mask(vb.shape, r0, 7, 4), vb, 0.0).astype(bf16)
    for wc_ref, w_scr in zip(wc_refs, w_scrs):
        for r0 in range(0, S5_K, step):
            wt = jnp.concatenate([wc_ref[r0:r0 + step, :].astype(f32)] * SLAB_G, axis=1)
            w_scr[r0:r0 + step, :] = jnp.where(_group_mask(wt.shape, r0, 4, 7), wt, 0.0).astype(bf16)


def _s5_toeplitz(lhs, m_scr):
    return [_dot(lhs[:, :(a + 1) * MXU_DIM],
                 m_scr[:(a + 1) * MXU_DIM, a * MXU_DIM:(a + 1) * MXU_DIM])
            for a in range(S5_K // MXU_DIM)]


def _s5_outputs(toep, sprev, xs, v_scr, d_ref, store):
    per = MXU_DIM // LANES
    for a, part in enumerate(toep):
        y = part + _dot(sprev, v_scr[:, a * MXU_DIM:(a + 1) * MXU_DIM])
        for i in range(per):
            t = a * per + i
            piece = y[:, i * LANES:(i + 1) * LANES] + d_ref[...] * xs[t]
            store(t, jax.nn.gelu(piece))


def _s5_prompt_kernel(u_ref, kc_ref, vc_ref, wc_ref, r_ref, pq_ref, d_ref, h0_ref, h0s_ref,
                      y_ref, sfin_ref,
                      m_scr, v_scr, w_scr, s_scr, t_scr, loc_scr, loct_scr, prev_scr, *, nb, cb):
    ci = pl.program_id(1)

    @pl.when(ci == 0)
    def _():
        _expand_tables(kc_ref, vc_ref, [wc_ref], r_ref, m_scr, v_scr, [w_scr])
        s_scr[...] = h0_ref[...]
        t_scr[...] = h0s_ref[...]
        loc_scr[...] = jnp.zeros_like(loc_scr)
        loct_scr[...] = jnp.zeros_like(loct_scr)

    xs = [jnp.concatenate([u_ref[b, pl.ds(t, cb, stride=S5_T), :] for b in range(nb)], axis=0)
          for t in range(S5_T)]
    lhs = jnp.concatenate(xs, axis=1).astype(bf16)

    loc = _dot(lhs, w_scr[...])
    toep = _s5_toeplitz(lhs, m_scr)
    for j in range(SLAB_G):
        lj = loc[:, j * S5_ST:(j + 1) * S5_ST]
        ljt = pltpu.roll(lj, N_STATE, axis=1)
        for b in range(nb):
            rows = slice(b * cb, (b + 1) * cb)
            loc_scr[j, pl.ds(b, cb, stride=SUBLANES), :] = lj[rows]
            loct_scr[j, pl.ds(b, cb, stride=SUBLANES), :] = ljt[rows]

    ss = [s_scr[j] for j in range(SLAB_G)]
    ts = [t_scr[j] for j in range(SLAB_G)]
    for c in range(cb):
        rows = slice(c * SUBLANES, (c + 1) * SUBLANES)
        for j in range(SLAB_G):
            p = pq_ref[0, j:j + 1, :]
            q = pq_ref[1, j:j + 1, :]
            prev_scr[j, rows, :] = ss[j]
            ss[j], ts[j] = (p * ss[j] + q * ts[j] + loc_scr[j, rows, :],
                            p * ts[j] - q * ss[j] + loct_scr[j, rows, :])
    for j in range(SLAB_G):
        s_scr[j] = ss[j]
        t_scr[j] = ts[j]
        sfin_ref[j] = ss[j]

    sprev = jnp.concatenate(
        [jnp.concatenate([prev_scr[j, pl.ds(b, cb, stride=SUBLANES), :] for b in range(nb)], axis=0)
         for j in range(SLAB_G)], axis=1).astype(bf16)

    def store(t, val):
        for b in range(nb):
            y_ref[b, pl.ds(t, cb, stride=S5_T), :] = val[b * cb:(b + 1) * cb]

    _s5_outputs(toep, sprev, xs, v_scr, d_ref, store)


def _s5_prompt(u, tables, h0, h0s, cb):
    kc, vc, wc, _, rep, pq, dvec = tables
    _, nb, length, _ = u.shape
    nc = length // S5_T
    kern = functools.partial(_s5_prompt_kernel, nb=nb, cb=cb)
    tok = pl.BlockSpec((None, nb, cb * S5_T, LANES), lambda k, ci: (k, 0, ci, 0))
    per_slab = lambda *shape: pl.BlockSpec((None,) + shape, lambda k, ci: (k,) + (0,) * len(shape))
    state = per_slab(SLAB_G, SUBLANES, S5_ST)
    return pl.pallas_call(
        kern,
        grid=(N_SLAB, nc // cb),
        in_specs=[tok, per_slab(LANES, S5_TQ), per_slab(S5_SL, S5_TQ), per_slab(S5_K, S5_ST),
                  _const_spec((S5_TQ, S5_K)),
                  per_slab(2, SLAB_G, S5_ST), per_slab(1, LANES), state, state],
        out_specs=[tok, state],
        out_shape=[
            jax.ShapeDtypeStruct(u.shape, f32),
            jax.ShapeDtypeStruct((N_SLAB, SLAB_G, SUBLANES, S5_ST), f32),
        ],
        scratch_shapes=[
            pltpu.VMEM((S5_K, S5_K), bf16),
            pltpu.VMEM((S5_SL, S5_K), bf16),
            pltpu.VMEM((S5_K, S5_SL), bf16),
            pltpu.VMEM((SLAB_G, SUBLANES, S5_ST), f32),
            pltpu.VMEM((SLAB_G, SUBLANES, S5_ST), f32),
            pltpu.VMEM((SLAB_G, cb * SUBLANES, S5_ST), f32),
            pltpu.VMEM((SLAB_G, cb * SUBLANES, S5_ST), f32),
            pltpu.VMEM((SLAB_G, cb * SUBLANES, S5_ST), f32),
        ],
        compiler_params=_params(("arbitrary", "arbitrary")),
        name="s5_prompt",
    )(u, kc, vc, wc, rep, pq, dvec, h0, h0s)


def _s5_sample_kernel(u_ref, kc_ref, vc_ref, wc_ref, wlc_ref, r_ref, pq_ref, d_ref, h0_ref, h0s_ref,
                      y_ref, sfin_ref, m_scr, v_scr, wh_scr, wl_scr, *, nb):
    _expand_tables(kc_ref, vc_ref, [wc_ref, wlc_ref], r_ref, m_scr, v_scr, [wh_scr, wl_scr])
    xs = [u_ref[pl.ds(t, nb, stride=S5_T), :] for t in range(S5_T)]
    lhs_f = jnp.concatenate(xs, axis=1)
    lhs = lhs_f.astype(bf16)
    lhs_lo = (lhs_f - lhs.astype(f32)).astype(bf16)
    wh = wh_scr[...]
    loc = _dot(lhs, wh) + _dot(lhs_lo, wh) + _dot(lhs, wl_scr[...])
    for j in range(SLAB_G):
        p = pq_ref[0, j:j + 1, :]
        q = pq_ref[1, j:j + 1, :]
        sfin_ref[j] = p * h0_ref[j] + q * h0s_ref[j] + loc[:, j * S5_ST:(j + 1) * S5_ST]
    sprev = jnp.concatenate([h0_ref[j] for j in range(SLAB_G)], axis=1).astype(bf16)

    def store(t, val):
        y_ref[pl.ds(t, nb, stride=S5_T), :] = val

    _s5_outputs(_s5_toeplitz(lhs, m_scr), sprev, xs, v_scr, d_ref, store)


def _s5_sample(u, tables, h0, h0s, nb):
    kc, vc, wc, wlc, rep, pq, dvec = tables
    kern = functools.partial(_s5_sample_kernel, nb=nb)
    per_slab = lambda *shape: pl.BlockSpec((None,) + shape, lambda k: (k,) + (0,) * len(shape))
    tok = per_slab(nb * S5_T, LANES)
    state = per_slab(SLAB_G, nb, S5_ST)
    return pl.pallas_call(
        kern,
        grid=(N_SLAB,),
        in_specs=[tok, per_slab(LANES, S5_TQ), per_slab(S5_SL, S5_TQ), per_slab(S5_K, S5_ST),
                  per_slab(S5_K, S5_ST), _const_spec((S5_TQ, S5_K)),
                  per_slab(2, SLAB_G, S5_ST), per_slab(1, LANES), state, state],
        out_specs=[tok, state],
        out_shape=[
            jax.ShapeDtypeStruct(u.shape, f32),
            jax.ShapeDtypeStruct((N_SLAB, SLAB_G, nb, S5_ST), f32),
        ],
        scratch_shapes=[
            pltpu.VMEM((S5_K, S5_K), bf16),
            pltpu.VMEM((S5_SL, S5_K), bf16),
            pltpu.VMEM((S5_K, S5_SL), bf16),
            pltpu.VMEM((S5_K, S5_SL), bf16),
        ],
        compiler_params=_params(("arbitrary",)),
        name="s5_sample",
    )(u, kc, vc, wc, wlc, rep, pq, dvec, h0, h0s)


def _replication_matrix():
    src = np.arange(S5_TQ)
    dst = np.arange(S5_K)
    same = ((src[:, None] // S5_GROUP == dst[None, :] // LANES)
            & (src[:, None] % S5_GROUP == dst[None, :] % S5_GROUP))
    return jnp.asarray(same, dtype=bf16)


def _s5_tables(lam_re, lam_im, log_dt, b_re, b_im, c_re, c_im, d_skip):
    hp = lax.Precision.HIGHEST
    dt = jnp.exp(log_dt.astype(f32))[:, None]
    lr, li = lam_re.astype(f32), lam_im.astype(f32)
    mag = jnp.exp(lr * dt)
    ab_re = mag * jnp.cos(li * dt)
    ab_im = mag * jnp.sin(li * dt)
    den = lr * lr + li * li
    nr, ni = ab_re - 1.0, ab_im
    f_re = (nr * lr + ni * li) / den
    f_im = (ni * lr - nr * li) / den
    br, bi = b_re.astype(f32), b_im.astype(f32)
    bb_re = f_re[..., None] * br - f_im[..., None] * bi
    bb_im = f_re[..., None] * bi + f_im[..., None] * br
    cr, ci = c_re.astype(f32), c_im.astype(f32)

    k = jnp.arange(S5_T + 1, dtype=f32)[:, None, None]
    pm = jnp.exp(lr * dt * k)
    pr = pm * jnp.cos(li * dt * k)
    pi = pm * jnp.sin(li * dt * k)

    lag = np.arange(S5_TQ) // S5_GROUP
    col = np.arange(S5_TQ) % S5_GROUP
    fill_q = jnp.asarray(np.arange(S5_GROUP)[:, None] == col[None, :], dtype=f32)
    fill_t = jnp.asarray(np.arange(S5_T)[:, None] == lag[None, :], dtype=f32)
    rows = G_A * N_STATE

    def on_lanes(a, fill):
        return jnp.dot(a.reshape(rows, a.shape[-1]), fill, precision=hp)

    ct_re = on_lanes(cr.transpose(0, 2, 1), fill_q)
    ct_im = on_lanes(ci.transpose(0, 2, 1), fill_q)

    def c_times_powers(p_re, p_im):
        pt_re = on_lanes(p_re.transpose(1, 2, 0), fill_t)
        pt_im = on_lanes(p_im.transpose(1, 2, 0), fill_t)
        return ((ct_re * pt_re - ct_im * pt_im).reshape(G_A, N_STATE, S5_TQ),
                (ct_re * pt_im + ct_im * pt_re).reshape(G_A, N_STATE, S5_TQ))

    e_re, e_im = c_times_powers(pr[:S5_T], pi[:S5_T])
    kk = jnp.einsum('gpn,gnx->gpx',
                    jnp.concatenate([bb_re.transpose(0, 2, 1), -bb_im.transpose(0, 2, 1)], axis=2),
                    jnp.concatenate([e_re, e_im], axis=1), precision=hp)
    kc = kk.reshape(N_SLAB, LANES, S5_TQ).astype(bf16)

    def slab_time(a):
        return a.reshape(S5_T, N_SLAB, SLAB_G, 1, N_STATE).transpose(1, 0, 2, 3, 4)

    back = S5_T - 1 - jnp.arange(S5_T)
    prr, pir = slab_time(pr[back]), slab_time(pi[back])
    bt_re = bb_re.transpose(0, 2, 1).reshape(N_SLAB, 1, SLAB_G, S5_GROUP, N_STATE)
    bt_im = bb_im.transpose(0, 2, 1).reshape(N_SLAB, 1, SLAB_G, S5_GROUP, N_STATE)
    lanes = lambda a, b: jnp.concatenate([a, b], axis=-1)
    w = lanes(prr, prr) * lanes(bt_re, bt_im) + lanes(-pir, pir) * lanes(bt_im, bt_re)
    w = w.reshape(N_SLAB, S5_K, S5_ST)
    wc = w.astype(bf16)
    wlc = (w - wc.astype(f32)).astype(bf16)

    v_re, v_im = c_times_powers(pr[1:], pi[1:])
    vc = jnp.concatenate([v_re, -v_im], axis=1).reshape(N_SLAB, S5_SL, S5_TQ).astype(bf16)

    ar, ai = pr[S5_T], pi[S5_T]
    p_row = jnp.concatenate([ar, ar], axis=1).reshape(N_SLAB, 1, SLAB_G, S5_ST)
    q_row = jnp.concatenate([-ai, ai], axis=1).reshape(N_SLAB, 1, SLAB_G, S5_ST)
    pq = jnp.concatenate([p_row, q_row], axis=1)
    dvec = d_skip.astype(f32).reshape(N_SLAB, 1, LANES)
    return kc, vc, wc, wlc, _replication_matrix(), pq, dvec


def _s5_mixer(u, s_re, s_im, tables, cb):
    nb = s_re.shape[0]
    length = u.shape[1] // nb

    def to_slab(a):
        return a.reshape(nb, N_SLAB, SLAB_G, S5_ST).transpose(1, 2, 0, 3)

    h0 = to_slab(jnp.concatenate([s_re, s_im], axis=-1))
    h0s = to_slab(jnp.concatenate([s_im, s_re], axis=-1))
    if length == S5_T:
        y, sfin = _s5_sample(u, tables, h0, h0s, nb)
    else:
        pad = [(0, 0), (0, 0), (0, SUBLANES - nb), (0, 0)]
        y, sfin = _s5_prompt(u.reshape(N_SLAB, nb, length, LANES), tables,
                             jnp.pad(h0, pad), jnp.pad(h0s, pad), cb)
        y = y.reshape(u.shape)
        sfin = sfin[:, :, :nb]
    sfin = sfin.transpose(2, 0, 1, 3).reshape(nb, G_A, S5_ST)
    return y, sfin[..., :N_STATE], sfin[..., N_STATE:]


def _conv3(ext_ref, w_ref, b_ref, lt, cols=slice(None)):
    lo = CONV_PAD - 2
    out = ext_ref[:, lo:lo + lt, cols] * w_ref[0:1, cols]
    out = out + ext_ref[:, lo + 1:lo + 1 + lt, cols] * w_ref[1:2, cols]
    out = out + ext_ref[:, lo + 2:lo + 2 + lt, cols] * w_ref[2:3, cols]
    return out + b_ref[:, cols]


PERM_BLOCK = SUBLANES * SUBLANES


def _swap_rows(val, scr):
    rows, width = val.shape
    nsl = width // LANES
    for k in range(nsl):
        scr[k] = val[:, k * LANES:(k + 1) * LANES]
    return jnp.concatenate(
        [jnp.concatenate([scr[k, pl.ds(r0 + m, SUBLANES, stride=SUBLANES), :]
                          for r0 in range(0, rows, PERM_BLOCK) for m in range(SUBLANES)], axis=0)
         for k in range(nsl)], axis=1)


def _mix_kernel(ya_ref, gin_ref, cv_ref, gates_ref, x_ref, cache_ref,
                wglu_ref, wsc_ref, wo_ref, cw_ref, cb_ref, g2_ref,
                x1_ref, h2_ref, nsc_ref, ext_scr, mrg_scr, *perm_scr, sb, lt, nchunk):
    i = pl.program_id(1)
    rows = sb * lt
    lo = CONV_PAD - 2

    @pl.when(i == 0)
    def _():
        ext_scr[:, lo:CONV_PAD, :] = cache_ref[...]

    @pl.when(i > 0)
    def _():
        ext_scr[:, lo:CONV_PAD, :] = ext_scr[:, lo + lt:CONV_PAD + lt, :]

    ext_scr[:, CONV_PAD:, :] = cv_ref[...].astype(f32).reshape(sb, lt, D_B)
    nsc_ref[...] = ext_scr[:, lo + lt:CONV_PAD + lt, :]

    conv = _conv3(ext_scr, cw_ref, cb_ref, lt).reshape(rows, D_B)
    gated = (gin_ref[...].astype(f32) * conv).astype(bf16)
    ya = jnp.concatenate([ya_ref[k].reshape(rows, LANES) for k in range(N_SLAB)],
                         axis=1).astype(bf16)
    wc = D_MODEL // nchunk
    for n in range(nchunk):
        cols = slice(n * wc, (n + 1) * wc)
        gcols = slice(D_MODEL + n * wc, D_MODEL + (n + 1) * wc)
        a = _dot(ya, wglu_ref[:, cols])
        gt = _dot(ya, wglu_ref[:, gcols])
        br_a = a * jax.nn.sigmoid(gt)
        br_b = _dot(gated, wsc_ref[:, cols])
        g_a = gates_ref[:, cols].astype(f32)
        g_b = gates_ref[:, gcols].astype(f32)
        mrg_scr[:, cols] = (g_a * br_a + g_b * br_b).astype(bf16)
    x1 = x_ref[...].reshape(rows, D_MODEL) + _dot(mrg_scr[...], wo_ref[...])
    x1_ref[...] = x1.reshape(sb, lt, D_MODEL)
    h2 = _rmsnorm(x1, g2_ref[...])
    if perm_scr:
        h2 = _swap_rows(h2, perm_scr[0])
    h2_ref[...] = h2.astype(bf16)


def _swapped_rows(sb, lt):
    return sb == 1 and lt % PERM_BLOCK == 0


def _swap_scratch(rows, width):
    return pltpu.VMEM((width // LANES, rows, LANES), f32)


def _mix_out(ya, gin, cv, gates, x, cache, wglu, wsc, wo, cw, cbias, g2, sb, lt):
    ns, length, _ = x.shape
    nt = length // lt
    kern = functools.partial(_mix_kernel, sb=sb, lt=lt, nchunk=4)
    perm = [_swap_scratch(sb * lt, D_MODEL)] if _swapped_rows(sb, lt) else []
    tile = lambda c: pl.BlockSpec((sb, lt, c), lambda s, i: (s, i, 0))
    flat = lambda c: pl.BlockSpec((sb * lt, c), lambda s, i: (s * nt + i, 0))
    return pl.pallas_call(
        kern,
        grid=(ns // sb, nt),
        in_specs=[
            pl.BlockSpec((N_SLAB, sb, lt, LANES), lambda s, i: (0, s, i, 0)),
            flat(D_B), flat(D_B), flat(2 * D_MODEL), tile(D_MODEL),
            pl.BlockSpec((sb, 2, D_B), lambda s, i: (s, 0, 0)),
            _const_spec((D_A, 2 * D_MODEL)),
            _const_spec((D_B, D_MODEL)),
            _const_spec((D_MODEL, D_MODEL)),
            _const_spec((3, D_B)),
            _const_spec((1, D_B)),
            _const_spec((1, D_MODEL)),
        ],
        out_specs=[
            tile(D_MODEL),
            flat(D_MODEL),
            pl.BlockSpec((sb, 2, D_B), lambda s, i: (s, 0, 0)),
        ],
        out_shape=[
            jax.ShapeDtypeStruct((ns, length, D_MODEL), f32),
            jax.ShapeDtypeStruct((ns * length, D_MODEL), bf16),
            jax.ShapeDtypeStruct((ns, 2, D_B), f32),
        ],
        scratch_shapes=[
            pltpu.VMEM((sb, lt + CONV_PAD, D_B), f32),
            pltpu.VMEM((sb * lt, D_MODEL), bf16),
        ] + perm,
        compiler_params=_params(("arbitrary", "arbitrary")),
        name="mix_out",
    )(ya, gin, cv, gates, x, cache, wglu, wsc, wo, cw, cbias, g2)


def _ffn_up_kernel(h2_ref, wv_ref, wg_ref, cwv_ref, cwg_ref, cbv_ref, cbg_ref,
                   cachev_ref, cacheg_ref, act_ref, nfv_ref, nfg_ref, extv_scr, extg_scr,
                   *, sb, lt):
    i = pl.program_id(2)
    rows = sb * lt
    lo = CONV_PAD - 2

    @pl.when(i == 0)
    def _():
        extv_scr[:, lo:CONV_PAD, :] = cachev_ref[...]
        extg_scr[:, lo:CONV_PAD, :] = cacheg_ref[...]

    @pl.when(i > 0)
    def _():
        extv_scr[:, lo:CONV_PAD, :] = extv_scr[:, lo + lt:CONV_PAD + lt, :]
        extg_scr[:, lo:CONV_PAD, :] = extg_scr[:, lo + lt:CONV_PAD + lt, :]

    h2 = h2_ref[...]
    extv_scr[:, CONV_PAD:, :] = _dot(h2, wv_ref[...]).reshape(sb, lt, FFN_HALF)
    extg_scr[:, CONV_PAD:, :] = _dot(h2, wg_ref[...]).reshape(sb, lt, FFN_HALF)
    for c in range(FFN_HALF // MXU_DIM):
        cols = slice(c * MXU_DIM, (c + 1) * MXU_DIM)
        val = _conv3(extv_scr, cwv_ref, cbv_ref, lt, cols)
        gt = _conv3(extg_scr, cwg_ref, cbg_ref, lt, cols)
        act_ref[:, cols] = (jax.nn.silu(gt) * val).reshape(rows, MXU_DIM).astype(bf16)
    nfv_ref[...] = extv_scr[:, lo + lt:CONV_PAD + lt, :]
    nfg_ref[...] = extg_scr[:, lo + lt:CONV_PAD + lt, :]


def _conv3_swapped(x, car_ref, w_ref, b_ref, cols):
    x6, x7 = x[:, SUBLANES - 2], x[:, SUBLANES - 1]
    prev6 = jnp.concatenate([car_ref[0, :, cols][None], x6[:-1]], axis=0)
    prev7 = jnp.concatenate([car_ref[1, :, cols][None], x7[:-1]], axis=0)
    car_ref[0, :, cols] = x6[-1]
    car_ref[1, :, cols] = x7[-1]
    first = lax.broadcasted_iota(jnp.int32, x6.shape, 1) == 0
    back2 = jnp.where(first, pltpu.roll(prev6, 1, axis=1), pltpu.roll(x6, 1, axis=1))
    back1 = jnp.where(first, pltpu.roll(prev7, 1, axis=1), pltpu.roll(x7, 1, axis=1))
    s1 = jnp.concatenate([back1[:, None], x[:, :SUBLANES - 1]], axis=1)
    s2 = jnp.concatenate([back2[:, None], back1[:, None], x[:, :SUBLANES - 2]], axis=1)
    return (s2 * w_ref[0:1, cols] + s1 * w_ref[1:2, cols] + x * w_ref[2:3, cols]
            + b_ref[:, cols])


def _ffn_up_swapped_kernel(h2_ref, wv_ref, wg_ref, cwv_ref, cwg_ref, cbv_ref, cbg_ref,
                           cachev_ref, cacheg_ref, act_ref, nfv_ref, nfg_ref,
                           carv_scr, carg_scr, *, lt):
    i = pl.program_id(2)
    nblk = lt // PERM_BLOCK

    @pl.when(i == 0)
    def _():
        for car, cache in ((carv_scr, cachev_ref), (carg_scr, cacheg_ref)):
            for r in range(2):
                car[r] = jnp.broadcast_to(cache[0, r:r + 1, :], (SUBLANES, FFN_HALF))

    h2 = h2_ref[...]
    upv = _dot(h2, wv_ref[...])
    upg = _dot(h2, wg_ref[...])
    for c in range(FFN_HALF // MXU_DIM):
        cols = slice(c * MXU_DIM, (c + 1) * MXU_DIM)
        shape = (nblk, SUBLANES, SUBLANES, MXU_DIM)
        val = _conv3_swapped(upv[:, cols].reshape(shape), carv_scr, cwv_ref, cbv_ref, cols)
        gt = _conv3_swapped(upg[:, cols].reshape(shape), carg_scr, cwg_ref, cbg_ref, cols)
        act_ref[:, cols] = (jax.nn.silu(gt) * val).reshape(lt, MXU_DIM).astype(bf16)
    for car, nf_ref in ((carv_scr, nfv_ref), (carg_scr, nfg_ref)):
        for r in range(2):
            nf_ref[0, r:r + 1, :] = car[r, SUBLANES - 1:SUBLANES, :]


def _ffn_down_kernel(act_ref, x1_ref, wd_ref, gf_ref, y_ref, *swap_scr, sb, lt):
    rows = sb * lt
    delta = _dot(act_ref[:, :D_FF], wd_ref[...])
    if swap_scr:
        delta = _swap_rows(delta, swap_scr[0])
    x2 = x1_ref[...].reshape(rows, D_MODEL) + delta
    y_ref[...] = _rmsnorm(x2, gf_ref[...]).reshape(sb, lt, D_MODEL)


def _ffn(h2, x1, gfin, w_val_p, w_gate_p, w_down_p, cw_p, cb_p, cache_p, sb, lt, lt_up):
    ns, length, _ = x1.shape
    nh = D_FF_PAD // FFN_HALF
    nt = length // lt
    ntu = length // lt_up
    resident = dict(pipeline_mode=pl.Buffered(1))
    val = lambda rows, **kw: pl.BlockSpec((rows, FFN_HALF), lambda h, s, i: (0, h), **kw)
    gate = lambda rows, **kw: pl.BlockSpec((rows, FFN_HALF), lambda h, s, i: (0, nh + h), **kw)
    swapped = _swapped_rows(sb, lt)
    if swapped:
        assert _swapped_rows(sb, lt_up)
        up_kernel = functools.partial(_ffn_up_swapped_kernel, lt=lt_up)
        up_scratch = [pltpu.VMEM((2, SUBLANES, FFN_HALF), f32)] * 2
        down_scratch = [_swap_scratch(sb * lt, D_MODEL)]
    else:
        up_kernel = functools.partial(_ffn_up_kernel, sb=sb, lt=lt_up)
        up_scratch = [pltpu.VMEM((sb, lt_up + CONV_PAD, FFN_HALF), f32)] * 2
        down_scratch = []
    act, nfv, nfg = pl.pallas_call(
        up_kernel,
        grid=(nh, ns // sb, ntu),
        in_specs=[
            pl.BlockSpec((sb * lt_up, D_MODEL), lambda h, s, i: (s * ntu + i, 0)),
            val(D_MODEL, **resident), val(D_MODEL, **resident),
            val(3), gate(3), val(1), gate(1),
            pl.BlockSpec((sb, 2, FFN_HALF), lambda h, s, i: (s, 0, h)),
            pl.BlockSpec((sb, 2, FFN_HALF), lambda h, s, i: (s, 0, nh + h)),
        ],
        out_specs=[
            pl.BlockSpec((sb * lt_up, FFN_HALF), lambda h, s, i: (s * ntu + i, h)),
            pl.BlockSpec((sb, None, 2, FFN_HALF), lambda h, s, i: (s, i, 0, h)),
            pl.BlockSpec((sb, None, 2, FFN_HALF), lambda h, s, i: (s, i, 0, h)),
        ],
        out_shape=[
            jax.ShapeDtypeStruct((ns * length, D_FF_PAD), bf16),
            jax.ShapeDtypeStruct((ns, ntu, 2, D_FF_PAD), f32),
            jax.ShapeDtypeStruct((ns, ntu, 2, D_FF_PAD), f32),
        ],
        scratch_shapes=up_scratch,
        compiler_params=_params(("arbitrary", "arbitrary", "arbitrary")),
        name="ffn_up",
    )(h2, w_val_p, w_gate_p, cw_p, cw_p, cb_p, cb_p, cache_p, cache_p)
    y = pl.pallas_call(
        functools.partial(_ffn_down_kernel, sb=sb, lt=lt),
        grid=(ns // sb, nt),
        in_specs=[
            pl.BlockSpec((sb * lt, D_FF_PAD), lambda s, i: (s * nt + i, 0)),
            pl.BlockSpec((sb, lt, D_MODEL), lambda s, i: (s, i, 0)),
            _const_spec((D_FF, D_MODEL)),
            _const_spec((1, D_MODEL)),
        ],
        out_specs=pl.BlockSpec((sb, lt, D_MODEL), lambda s, i: (s, i, 0)),
        out_shape=jax.ShapeDtypeStruct((ns, length, D_MODEL), f32),
        scratch_shapes=down_scratch,
        compiler_params=_params(("arbitrary", "arbitrary")),
        name="ffn_down",
    )(act, x1, w_down_p, gfin)
    return y, nfv, nfg


def _pad_ff(a, axis):
    pad = [(0, 0)] * a.ndim
    pad[axis] = (0, D_FF_PAD - D_FF)
    if a.shape[axis] == D_FF:
        return jnp.pad(a, pad)
    lo, hi = jnp.split(a, 2, axis=axis)
    return jnp.concatenate([jnp.pad(lo, pad), jnp.pad(hi, pad)], axis=axis)


def _cast_up_kernel(w_ref, val_ref, gate_ref):
    zeros = jnp.zeros((w_ref.shape[0], D_FF_PAD - D_FF), bf16)
    val_ref[:, :D_FF] = w_ref[:, :D_FF].astype(bf16)
    val_ref[:, D_FF:] = zeros
    gate_ref[:, :D_FF] = w_ref[:, D_FF:].astype(bf16)
    gate_ref[:, D_FF:] = zeros


def _cast_up(w_up, tr=256):
    out = pl.BlockSpec((tr, D_FF_PAD), lambda i: (i, 0))
    return pl.pallas_call(
        _cast_up_kernel,
        grid=(D_MODEL // tr,),
        in_specs=[pl.BlockSpec((tr, 2 * D_FF), lambda i: (i, 0))],
        out_specs=[out, out],
        out_shape=[jax.ShapeDtypeStruct((D_MODEL, D_FF_PAD), bf16)] * 2,
        compiler_params=_params(("arbitrary",)),
        name="cast_up",
    )(w_up)


def _trunk(x, s_re, s_im, sc_buf, ffn_buf, w, cfg):
    ns, length, _ = x.shape
    tm_in, sb_mix, lt_mix, sb_ffn, lt_ffn, lt_up, cb = cfg
    xf = x.reshape(ns * length, D_MODEL)
    u_a, gin, cv, gates = _in_proj(xf, w["norm1_g"], w["w_in"], tm_in)
    ya, n_re, n_im = _s5_mixer(u_a, s_re, s_im, w["s5"], cb)
    x1, h2, new_sc = _mix_out(ya.reshape(N_SLAB, ns, length, LANES), gin, cv, gates,
                              x, sc_buf, w["w_glu"], w["w_sc_out"], w["w_o"],
                              w["sc_conv_w"], w["sc_conv_b"], w["norm2_g"], sb_mix, lt_mix)
    y, nfv, nfg = _ffn(h2, x1, w["final_norm_g"], w["w_val"], w["w_gate"], w["w_down"],
                       w["ffn_conv_w"], w["ffn_conv_b"], _pad_ff(ffn_buf, 2),
                       sb_ffn, lt_ffn, lt_up)
    new_ffn = jnp.concatenate([nfv[:, -1, :, :D_FF], nfg[:, -1, :, :D_FF]], axis=-1)
    return y, n_re[None], n_im[None], new_sc[None], new_ffn[None]


def kernel(x_prompt, x_sample, state_s5_re, state_s5_im, cache_sc_conv, cache_ffn_conv, norm1_g, w_in, lam_re, lam_im, log_dt, b_re, b_im, c_re, c_im, d_skip, w_glu, sc_conv_w, sc_conv_b, w_sc_out, w_o, norm2_g, w_up, ffn_conv_w, ffn_conv_b, w_down, final_norm_g):
    w_val, w_gate = _cast_up(w_up[0])
    w = {
        "norm1_g": norm1_g[0].reshape(1, D_MODEL),
        "w_in": w_in[0].astype(bf16),
        "s5": _s5_tables(lam_re[0], lam_im[0], log_dt[0], b_re[0], b_im[0],
                         c_re[0], c_im[0], d_skip[0]),
        "w_glu": w_glu[0].astype(bf16),
        "sc_conv_w": sc_conv_w[0],
        "sc_conv_b": sc_conv_b[0].reshape(1, D_B),
        "w_sc_out": w_sc_out[0].astype(bf16),
        "w_o": w_o[0].astype(bf16),
        "norm2_g": norm2_g[0].reshape(1, D_MODEL),
        "w_val": w_val,
        "w_gate": w_gate,
        "ffn_conv_w": _pad_ff(ffn_conv_w[0], 1),
        "ffn_conv_b": _pad_ff(ffn_conv_b[0].reshape(1, 2 * D_FF), 1),
        "w_down": w_down[0].astype(bf16),
        "final_norm_g": final_norm_g.reshape(1, D_MODEL),
    }
    bp, lp, _ = x_prompt.shape
    bs, ls, _ = x_sample.shape
    zeros = lambda *s: jnp.zeros(s, f32)
    p_cfg = (256, 1, 256, 1, 256, 512, 64)
    yp, p_re, p_im, p_sc, p_ffn = _trunk(
        x_prompt, zeros(bp, G_A, N_STATE), zeros(bp, G_A, N_STATE),
        zeros(bp, 2, D_B), zeros(bp, 2, 2 * D_FF), w, p_cfg)
    s_cfg = (256, bs // 2, ls, bs // 2, ls, ls, 1)
    ys, s_re, s_im, s_sc, s_ffn = _trunk(
        x_sample, state_s5_re[0], state_s5_im[0], cache_sc_conv[0], cache_ffn_conv[0], w, s_cfg)
    return (yp, ys, p_re, p_im, p_sc, p_ffn, s_re, s_im, s_sc, s_ffn)
```
